```python
import math
import jax, jax.numpy as jnp
from jax import lax
import numpy as np

D_MODEL = 1024
BATCH = 4
SEQ = 4096
DEPTH = 1

DA_HEADS = 8
DA_HEAD_DIM = D_MODEL // DA_HEADS // 2
DA_V_DIM = 2 * DA_HEAD_DIM
DA_QK_WIDTH = DA_HEADS * 2 * DA_HEAD_DIM
DA_V_WIDTH = DA_HEADS * DA_V_DIM
Q_BLOCK = 128
ML_HEADS = 4
ML_V_DIM = D_MODEL // ML_HEADS
ML_QK_DIM = ML_V_DIM // 2
ML_QK_WIDTH = ML_HEADS * ML_QK_DIM
ML_V_WIDTH = ML_HEADS * ML_V_DIM
ML_CHUNK = 64
CONV_WIDTH = 4
N_BRANCHES = 2
D_FF = 4 * D_MODEL
EPS = 1e-6

SPLIT_SIZES = (DA_QK_WIDTH, DA_QK_WIDTH, DA_V_WIDTH,
               ML_QK_WIDTH, ML_QK_WIDTH, ML_V_WIDTH, 2 * ML_HEADS, ML_V_WIDTH,
               N_BRANCHES * D_MODEL)
D_IN = int(sum(SPLIT_SIZES))
SPLIT_POINTS = [int(s) for s in np.cumsum(SPLIT_SIZES)[:-1]]

kernel_name = "hybrid_gated_diffattn_mlstm_block"


def rms_norm(x, g):
    xf = x.astype(jnp.float32)
    y = xf * lax.rsqrt(jnp.mean(xf * xf, axis=-1, keepdims=True) + EPS)
    return (y * g.astype(jnp.float32)).astype(x.dtype)


def head_rms(x):
    return x * lax.rsqrt(jnp.mean(x * x, axis=-1, keepdims=True) + EPS)


def alibi_slopes(n_heads):
    return 2.0 ** (-8.0 * jnp.arange(1, n_heads + 1, dtype=jnp.float32) / n_heads)


def causal_conv(u, w, b):
    S = u.shape[1]
    up = jnp.pad(u, ((0, 0), (CONV_WIDTH - 1, 0), (0, 0)))
    out = sum(up[:, j:j + S] * w[j] for j in range(CONV_WIDTH))
    return out + b


def diff_attention(q, k, v, lam, lam_init, norm_g):
    B, S, _ = q.shape
    f32 = jnp.float32
    q = q.astype(f32).reshape(B, S, DA_HEADS, 2, DA_HEAD_DIM).transpose(0, 2, 3, 1, 4) * (DA_HEAD_DIM ** -0.5)
    k = k.astype(f32).reshape(B, S, DA_HEADS, 2, DA_HEAD_DIM).transpose(0, 2, 3, 1, 4)
    vf = v.astype(f32).reshape(B, S, DA_HEADS, DA_V_DIM).transpose(0, 2, 1, 3)
    lamf = lam.astype(f32)
    lam_full = jnp.exp(jnp.sum(lamf[0] * lamf[1])) - jnp.exp(jnp.sum(lamf[2] * lamf[3])) + lam_init
    slopes = alibi_slopes(DA_HEADS)
    key_pos = jnp.arange(S)
    n_blocks = S // Q_BLOCK
    q_blocks = q.reshape(B, DA_HEADS, 2, n_blocks, Q_BLOCK, DA_HEAD_DIM).transpose(3, 0, 1, 2, 4, 5)
    starts = jnp.arange(n_blocks) * Q_BLOCK

    def one_block(args):
        qb, start = args
        qpos = start + jnp.arange(Q_BLOCK)
        dist = qpos[:, None] - key_pos[None, :]
        causal = dist >= 0
        bias = -slopes[:, None, None] * dist.astype(f32)
        s = jnp.einsum('bhmqd,bhmkd->bhmqk', qb, k) + bias[None, :, None]
        s = jnp.where(causal, s, -jnp.inf)
        p = jax.nn.softmax(s, axis=-1)
        a = p[:, :, 0] - lam_full * p[:, :, 1]
        return jnp.einsum('bhqk,bhkd->bhqd', a, vf)

    o = lax.map(one_block, (q_blocks, starts))
    o = o.transpose(1, 0, 3, 2, 4).reshape(B, S, DA_HEADS, DA_V_DIM)
    o = head_rms(o) * (1.0 - lam_init)
    o = o.reshape(B, S, DA_V_WIDTH) * norm_g.astype(f32)
    return o.astype(v.dtype)


def mlstm(q, k, v, i_pre, f_pre, o_pre, norm_g):
    B, S, _ = q.shape
    f32 = jnp.float32
    nc = S // ML_CHUNK

    def heads(t, d):
        return t.astype(f32).reshape(B, nc, ML_CHUNK, ML_HEADS, d).transpose(1, 0, 3, 2, 4)

    def gates(t):
        return t.astype(f32).reshape(B, nc, ML_CHUNK, ML_HEADS).transpose(1, 0, 3, 2)

    qc = heads(q, ML_QK_DIM) * (ML_QK_DIM ** -0.5)
    kc = heads(k, ML_QK_DIM)
    vc = heads(v, ML_V_DIM)
    ic = gates(i_pre)
    lfc = gates(jax.nn.log_sigmoid(f_pre.astype(f32)))
    causal = jnp.tril(jnp.ones((ML_CHUNK, ML_CHUNK), dtype=bool))

    def step(carry, xs):
        C, n, m = carry
        q_, k_, v_, ig, lf = xs
        b = jnp.cumsum(lf, axis=-1)
        g = b[..., -1]
        logd = b[..., :, None] - b[..., None, :] + ig[..., None, :]
        logd = jnp.where(causal, logd, -jnp.inf)
        inter = b + m[..., None]
        m_t = jnp.maximum(inter, jnp.max(logd, axis=-1))
        sc = jnp.einsum('bhtd,bhsd->bhts', q_, k_) * jnp.exp(logd - m_t[..., None])
        w_inter = jnp.exp(inter - m_t)
        num = w_inter[..., None] * jnp.einsum('bhtd,bhde->bhte', q_, C) + jnp.einsum('bhts,bhse->bhte', sc, v_)
        den = w_inter * jnp.einsum('bhtd,bhd->bht', q_, n) + jnp.sum(sc, axis=-1)
        h = num / jnp.maximum(jnp.abs(den), jnp.exp(-m_t))[..., None]
        log_w = g[..., None] - b + ig
        m_new = jnp.maximum(g + m, jnp.max(log_w, axis=-1))
        w_s = jnp.exp(log_w - m_new[..., None])
        decay = jnp.exp(g + m - m_new)
        C_new = decay[..., None, None] * C + jnp.einsum('bhs,bhsd,bhse->bhde', w_s, k_, v_)
        n_new = decay[..., None] * n + jnp.einsum('bhs,bhsd->bhd', w_s, k_)
        return (C_new, n_new, m_new), h

    init = (jnp.zeros((B, ML_HEADS, ML_QK_DIM, ML_V_DIM), f32),
            jnp.zeros((B, ML_HEADS, ML_QK_DIM), f32),
            jnp.zeros((B, ML_HEADS), f32))
    _, hs = lax.scan(step, init, (qc, kc, vc, ic, lfc))
    h = hs.transpose(1, 0, 3, 2, 4).reshape(B, S, ML_HEADS, ML_V_DIM)
    h = head_rms(h).reshape(B, S, ML_V_WIDTH) * norm_g.astype(f32)
    h = jax.nn.sigmoid(o_pre.astype(f32)) * h
    return h.astype(v.dtype)


def setup_inputs(seed: int = 0) -> dict:
    key = jax.random.key(seed)
    ks = jax.random.split(key, 20)
    nrm = jax.random.normal
    L = DEPTH
    f_bias = jnp.linspace(3.0, 6.0, ML_HEADS)[None, :] + 0.01 * nrm(ks[3], (L, ML_HEADS))
    i_bias = 0.1 * nrm(ks[4], (L, ML_HEADS))
    return {
        "x": nrm(ks[0], (BATCH, SEQ, D_MODEL), jnp.float32),
        "norm_mix_g": 1.0 + 0.02 * nrm(ks[1], (L, D_MODEL)),
        "w_in": nrm(ks[2], (L, D_MODEL, D_IN)) * D_MODEL ** -0.5,
        "b_gates": jnp.concatenate([i_bias, f_bias], axis=-1),
        "conv_w": nrm(ks[5], (L, CONV_WIDTH, 2 * ML_QK_WIDTH)) * CONV_WIDTH ** -0.5,
        "conv_b": 0.01 * nrm(ks[6], (L, 2 * ML_QK_WIDTH)),
        "lam": 0.1 * nrm(ks[7], (L, 4, DA_HEAD_DIM)),
        "da_norm_g": 1.0 + 0.02 * nrm(ks[8], (L, DA_V_WIDTH)),
        "ml_norm_g": 1.0 + 0.02 * nrm(ks[9], (L, ML_V_WIDTH)),
        "b_merge": 0.01 * nrm(ks[10], (L, N_BRANCHES * D_MODEL)),
        "w_branch_a": nrm(ks[11], (L, DA_V_WIDTH, D_MODEL)) * DA_V_WIDTH ** -0.5,
        "w_branch_m": nrm(ks[12], (L, ML_V_WIDTH, D_MODEL)) * ML_V_WIDTH ** -0.5,
        "w_out": nrm(ks[13], (L, D_MODEL, D_MODEL)) * D_MODEL ** -0.5,
        "norm_mlp_g": 1.0 + 0.02 * nrm(ks[14], (L, D_MODEL)),
        "w_ff1": nrm(ks[15], (L, D_MODEL, D_FF)) * D_MODEL ** -0.5,
        "w_ff2": nrm(ks[16], (L, D_FF, D_MODEL)) * D_FF ** -0.5,
        "norm_final_g": 1.0 + 0.02 * nrm(ks[17], (D_MODEL,)),
    }


def reference(x, norm_mix_g, w_in, b_gates, conv_w, conv_b, lam, da_norm_g, ml_norm_g,
              b_merge, w_branch_a, w_branch_m, w_out, norm_mlp_g, w_ff1, w_ff2, norm_final_g):
    for l in range(DEPTH):
        lam_init = 0.8 - 0.6 * math.exp(-0.3 * l)
        h = rms_norm(x, norm_mix_g[l])
        proj = h @ w_in[l]
        da_q, da_k, da_v, ml_q, ml_k, ml_v, ml_if, ml_o, mg = jnp.split(proj, SPLIT_POINTS, axis=-1)
        a_out = diff_attention(da_q, da_k, da_v, lam[l], lam_init, da_norm_g[l])
        qk = jax.nn.silu(causal_conv(jnp.concatenate([ml_q, ml_k], axis=-1), conv_w[l], conv_b[l]))
        ml_qc, ml_kc = jnp.split(qk, [ML_QK_WIDTH], axis=-1)
        if_pre = ml_if + b_gates[l]
        m_out = mlstm(ml_qc, ml_kc, ml_v, if_pre[..., :ML_HEADS], if_pre[..., ML_HEADS:], ml_o, ml_norm_g[l])
        gate = jax.nn.sigmoid(mg + b_merge[l])
        g_a, g_m = jnp.split(gate, [D_MODEL], axis=-1)
        merged = g_a * (a_out @ w_branch_a[l]) + g_m * (m_out @ w_branch_m[l])
        x = x + merged @ w_out[l]
        hm = rms_norm(x, norm_mlp_g[l])
        x = x + jnp.square(jax.nn.relu(hm @ w_ff1[l])) @ w_ff2[l]
    return rms_norm(x, norm_final_g)
```

```python
import functools
import math

import jax
import jax.numpy as jnp
from jax import lax
from jax.experimental import pallas as pl
from jax.experimental.pallas import tpu as pltpu

F32 = jnp.float32
BF16 = jnp.bfloat16

D_MODEL = 1024
DA_HEADS = 8
DA_HEAD_DIM = 64
DA_V_DIM = 2 * DA_HEAD_DIM
ML_HEADS = 4
ML_V_DIM = D_MODEL // ML_HEADS
ML_QK_DIM = ML_V_DIM // 2
ML_QK_WIDTH = ML_HEADS * ML_QK_DIM
CONV_WIDTH = 4
D_FF = 4 * D_MODEL
EPS = 1e-6
LAM_INIT = 0.8 - 0.6 * math.exp(-0.3 * 0)
NEG_BIG = -1e30

LANES = 128
SUBLANES = 8
VMEM_LIMIT = 56 * 1024 * 1024

COL_DA_Q, COL_DA_K, COL_ML_QK, COL_ML_V, COL_ML_O, COL_MG_A, COL_MG_M = range(7)
N_PROJ_BLOCKS = 7

NT_DIMS = (((1,), (1,)), ((), ()))
TN_DIMS = (((0,), (0,)), ((), ()))


def _sigmoid(x):
    return 1.0 / (1.0 + jnp.exp(-x))


def _log_sigmoid(x):
    return jnp.minimum(x, 0.0) - jnp.log(1.0 + jnp.exp(-jnp.abs(x)))


def _in_proj_kernel(x_ref, g_ref, w_ref, wvt_ref, wif_ref, proj_ref, vt_ref, if_ref, h_scr):
    j = pl.program_id(1)

    @pl.when(j == 0)
    def _():
        x = x_ref[...]
        h = x * lax.rsqrt(jnp.mean(x * x, axis=-1, keepdims=True) + EPS) * g_ref[...]
        hb = h.astype(BF16)
        h_scr[...] = hb
        if_ref[...] = jnp.dot(hb, wif_ref[...], preferred_element_type=F32)
        vt = lax.dot_general(wvt_ref[...], hb, NT_DIMS, preferred_element_type=F32)
        vt_ref[...] = vt.astype(BF16)

    proj_ref[...] = jnp.dot(h_scr[...], w_ref[...], preferred_element_type=F32).astype(BF16)


def _in_proj(x2, g, w_main, wvt, wif, batch, seq, tm):
    tokens = batch * seq
    nsb = seq // tm
    return pl.pallas_call(
        _in_proj_kernel,
        grid=(tokens // tm, N_PROJ_BLOCKS),
        in_specs=[
            pl.BlockSpec((tm, D_MODEL), lambda i, j: (i, 0)),
            pl.BlockSpec((1, D_MODEL), lambda i, j: (0, 0)),
            pl.BlockSpec((D_MODEL, D_MODEL), lambda i, j: (0, j)),
            pl.BlockSpec((D_MODEL, D_MODEL), lambda i, j: (0, 0)),
            pl.BlockSpec((D_MODEL, LANES), lambda i, j: (0, 0)),
        ],
        out_specs=[
            pl.BlockSpec((tm, D_MODEL), lambda i, j: (i, j)),
            pl.BlockSpec((None, D_MODEL, tm), lambda i, j: (i // nsb, 0, i % nsb)),
            pl.BlockSpec((tm, LANES), lambda i, j: (i, 0)),
        ],
        out_shape=[
            jax.ShapeDtypeStruct((tokens, N_PROJ_BLOCKS * D_MODEL), BF16),
            jax.ShapeDtypeStruct((batch, D_MODEL, seq), BF16),
            jax.ShapeDtypeStruct((tokens, LANES), F32),
        ],
        scratch_shapes=[pltpu.VMEM((tm, D_MODEL), BF16)],
        compiler_params=pltpu.CompilerParams(
            dimension_semantics=("parallel", "arbitrary"), vmem_limit_bytes=VMEM_LIMIT),
    )(x2, g, w_main, wvt, wif)


def _attn_kernel(slopes_ref, lam_ref, q_ref, k_ref, vt_ref, g_ref, o_ref,
                 qq_scr, bias_scr, acc_scr, m_scr, l_scr, *, tile):
    h = pl.program_id(1)
    qi = pl.program_id(2)
    slope = slopes_ref[h]

    qt = q_ref[...].astype(F32).T
    d_idx = lax.broadcasted_iota(jnp.int32, qt.shape, 0)
    q0 = jnp.where(d_idx < DA_HEAD_DIM, qt, 0.0)
    q1 = jnp.where(d_idx >= DA_HEAD_DIM, qt, 0.0)
    qq_scr[...] = jnp.concatenate([q0, q1], axis=1).astype(BF16)

    key_idx = lax.broadcasted_iota(jnp.int32, bias_scr.shape, 0)
    bias_scr[...] = slope * key_idx.astype(F32)

    m_scr[...] = jnp.full(m_scr.shape, NEG_BIG, F32)
    l_scr[...] = jnp.zeros(l_scr.shape, F32)
    acc_scr[...] = jnp.zeros(acc_scr.shape, F32)

    def step(j, diagonal):
        start = pl.multiple_of(j * tile, tile)
        kt = k_ref[pl.ds(start, tile), :]
        vt = vt_ref[:, pl.ds(start, tile)]
        s = jnp.dot(kt, qq_scr[...], preferred_element_type=F32)
        s = s + bias_scr[...]
        if diagonal:
            kk = lax.broadcasted_iota(jnp.int32, s.shape, 0)
            qpos = lax.broadcasted_iota(jnp.int32, s.shape, 1)
            qpos = jnp.where(qpos >= tile, qpos - tile, qpos)
            s = jnp.where(kk <= qpos, s, NEG_BIG)
        c = slope * ((j - qi) * tile).astype(F32)
        m_old = m_scr[...]
        m_new = jnp.maximum(m_old, jnp.max(s, axis=0, keepdims=True) + c)
        p = jnp.exp(s - (m_new - c))
        alpha = jnp.exp(m_old - m_new)
        l_scr[...] = alpha * l_scr[...] + jnp.sum(p, axis=0, keepdims=True)
        acc_scr[...] = alpha * acc_scr[...] + jnp.dot(
            vt, p.astype(BF16), preferred_element_type=F32)
        m_scr[...] = m_new

    def body(j, carry):
        step(j, False)
        return carry

    lax.fori_loop(0, qi, body, 0)
    step(qi, True)

    lam = lam_ref[...]
    lam_full = (jnp.exp(jnp.sum(lam[0:1] * lam[1:2], axis=1, keepdims=True))
                - jnp.exp(jnp.sum(lam[2:3] * lam[3:4], axis=1, keepdims=True)) + LAM_INIT)
    acc = acc_scr[...]
    l = l_scr[...]
    o = acc[:, :tile] / l[:, :tile] - lam_full * (acc[:, tile:] / l[:, tile:])
    o = o * lax.rsqrt(jnp.mean(o * o, axis=0, keepdims=True) + EPS) * (1.0 - LAM_INIT)
    o_ref[...] = (o.T * g_ref[...]).astype(o_ref.dtype)


def _attention(slopes, lam, proj3, vt, g, batch, seq, tile):
    kern = functools.partial(_attn_kernel, tile=tile)
    return pl.pallas_call(
        kern,
        grid=(batch, DA_HEADS, seq // tile),
        in_specs=[
            pl.BlockSpec(memory_space=pltpu.SMEM),
            pl.BlockSpec((4, DA_HEAD_DIM), lambda b, h, i: (0, 0)),
            pl.BlockSpec((None, tile, DA_V_DIM), lambda b, h, i: (b, i, COL_DA_Q * DA_HEADS + h)),
            pl.BlockSpec((None, seq, DA_V_DIM), lambda b, h, i: (b, 0, COL_DA_K * DA_HEADS + h)),
            pl.BlockSpec((None, DA_V_DIM, seq), lambda b, h, i: (b, h, 0)),
            pl.BlockSpec((1, DA_V_DIM), lambda b, h, i: (0, h)),
        ],
        out_specs=pl.BlockSpec((None, tile, DA_V_DIM), lambda b, h, i: (b, i, h)),
        out_shape=jax.ShapeDtypeStruct((batch, seq, D_MODEL), BF16),
        scratch_shapes=[
            pltpu.VMEM((DA_V_DIM, 2 * tile), BF16),
            pltpu.VMEM((tile, 2 * tile), F32),
            pltpu.VMEM((DA_V_DIM, 2 * tile), F32),
            pltpu.VMEM((1, 2 * tile), F32),
            pltpu.VMEM((1, 2 * tile), F32),
        ],
        compiler_params=pltpu.CompilerParams(
            dimension_semantics=("parallel", "parallel", "arbitrary"),
            vmem_limit_bytes=VMEM_LIMIT),
    )(slopes, lam, proj3, proj3, vt, g)


def _mlstm_kernel(qk_ref, v_ref, og_ref, grow_ref, gcol_ref, cw_ref, cb_ref, brow_ref, bcol_ref,
                  ng_ref, out_ref, c_scr, n_scr, m_scr, ext_scr, *, chunk):
    ci = pl.program_id(1)
    pad = SUBLANES

    @pl.when(ci == 0)
    def _():
        c_scr[...] = jnp.zeros(c_scr.shape, F32)
        n_scr[...] = jnp.zeros(n_scr.shape, F32)
        m_scr[...] = jnp.zeros(m_scr.shape, F32)
        ext_scr[0:pad, :] = jnp.zeros((pad, ext_scr.shape[1]), F32)

    ext_scr[pad:pad + chunk, :] = qk_ref[...].astype(F32)
    conv = cb_ref[...] + cw_ref[CONV_WIDTH - 1:CONV_WIDTH, :] * ext_scr[pad:pad + chunk, :]
    for tap in range(1, CONV_WIDTH):
        conv = conv + (cw_ref[CONV_WIDTH - 1 - tap:CONV_WIDTH - tap, :]
                       * ext_scr[pad - tap:pad - tap + chunk, :])
    ext_scr[0:pad, :] = ext_scr[chunk:chunk + pad, :]
    qk = conv * _sigmoid(conv)

    g_rows = grow_ref[...] + brow_ref[...]
    g_cols = gcol_ref[...] + bcol_ref[...]
    r_idx = lax.broadcasted_iota(jnp.int32, (chunk, chunk), 0)
    c_idx = lax.broadcasted_iota(jnp.int32, (chunk, chunk), 1)
    causal = r_idx >= c_idx
    tril = jnp.where(causal, 1.0, 0.0).astype(F32)
    triu = jnp.where(r_idx <= c_idx, 1.0, 0.0).astype(F32)
    b_rows = jnp.dot(_log_sigmoid(g_rows), triu, preferred_element_type=F32,
                     precision=lax.Precision.HIGHEST)
    b_cols = jnp.dot(tril, _log_sigmoid(g_cols), preferred_element_type=F32,
                     precision=lax.Precision.HIGHEST)

    q_scale = ML_QK_DIM ** -0.5
    for hd in range(ML_HEADS):
        qf = qk[:, hd * ML_QK_DIM:(hd + 1) * ML_QK_DIM] * q_scale
        kf = qk[:, ML_QK_WIDTH + hd * ML_QK_DIM:ML_QK_WIDTH + (hd + 1) * ML_QK_DIM]
        qb = qf.astype(BF16)
        vb = v_ref[:, hd * ML_V_DIM:(hd + 1) * ML_V_DIM]
        fcol = ML_HEADS + hd
        bt = b_cols[:, fcol:fcol + 1]
        bs = b_rows[fcol:fcol + 1, :]
        i_row = g_rows[hd:hd + 1, :]
        i_col = g_cols[:, hd:hd + 1]
        m_prev = m_scr[hd]
        c_prev = c_scr[hd]
        n_prev = n_scr[hd]

        logd = jnp.where(causal, bt - bs + i_row, NEG_BIG)
        inter = bt + m_prev
        m_t = jnp.maximum(inter, jnp.max(logd, axis=1, keepdims=True))
        dmat = jnp.exp(logd - m_t)
        sc = lax.dot_general(qb, kf.astype(BF16), NT_DIMS, preferred_element_type=F32) * dmat
        w_inter = jnp.exp(inter - m_t)
        num = (w_inter * jnp.dot(qb, c_prev.astype(BF16), preferred_element_type=F32)
               + jnp.dot(sc.astype(BF16), vb, preferred_element_type=F32))
        den = (w_inter * jnp.sum(qf * n_prev, axis=1, keepdims=True)
               + jnp.sum(sc, axis=1, keepdims=True))
        hh = num / jnp.maximum(jnp.abs(den), jnp.exp(-m_t))

        g_last = bt[chunk - 1:chunk, :]
        log_w = g_last - bt + i_col
        m_new = jnp.maximum(g_last + m_prev, jnp.max(log_w, axis=0, keepdims=True))
        kw = kf * jnp.exp(log_w - m_new)
        decay = jnp.exp(g_last + m_prev - m_new)
        c_scr[hd] = decay * c_prev + lax.dot_general(
            kw.astype(BF16), vb, TN_DIMS, preferred_element_type=F32)
        n_scr[hd] = decay * n_prev + jnp.sum(kw, axis=0, keepdims=True)
        m_scr[hd] = m_new

        sl = slice(hd * ML_V_DIM, (hd + 1) * ML_V_DIM)
        hn = hh * lax.rsqrt(jnp.mean(hh * hh, axis=1, keepdims=True) + EPS) * ng_ref[:, sl]
        out_ref[:, sl] = (_sigmoid(og_ref[:, sl].astype(F32)) * hn).astype(out_ref.dtype)


def _mlstm(proj3, g_rows, g_cols3, conv_w, conv_b, b_row, b_col, norm_g, batch, seq, chunk):
    kern = functools.partial(_mlstm_kernel, chunk=chunk)
    return pl.pallas_call(
        kern,
        grid=(batch, seq // chunk),
        in_specs=[
            pl.BlockSpec((None, chunk, D_MODEL), lambda b, c: (b, c, COL_ML_QK)),
            pl.BlockSpec((None, chunk, D_MODEL), lambda b, c: (b, c, COL_ML_V)),
            pl.BlockSpec((None, chunk, D_MODEL), lambda b, c: (b, c, COL_ML_O)),
            pl.BlockSpec((None, 2 * ML_HEADS, chunk), lambda b, c: (b, 0, c)),
            pl.BlockSpec((None, chunk, LANES), lambda b, c: (b, c, 0)),
            pl.BlockSpec((CONV_WIDTH, D_MODEL), lambda b, c: (0, 0)),
            pl.BlockSpec((1, D_MODEL), lambda b, c: (0, 0)),
            pl.BlockSpec((2 * ML_HEADS, 1), lambda b, c: (0, 0)),
            pl.BlockSpec((1, LANES), lambda b, c: (0, 0)),
            pl.BlockSpec((1, D_MODEL), lambda b, c: (0, 0)),
        ],
        out_specs=pl.BlockSpec((None, chunk, D_MODEL), lambda b, c: (b, c, 0)),
        out_shape=jax.ShapeDtypeStruct((batch, seq, D_MODEL), BF16),
        scratch_shapes=[
            pltpu.VMEM((ML_HEADS, ML_QK_DIM, ML_V_DIM), F32),
            pltpu.VMEM((ML_HEADS, 1, ML_QK_DIM), F32),
            pltpu.VMEM((ML_HEADS, 1, 1), F32),
            pltpu.VMEM((chunk + 2 * SUBLANES, D_MODEL), F32),
        ],
        compiler_params=pltpu.CompilerParams(
            dimension_semantics=("parallel", "arbitrary"), vmem_limit_bytes=VMEM_LIMIT),
    )(proj3, proj3, proj3, g_rows, g_cols3, conv_w, conv_b, b_row, b_col, norm_g)


def _rms(x, g):
    return x * lax.rsqrt(jnp.mean(x * x, axis=-1, keepdims=True) + EPS) * g


def _tail_kernel(x_ref, a_ref, m_ref, ga_ref, gm_ref, bm_ref, wa_ref, wm_ref, wo_ref,
                 gmlp_ref, w1_ref, w2_ref, gfin_ref, o_ref):
    ya = jnp.dot(a_ref[...], wa_ref[...], preferred_element_type=F32)
    ym = jnp.dot(m_ref[...], wm_ref[...], preferred_element_type=F32)
    gate_a = _sigmoid(ga_ref[...].astype(F32) + bm_ref[:, :D_MODEL])
    gate_m = _sigmoid(gm_ref[...].astype(F32) + bm_ref[:, D_MODEL:])
    merged = (gate_a * ya + gate_m * ym).astype(BF16)
    x1 = x_ref[...] + jnp.dot(merged, wo_ref[...], preferred_element_type=F32)
    hm = _rms(x1, gmlp_ref[...]).astype(BF16)
    acc = x1
    for c in range(D_FF // D_MODEL):
        cols = slice(c * D_MODEL, (c + 1) * D_MODEL)
        u = jnp.maximum(jnp.dot(hm, w1_ref[:, cols], preferred_element_type=F32), 0.0)
        acc = acc + jnp.dot((u * u).astype(BF16), w2_ref[cols, :], preferred_element_type=F32)
    o_ref[...] = _rms(acc, gfin_ref[...])


def _tail(x2, a2, m2, proj, b_merge, wa, wm, wo, g_mlp, w1, w2, g_fin, tm):
    tokens = x2.shape[0]
    const = lambda i: (0, 0)

    def resident(shape):
        return pl.BlockSpec(shape, const, pipeline_mode=pl.Buffered(1))

    return pl.pallas_call(
        _tail_kernel,
        grid=(tokens // tm,),
        in_specs=[
            pl.BlockSpec((tm, D_MODEL), lambda i: (i, 0)),
            pl.BlockSpec((tm, D_MODEL), lambda i: (i, 0)),
            pl.BlockSpec((tm, D_MODEL), lambda i: (i, 0)),
            pl.BlockSpec((tm, D_MODEL), lambda i: (i, COL_MG_A)),
            pl.BlockSpec((tm, D_MODEL), lambda i: (i, COL_MG_M)),
            resident((1, 2 * D_MODEL)),
            resident((D_MODEL, D_MODEL)),
            resident((D_MODEL, D_MODEL)),
            resident((D_MODEL, D_MODEL)),
            resident((1, D_MODEL)),
            resident((D_MODEL, D_FF)),
            resident((D_FF, D_MODEL)),
            resident((1, D_MODEL)),
        ],
        out_specs=pl.BlockSpec((tm, D_MODEL), lambda i: (i, 0)),
        out_shape=jax.ShapeDtypeStruct((tokens, D_MODEL), F32),
        compiler_params=pltpu.CompilerParams(
            dimension_semantics=("parallel",), vmem_limit_bytes=VMEM_LIMIT),
    )(x2, a2, m2, proj, proj, b_merge, wa, wm, wo, g_mlp, w1, w2, g_fin)


def kernel(x, norm_mix_g, w_in, b_gates, conv_w, conv_b, lam, da_norm_g, ml_norm_g, b_merge,
           w_branch_a, w_branch_m, w_out, norm_mlp_g, w_ff1, w_ff2, norm_final_g):
    batch, seq, _ = x.shape
    tokens = batch * seq
    x2 = x.reshape(tokens, D_MODEL)

    w = w_in[0]
    o_q, o_k, o_v = 0, D_MODEL, 2 * D_MODEL
    o_mq = 3 * D_MODEL
    o_mv = o_mq + 2 * ML_QK_WIDTH
    o_if = o_mv + D_MODEL
    o_mo = o_if + 2 * ML_HEADS
    o_mg = o_mo + D_MODEL
    q_scale = DA_HEAD_DIM ** -0.5
    w_main = jnp.concatenate(
        [w[:, o_q:o_k] * q_scale, w[:, o_k:o_v], w[:, o_mq:o_if], w[:, o_mo:]], axis=1).astype(BF16)
    wvt = w[:, o_v:o_mq].T.astype(BF16)
    wif = jnp.pad(w[:, o_if:o_mo], ((0, 0), (0, LANES - 2 * ML_HEADS))).astype(BF16)

    tm_proj = min(1024, seq)
    proj, vt, ifg = _in_proj(x2, norm_mix_g, w_main, wvt, wif, batch, seq, tm_proj)
    proj3 = proj.reshape(batch, seq, N_PROJ_BLOCKS * D_MODEL)

    slopes = 2.0 ** (-8.0 * jnp.arange(1, DA_HEADS + 1, dtype=F32) / DA_HEADS)
    attn_tile = min(256, seq)
    a_out = _attention(slopes, lam[0], proj3, vt, da_norm_g, batch, seq, attn_tile)

    ifg3 = ifg.reshape(batch, seq, LANES)
    g_rows = jnp.swapaxes(ifg3[:, :, :2 * ML_HEADS], 1, 2)
    b_row = b_gates[0].reshape(2 * ML_HEADS, 1)
    b_col = jnp.pad(b_gates, ((0, 0), (0, LANES - 2 * ML_HEADS)))
    chunk = min(256, seq)
    m_out = _mlstm(proj3, g_rows, ifg3, conv_w[0], conv_b, b_row, b_col, ml_norm_g,
                   batch, seq, chunk)

    out = _tail(x2, a_out.reshape(tokens, D_MODEL), m_out.reshape(tokens, D_MODEL), proj,
                b_merge, w_branch_a[0].astype(BF16), w_branch_m[0].astype(BF16),
                w_out[0].astype(BF16), norm_mlp_g, w_ff1[0].astype(BF16), w_ff2[0].astype(BF16),
                norm_final_g.reshape(1, D_MODEL), min(256, tokens))
    return out.reshape(batch, seq, D_MODEL)
```

```python
import functools
import math

import jax
import jax.numpy as jnp
from jax import lax
from jax.experimental import pallas as pl
from jax.experimental.pallas import tpu as pltpu

F32 = jnp.float32
BF16 = jnp.bfloat16

D_MODEL = 1024
DA_HEADS = 8
DA_HEAD_DIM = 64
DA_V_DIM = 2 * DA_HEAD_DIM
ML_HEADS = 4
ML_V_DIM = D_MODEL // ML_HEADS
ML_QK_DIM = ML_V_DIM // 2
ML_QK_WIDTH = ML_HEADS * ML_QK_DIM
CONV_WIDTH = 4
D_FF = 4 * D_MODEL
EPS = 1e-6
LAM_INIT = 0.8 - 0.6 * math.exp(-0.3 * 0)
NEG_BIG = -1e30
LOG2E = math.log2(math.e)

LANES = 128
SUBLANES = 8
VMEM_LIMIT = 56 * 1024 * 1024

COL_DA_Q, COL_DA_K, COL_ML_QK, COL_ML_V, COL_ML_O, COL_MG_A, COL_MG_M = range(7)
N_PROJ_BLOCKS = 7

NT_DIMS = (((1,), (1,)), ((), ()))
TN_DIMS = (((0,), (0,)), ((), ()))


def _sigmoid(x):
    return 1.0 / (1.0 + jnp.exp(-x))


def _log_sigmoid(x):
    return jnp.minimum(x, 0.0) - jnp.log(1.0 + jnp.exp(-jnp.abs(x)))


def _in_proj_kernel(x_ref, g_ref, w_ref, wvt_ref, wif_ref, proj_ref, vt_ref, if_ref, h_scr):
    j = pl.program_id(1)

    @pl.when(j == 0)
    def _():
        x = x_ref[...]
        h = x * lax.rsqrt(jnp.mean(x * x, axis=-1, keepdims=True) + EPS) * g_ref[...]
        hb = h.astype(BF16)
        h_scr[...] = hb
        if_ref[...] = jnp.dot(hb, wif_ref[...], preferred_element_type=F32)
        vt = lax.dot_general(wvt_ref[...], hb, NT_DIMS, preferred_element_type=F32)
        vt_ref[...] = vt.astype(BF16)

    proj_ref[...] = jnp.dot(h_scr[...], w_ref[...], preferred_element_type=F32).astype(BF16)


def _in_proj(x2, g, w_main, wvt, wif, batch, seq, tm):
    tokens = batch * seq
    nsb = seq // tm
    return pl.pallas_call(
        _in_proj_kernel,
        grid=(tokens // tm, N_PROJ_BLOCKS),
        in_specs=[
            pl.BlockSpec((tm, D_MODEL), lambda i, j: (i, 0)),
            pl.BlockSpec((1, D_MODEL), lambda i, j: (0, 0)),
            pl.BlockSpec((D_MODEL, D_MODEL), lambda i, j: (0, j)),
            pl.BlockSpec((D_MODEL, D_MODEL), lambda i, j: (0, 0)),
            pl.BlockSpec((D_MODEL, LANES), lambda i, j: (0, 0)),
        ],
        out_specs=[
            pl.BlockSpec((tm, D_MODEL), lambda i, j: (i, j)),
            pl.BlockSpec((None, D_MODEL, tm), lambda i, j: (i // nsb, 0, i % nsb)),
            pl.BlockSpec((tm, LANES), lambda i, j: (i, 0)),
        ],
        out_shape=[
            jax.ShapeDtypeStruct((tokens, N_PROJ_BLOCKS * D_MODEL), BF16),
            jax.ShapeDtypeStruct((batch, D_MODEL, seq), BF16),
            jax.ShapeDtypeStruct((tokens, LANES), F32),
        ],
        scratch_shapes=[pltpu.VMEM((tm, D_MODEL), BF16)],
        compiler_params=pltpu.CompilerParams(
            dimension_semantics=("parallel", "arbitrary"), vmem_limit_bytes=VMEM_LIMIT),
    )(x2, g, w_main, wvt, wif)


def _attn_kernel(slopes_ref, lam_ref, q_ref, k_ref, vt_ref, g_ref, o_ref,
                 qq_scr, bias_scr, s_scr, p_scr, acc_scr, m_scr, l_scr, *, tile):
    h = pl.program_id(1)
    qi = pl.program_id(2)
    kt_size = tile // 2
    slope2 = slopes_ref[h] * LOG2E

    qt = q_ref[...].astype(F32).T * LOG2E
    d_idx = lax.broadcasted_iota(jnp.int32, qt.shape, 0)
    q0 = jnp.where(d_idx < DA_HEAD_DIM, qt, 0.0)
    q1 = jnp.where(d_idx >= DA_HEAD_DIM, qt, 0.0)
    qq_scr[...] = jnp.concatenate([q0, q1], axis=1).astype(BF16)

    key_idx = lax.broadcasted_iota(jnp.int32, bias_scr.shape, 0)
    bias_scr[...] = slope2 * key_idx.astype(F32)

    m_scr[...] = jnp.full(m_scr.shape, NEG_BIG, F32)
    l_scr[...] = jnp.zeros(l_scr.shape, F32)
    acc_scr[...] = jnp.zeros(acc_scr.shape, F32)
    p_scr[1] = jnp.zeros(p_scr.shape[1:], BF16)

    def scores(t, slot):
        start = pl.multiple_of(t * kt_size, kt_size)
        kt = k_ref[pl.ds(start, kt_size), :]
        s_scr[slot] = jnp.dot(kt, qq_scr[...], preferred_element_type=F32) + bias_scr[...]

    def weighted_values(t, slot):
        start = pl.multiple_of(jnp.maximum(t, 0) * kt_size, kt_size)
        return jnp.dot(vt_ref[:, pl.ds(start, kt_size)], p_scr[slot],
                       preferred_element_type=F32)

    def half_step(t, slot, mask=None, prefetch=True):
        if prefetch:
            scores(t + 1, 1 - slot)
        s = s_scr[slot]
        if mask is not None:
            s = jnp.where(mask, s, NEG_BIG)
        c = slope2 * (t * kt_size - qi * tile).astype(F32)
        m_old = m_scr[...]
        m_new = jnp.maximum(m_old, jnp.max(s, axis=0, keepdims=True) + c)
        p = jnp.exp2(s - (m_new - c))
        alpha = jnp.exp2(m_old - m_new)
        p_scr[slot] = p.astype(BF16)
        m_scr[...] = m_new
        l_scr[...] = alpha * l_scr[...] + jnp.sum(p, axis=0, keepdims=True)
        acc_scr[...] = alpha * (acc_scr[...] + weighted_values(t - 1, 1 - slot))

    scores(0, 0)

    def body(i, carry):
        half_step(2 * i, 0)
        half_step(2 * i + 1, 1)
        return carry

    lax.fori_loop(0, qi, body, 0)

    kk = lax.broadcasted_iota(jnp.int32, (kt_size, 2 * tile), 0)
    qpos = lax.broadcasted_iota(jnp.int32, (kt_size, 2 * tile), 1)
    qpos = jnp.where(qpos >= tile, qpos - tile, qpos)
    half_step(2 * qi, 0, mask=kk <= qpos)
    half_step(2 * qi + 1, 1, mask=kk + kt_size <= qpos, prefetch=False)
    acc = acc_scr[...] + weighted_values(2 * qi + 1, 1)
    l = l_scr[...]

    lam = lam_ref[...]
    lam_full = (jnp.exp(jnp.sum(lam[0:1] * lam[1:2], axis=1, keepdims=True))
                - jnp.exp(jnp.sum(lam[2:3] * lam[3:4], axis=1, keepdims=True)) + LAM_INIT)
    o = acc[:, :tile] / l[:, :tile] - lam_full * (acc[:, tile:] / l[:, tile:])
    o = o * lax.rsqrt(jnp.mean(o * o, axis=0, keepdims=True) + EPS) * (1.0 - LAM_INIT)
    o_ref[...] = (o.T * g_ref[...]).astype(o_ref.dtype)


def _attention(slopes, lam, proj3, vt, g, batch, seq, tile):
    kern = functools.partial(_attn_kernel, tile=tile)
    return pl.pallas_call(
        kern,
        grid=(batch, DA_HEADS, seq // tile),
        in_specs=[
            pl.BlockSpec(memory_space=pltpu.SMEM),
            pl.BlockSpec((4, DA_HEAD_DIM), lambda b, h, i: (0, 0)),
            pl.BlockSpec((None, tile, DA_V_DIM), lambda b, h, i: (b, i, COL_DA_Q * DA_HEADS + h)),
            pl.BlockSpec((None, seq, DA_V_DIM), lambda b, h, i: (b, 0, COL_DA_K * DA_HEADS + h)),
            pl.BlockSpec((None, DA_V_DIM, seq), lambda b, h, i: (b, h, 0)),
            pl.BlockSpec((1, DA_V_DIM), lambda b, h, i: (0, h)),
        ],
        out_specs=pl.BlockSpec((None, tile, DA_V_DIM), lambda b, h, i: (b, i, h)),
        out_shape=jax.ShapeDtypeStruct((batch, seq, D_MODEL), BF16),
        scratch_shapes=[
            pltpu.VMEM((DA_V_DIM, 2 * tile), BF16),
            pltpu.VMEM((tile // 2, 2 * tile), F32),
            pltpu.VMEM((2, tile // 2, 2 * tile), F32),
            pltpu.VMEM((2, tile // 2, 2 * tile), BF16),
            pltpu.VMEM((DA_V_DIM, 2 * tile), F32),
            pltpu.VMEM((1, 2 * tile), F32),
            pltpu.VMEM((1, 2 * tile), F32),
        ],
        compiler_params=pltpu.CompilerParams(
            dimension_semantics=("parallel", "parallel", "arbitrary"),
            vmem_limit_bytes=VMEM_LIMIT),
    )(slopes, lam, proj3, proj3, vt, g)


def _mlstm_kernel(qk_ref, v_ref, og_ref, grow_ref, gcol_ref, cw_ref, cb_ref, brow_ref, bcol_ref,
                  ng_ref, out_ref, c_scr, n_scr, m_scr, ext_scr, *, chunk):
    ci = pl.program_id(1)
    pad = SUBLANES

    @pl.when(ci == 0)
    def _():
        c_scr[...] = jnp.zeros(c_scr.shape, F32)
        n_scr[...] = jnp.zeros(n_scr.shape, F32)
        m_scr[...] = jnp.zeros(m_scr.shape, F32)
        ext_scr[0:pad, :] = jnp.zeros((pad, ext_scr.shape[1]), F32)

    ext_scr[pad:pad + chunk, :] = qk_ref[...].astype(F32)
    conv = cb_ref[...] + cw_ref[CONV_WIDTH - 1:CONV_WIDTH, :] * ext_scr[pad:pad + chunk, :]
    for tap in range(1, CONV_WIDTH):
        conv = conv + (cw_ref[CONV_WIDTH - 1 - tap:CONV_WIDTH - tap, :]
                       * ext_scr[pad - tap:pad - tap + chunk, :])
    ext_scr[0:pad, :] = ext_scr[chunk:chunk + pad, :]
    qk = conv * _sigmoid(conv)

    g_rows = grow_ref[...] + brow_ref[...]
    g_cols = gcol_ref[...] + bcol_ref[...]
    r_idx = lax.broadcasted_iota(jnp.int32, (chunk, chunk), 0)
    c_idx = lax.broadcasted_iota(jnp.int32, (chunk, chunk), 1)
    causal = r_idx >= c_idx
    tril = jnp.where(causal, 1.0, 0.0).astype(F32)
    triu = jnp.where(r_idx <= c_idx, 1.0, 0.0).astype(F32)
    b_rows = jnp.dot(_log_sigmoid(g_rows), triu, preferred_element_type=F32,
                     precision=lax.Precision.HIGHEST)
    b_cols = jnp.dot(tril, _log_sigmoid(g_cols), preferred_element_type=F32,
                     precision=lax.Precision.HIGHEST)

    q_scale = ML_QK_DIM ** -0.5
    for hd in range(ML_HEADS):
        qf = qk[:, hd * ML_QK_DIM:(hd + 1) * ML_QK_DIM] * q_scale
        kf = qk[:, ML_QK_WIDTH + hd * ML_QK_DIM:ML_QK_WIDTH + (hd + 1) * ML_QK_DIM]
        qb = qf.astype(BF16)
        vb = v_ref[:, hd * ML_V_DIM:(hd + 1) * ML_V_DIM]
        fcol = ML_HEADS + hd
        bt = b_cols[:, fcol:fcol + 1]
        bs = b_rows[fcol:fcol + 1, :]
        i_row = g_rows[hd:hd + 1, :]
        i_col = g_cols[:, hd:hd + 1]
        m_prev = m_scr[hd]
        c_prev = c_scr[hd]
        n_prev = n_scr[hd]

        logd = jnp.where(causal, bt - bs + i_row, NEG_BIG)
        inter = bt + m_prev
        m_t = jnp.maximum(inter, jnp.max(logd, axis=1, keepdims=True))
        dmat = jnp.exp(logd - m_t)
        sc = lax.dot_general(qb, kf.astype(BF16), NT_DIMS, preferred_element_type=F32) * dmat
        w_inter = jnp.exp(inter - m_t)
        num = (w_inter * jnp.dot(qb, c_prev.astype(BF16), preferred_element_type=F32)
               + jnp.dot(sc.astype(BF16), vb, preferred_element_type=F32))
        den = (w_inter * jnp.sum(qf * n_prev, axis=1, keepdims=True)
               + jnp.sum(sc, axis=1, keepdims=True))
        hh = num / jnp.maximum(jnp.abs(den), jnp.exp(-m_t))

        g_last = bt[chunk - 1:chunk, :]
        log_w = g_last - bt + i_col
        m_new = jnp.maximum(g_last + m_prev, jnp.max(log_w, axis=0, keepdims=True))
        kw = kf * jnp.exp(log_w - m_new)
        decay = jnp.exp(g_last + m_prev - m_new)
        c_scr[hd] = decay * c_prev + lax.dot_general(
            kw.astype(BF16), vb, TN_DIMS, preferred_element_type=F32)
        n_scr[hd] = decay * n_prev + jnp.sum(kw, axis=0, keepdims=True)
        m_scr[hd] = m_new

        sl = slice(hd * ML_V_DIM, (hd + 1) * ML_V_DIM)
        hn = hh * lax.rsqrt(jnp.mean(hh * hh, axis=1, keepdims=True) + EPS) * ng_ref[:, sl]
        out_ref[:, sl] = (_sigmoid(og_ref[:, sl].astype(F32)) * hn).astype(out_ref.dtype)


def _mlstm(proj3, g_rows, g_cols3, conv_w, conv_b, b_row, b_col, norm_g, batch, seq, chunk):
    kern = functools.partial(_mlstm_kernel, chunk=chunk)
    return pl.pallas_call(
        kern,
        grid=(batch, seq // chunk),
        in_specs=[
            pl.BlockSpec((None, chunk, D_MODEL), lambda b, c: (b, c, COL_ML_QK)),
            pl.BlockSpec((None, chunk, D_MODEL), lambda b, c: (b, c, COL_ML_V)),
            pl.BlockSpec((None, chunk, D_MODEL), lambda b, c: (b, c, COL_ML_O)),
            pl.BlockSpec((None, 2 * ML_HEADS, chunk), lambda b, c: (b, 0, c)),
            pl.BlockSpec((None, chunk, LANES), lambda b, c: (b, c, 0)),
            pl.BlockSpec((CONV_WIDTH, D_MODEL), lambda b, c: (0, 0)),
            pl.BlockSpec((1, D_MODEL), lambda b, c: (0, 0)),
            pl.BlockSpec((2 * ML_HEADS, 1), lambda b, c: (0, 0)),
            pl.BlockSpec((1, LANES), lambda b, c: (0, 0)),
            pl.BlockSpec((1, D_MODEL), lambda b, c: (0, 0)),
        ],
        out_specs=pl.BlockSpec((None, chunk, D_MODEL), lambda b, c: (b, c, 0)),
        out_shape=jax.ShapeDtypeStruct((batch, seq, D_MODEL), BF16),
        scratch_shapes=[
            pltpu.VMEM((ML_HEADS, ML_QK_DIM, ML_V_DIM), F32),
            pltpu.VMEM((ML_HEADS, 1, ML_QK_DIM), F32),
            pltpu.VMEM((ML_HEADS, 1, 1), F32),
            pltpu.VMEM((chunk + 2 * SUBLANES, D_MODEL), F32),
        ],
        compiler_params=pltpu.CompilerParams(
            dimension_semantics=("parallel", "arbitrary"), vmem_limit_bytes=VMEM_LIMIT),
    )(proj3, proj3, proj3, g_rows, g_cols3, conv_w, conv_b, b_row, b_col, norm_g)


def _rms(x, g):
    return x * lax.rsqrt(jnp.mean(x * x, axis=-1, keepdims=True) + EPS) * g


def _tail_kernel(x_ref, a_ref, m_ref, ga_ref, gm_ref, bm_ref, wa_ref, wm_ref, wo_ref,
                 gmlp_ref, w1_ref, w2_ref, gfin_ref, o_ref):
    ya = jnp.dot(a_ref[...], wa_ref[...], preferred_element_type=F32)
    ym = jnp.dot(m_ref[...], wm_ref[...], preferred_element_type=F32)
    gate_a = _sigmoid(ga_ref[...].astype(F32) + bm_ref[:, :D_MODEL])
    gate_m = _sigmoid(gm_ref[...].astype(F32) + bm_ref[:, D_MODEL:])
    merged = (gate_a * ya + gate_m * ym).astype(BF16)
    x1 = x_ref[...] + jnp.dot(merged, wo_ref[...], preferred_element_type=F32)
    hm = _rms(x1, gmlp_ref[...]).astype(BF16)
    acc = x1
    for c in range(D_FF // D_MODEL):
        cols = slice(c * D_MODEL, (c + 1) * D_MODEL)
        u = jnp.maximum(jnp.dot(hm, w1_ref[:, cols], preferred_element_type=F32), 0.0)
        acc = acc + jnp.dot((u * u).astype(BF16), w2_ref[cols, :], preferred_element_type=F32)
    o_ref[...] = _rms(acc, gfin_ref[...])


def _tail(x2, a2, m2, proj, b_merge, wa, wm, wo, g_mlp, w1, w2, g_fin, tm):
    tokens = x2.shape[0]
    const = lambda i: (0, 0)

    def resident(shape):
        return pl.BlockSpec(shape, const, pipeline_mode=pl.Buffered(1))

    return pl.pallas_call(
        _tail_kernel,
        grid=(tokens // tm,),
        in_specs=[
            pl.BlockSpec((tm, D_MODEL), lambda i: (i, 0)),
            pl.BlockSpec((tm, D_MODEL), lambda i: (i, 0)),
            pl.BlockSpec((tm, D_MODEL), lambda i: (i, 0)),
            pl.BlockSpec((tm, D_MODEL), lambda i: (i, COL_MG_A)),
            pl.BlockSpec((tm, D_MODEL), lambda i: (i, COL_MG_M)),
            resident((1, 2 * D_MODEL)),
            resident((D_MODEL, D_MODEL)),
            resident((D_MODEL, D_MODEL)),
            resident((D_MODEL, D_MODEL)),
            resident((1, D_MODEL)),
            resident((D_MODEL, D_FF)),
            resident((D_FF, D_MODEL)),
            resident((1, D_MODEL)),
        ],
        out_specs=pl.BlockSpec((tm, D_MODEL), lambda i: (i, 0)),
        out_shape=jax.ShapeDtypeStruct((tokens, D_MODEL), F32),
        compiler_params=pltpu.CompilerParams(
            dimension_semantics=("parallel",), vmem_limit_bytes=VMEM_LIMIT),
    )(x2, a2, m2, proj, proj, b_merge, wa, wm, wo, g_mlp, w1, w2, g_fin)


def kernel(x, norm_mix_g, w_in, b_gates, conv_w, conv_b, lam, da_norm_g, ml_norm_g, b_merge,
           w_branch_a, w_branch_m, w_out, norm_mlp_g, w_ff1, w_ff2, norm_final_g):
    batch, seq, _ = x.shape
    tokens = batch * seq
    x2 = x.reshape(tokens, D_MODEL)

    w = w_in[0]
    o_q, o_k, o_v = 0, D_MODEL, 2 * D_MODEL
    o_mq = 3 * D_MODEL
    o_mv = o_mq + 2 * ML_QK_WIDTH
    o_if = o_mv + D_MODEL
    o_mo = o_if + 2 * ML_HEADS
    o_mg = o_mo + D_MODEL
    q_scale = DA_HEAD_DIM ** -0.5
    w_main = jnp.concatenate(
        [w[:, o_q:o_k] * q_scale, w[:, o_k:o_v], w[:, o_mq:o_if], w[:, o_mo:]], axis=1).astype(BF16)
    wvt = w[:, o_v:o_mq].T.astype(BF16)
    wif = jnp.pad(w[:, o_if:o_mo], ((0, 0), (0, LANES - 2 * ML_HEADS))).astype(BF16)

    tm_proj = min(1024, seq)
    proj, vt, ifg = _in_proj(x2, norm_mix_g, w_main, wvt, wif, batch, seq, tm_proj)
    proj3 = proj.reshape(batch, seq, N_PROJ_BLOCKS * D_MODEL)

    slopes = 2.0 ** (-8.0 * jnp.arange(1, DA_HEADS + 1, dtype=F32) / DA_HEADS)
    attn_tile = min(256, seq)
    a_out = _attention(slopes, lam[0], proj3, vt, da_norm_g, batch, seq, attn_tile)

    ifg3 = ifg.reshape(batch, seq, LANES)
    g_rows = jnp.swapaxes(ifg3[:, :, :2 * ML_HEADS], 1, 2)
    b_row = b_gates[0].reshape(2 * ML_HEADS, 1)
    b_col = jnp.pad(b_gates, ((0, 0), (0, LANES - 2 * ML_HEADS)))
    chunk = min(256, seq)
    m_out = _mlstm(proj3, g_rows, ifg3, conv_w[0], conv_b, b_row, b_col, ml_norm_g,
                   batch, seq, chunk)

    out = _tail(x2, a_out.reshape(tokens, D_MODEL), m_out.reshape(tokens, D_MODEL), proj,
                b_merge, w_branch_a[0].astype(BF16), w_branch_m[0].astype(BF16),
                w_out[0].astype(BF16), norm_mlp_g, w_ff1[0].astype(BF16), w_ff2[0].astype(BF16),
                norm_final_g.reshape(1, D_MODEL), min(256, tokens))
    return out.reshape(batch, seq, D_MODEL)
```

```python
import functools
import math

import jax
import jax.numpy as jnp
from jax import lax
from jax.experimental import pallas as pl
from jax.experimental.pallas import tpu as pltpu

F32 = jnp.float32
BF16 = jnp.bfloat16

D_MODEL = 1024
DA_HEADS = 8
DA_HEAD_DIM = 64
DA_V_DIM = 2 * DA_HEAD_DIM
ML_HEADS = 4
ML_V_DIM = D_MODEL // ML_HEADS
ML_QK_DIM = ML_V_DIM // 2
ML_QK_WIDTH = ML_HEADS * ML_QK_DIM
CONV_WIDTH = 4
D_FF = 4 * D_MODEL
EPS = 1e-6
LAM_INIT = 0.8 - 0.6 * math.exp(-0.3 * 0)
NEG_BIG = -1e30
LOG2E = math.log2(math.e)
ONES_ROWS = 16

LANES = 128
SUBLANES = 8
VMEM_LIMIT = 56 * 1024 * 1024

COL_DA_Q, COL_DA_K, COL_ML_QK, COL_ML_V, COL_ML_O, COL_MG_A, COL_MG_M = range(7)
N_PROJ_BLOCKS = 7

NT_DIMS = (((1,), (1,)), ((), ()))
TN_DIMS = (((0,), (0,)), ((), ()))


def _sigmoid(x):
    return 1.0 / (1.0 + jnp.exp(-x))


def _log_sigmoid(x):
    return jnp.minimum(x, 0.0) - jnp.log(1.0 + jnp.exp(-jnp.abs(x)))


def _in_proj_kernel(x_ref, g_ref, w_ref, wvt_ref, wif_ref, proj_ref, vt_ref, if_ref, h_scr):
    j = pl.program_id(1)

    @pl.when(j == 0)
    def _():
        x = x_ref[...]
        h = x * lax.rsqrt(jnp.mean(x * x, axis=-1, keepdims=True) + EPS) * g_ref[...]
        hb = h.astype(BF16)
        h_scr[...] = hb
        if_ref[...] = jnp.dot(hb, wif_ref[...], preferred_element_type=F32)
        vt = lax.dot_general(wvt_ref[...], hb, NT_DIMS, preferred_element_type=F32)
        vt_ref[...] = vt.astype(BF16)

    proj_ref[...] = jnp.dot(h_scr[...], w_ref[...], preferred_element_type=F32).astype(BF16)


def _in_proj(x2, g, w_main, wvt, wif, batch, seq, tm):
    tokens = batch * seq
    nsb = seq // tm
    return pl.pallas_call(
        _in_proj_kernel,
        grid=(tokens // tm, N_PROJ_BLOCKS),
        in_specs=[
            pl.BlockSpec((tm, D_MODEL), lambda i, j: (i, 0)),
            pl.BlockSpec((1, D_MODEL), lambda i, j: (0, 0)),
            pl.BlockSpec((D_MODEL, D_MODEL), lambda i, j: (0, j)),
            pl.BlockSpec((D_MODEL, D_MODEL), lambda i, j: (0, 0)),
            pl.BlockSpec((D_MODEL, LANES), lambda i, j: (0, 0)),
        ],
        out_specs=[
            pl.BlockSpec((tm, D_MODEL), lambda i, j: (i, j)),
            pl.BlockSpec((None, D_MODEL, tm), lambda i, j: (i // nsb, 0, i % nsb)),
            pl.BlockSpec((tm, LANES), lambda i, j: (i, 0)),
        ],
        out_shape=[
            jax.ShapeDtypeStruct((tokens, N_PROJ_BLOCKS * D_MODEL), BF16),
            jax.ShapeDtypeStruct((batch, D_MODEL, seq), BF16),
            jax.ShapeDtypeStruct((tokens, LANES), F32),
        ],
        scratch_shapes=[pltpu.VMEM((tm, D_MODEL), BF16)],
        compiler_params=pltpu.CompilerParams(
            dimension_semantics=("parallel", "arbitrary"), vmem_limit_bytes=VMEM_LIMIT),
    )(x2, g, w_main, wvt, wif)


def _attn_kernel(slopes_ref, lam_ref, q_ref, k_ref, vt_ref, g_ref, o_ref,
                 qq_scr, pos_scr, s_scr, p_scr, acc_scr, m_scr, *, tile):
    h = pl.program_id(1)
    qi = pl.program_id(2)
    kt_size = tile // 2
    slope2 = slopes_ref[h] * LOG2E

    qt = q_ref[...].astype(F32).T * LOG2E
    d_idx = lax.broadcasted_iota(jnp.int32, qt.shape, 0)
    q0 = jnp.where(d_idx < DA_HEAD_DIM, qt, 0.0)
    q1 = jnp.where(d_idx >= DA_HEAD_DIM, qt, 0.0)
    qq_scr[0:DA_V_DIM, :] = jnp.concatenate([q0, q1], axis=1).astype(BF16)
    slope_vec = jnp.full((DA_V_DIM, 2 * tile), slope2, F32)
    slope_hi = slope_vec.astype(BF16).astype(F32)
    feat = lax.broadcasted_iota(jnp.int32, slope_vec.shape, 0)
    qq_scr[DA_V_DIM:, :] = jnp.where(
        feat == 0, slope_hi, jnp.where(feat == 1, slope_vec - slope_hi, 0.0)).astype(BF16)
    key_off = lax.broadcasted_iota(jnp.int32, pos_scr.shape, 0).astype(F32)
    lane = lax.broadcasted_iota(jnp.int32, pos_scr.shape, 1)
    pos_scr[...] = jnp.where(lane < 2, key_off, 0.0).astype(BF16)

    m_scr[...] = jnp.full(m_scr.shape, NEG_BIG, F32)
    acc_scr[...] = jnp.zeros(acc_scr.shape, F32)
    p_scr[1] = jnp.zeros(p_scr.shape[1:], BF16)

    def scores(t, slot):
        start = pl.multiple_of(t * kt_size, kt_size)
        kt = jnp.concatenate([k_ref[pl.ds(start, kt_size), :], pos_scr[...]], axis=1)
        s_scr[slot] = jnp.dot(kt, qq_scr[...], preferred_element_type=F32)

    ones_rows = jnp.ones((ONES_ROWS, kt_size), BF16)

    def weighted_values(t, slot):
        start = pl.multiple_of(jnp.maximum(t, 0) * kt_size, kt_size)
        vt = jnp.concatenate([vt_ref[:, pl.ds(start, kt_size)], ones_rows], axis=0)
        return jnp.dot(vt, p_scr[slot], preferred_element_type=F32)

    def half_step(t, slot, mask=None, prefetch=True):
        if prefetch:
            scores(t + 1, 1 - slot)
        s = s_scr[slot]
        if mask is not None:
            s = jnp.where(mask, s, NEG_BIG)
        c = slope2 * (t * kt_size - qi * tile).astype(F32)
        m_old = m_scr[...]
        m_new = jnp.maximum(m_old, jnp.max(s, axis=0, keepdims=True) + c)
        p = jnp.exp2(s - (m_new - c))
        alpha = jnp.exp2(m_old - m_new)
        p_scr[slot] = p.astype(BF16)
        m_scr[...] = m_new
        acc_scr[...] = alpha * (acc_scr[...] + weighted_values(t - 1, 1 - slot))

    scores(0, 0)

    def body(i, carry):
        half_step(2 * i, 0)
        half_step(2 * i + 1, 1)
        return carry

    lax.fori_loop(0, qi, body, 0)

    kk = lax.broadcasted_iota(jnp.int32, (kt_size, 2 * tile), 0)
    qpos = lax.broadcasted_iota(jnp.int32, (kt_size, 2 * tile), 1)
    qpos = jnp.where(qpos >= tile, qpos - tile, qpos)
    half_step(2 * qi, 0, mask=kk <= qpos)
    half_step(2 * qi + 1, 1, mask=kk + kt_size <= qpos, prefetch=False)
    acc = acc_scr[...] + weighted_values(2 * qi + 1, 1)
    l = acc[DA_V_DIM:DA_V_DIM + 1, :]
    acc = acc[:DA_V_DIM, :]

    lam = lam_ref[...]
    lam_full = (jnp.exp(jnp.sum(lam[0:1] * lam[1:2], axis=1, keepdims=True))
                - jnp.exp(jnp.sum(lam[2:3] * lam[3:4], axis=1, keepdims=True)) + LAM_INIT)
    o = acc[:, :tile] / l[:, :tile] - lam_full * (acc[:, tile:] / l[:, tile:])
    o = o * lax.rsqrt(jnp.mean(o * o, axis=0, keepdims=True) + EPS) * (1.0 - LAM_INIT)
    o_ref[...] = (o.T * g_ref[...]).astype(o_ref.dtype)


def _attention(slopes, lam, proj3, vt, g, batch, seq, tile):
    kern = functools.partial(_attn_kernel, tile=tile)
    return pl.pallas_call(
        kern,
        grid=(batch, DA_HEADS, seq // tile),
        in_specs=[
            pl.BlockSpec(memory_space=pltpu.SMEM),
            pl.BlockSpec((4, DA_HEAD_DIM), lambda b, h, i: (0, 0)),
            pl.BlockSpec((None, tile, DA_V_DIM), lambda b, h, i: (b, i, COL_DA_Q * DA_HEADS + h)),
            pl.BlockSpec((None, seq, DA_V_DIM), lambda b, h, i: (b, 0, COL_DA_K * DA_HEADS + h)),
            pl.BlockSpec((None, DA_V_DIM, seq), lambda b, h, i: (b, h, 0)),
            pl.BlockSpec((1, DA_V_DIM), lambda b, h, i: (0, h)),
        ],
        out_specs=pl.BlockSpec((None, tile, DA_V_DIM), lambda b, h, i: (b, i, h)),
        out_shape=jax.ShapeDtypeStruct((batch, seq, D_MODEL), BF16),
        scratch_shapes=[
            pltpu.VMEM((2 * DA_V_DIM, 2 * tile), BF16),
            pltpu.VMEM((tile // 2, DA_V_DIM), BF16),
            pltpu.VMEM((2, tile // 2, 2 * tile), F32),
            pltpu.VMEM((2, tile // 2, 2 * tile), BF16),
            pltpu.VMEM((DA_V_DIM + ONES_ROWS, 2 * tile), F32),
            pltpu.VMEM((1, 2 * tile), F32),
        ],
        compiler_params=pltpu.CompilerParams(
            dimension_semantics=("parallel", "parallel", "arbitrary"),
            vmem_limit_bytes=VMEM_LIMIT),
    )(slopes, lam, proj3, proj3, vt, g)


def _mlstm_kernel(qk_ref, v_ref, og_ref, grow_ref, gcol_ref, cw_ref, cb_ref, brow_ref, bcol_ref,
                  ng_ref, out_ref, c_scr, n_scr, m_scr, ext_scr, *, chunk):
    ci = pl.program_id(1)
    pad = SUBLANES

    @pl.when(ci == 0)
    def _():
        c_scr[...] = jnp.zeros(c_scr.shape, F32)
        n_scr[...] = jnp.zeros(n_scr.shape, F32)
        m_scr[...] = jnp.zeros(m_scr.shape, F32)
        ext_scr[0:pad, :] = jnp.zeros((pad, ext_scr.shape[1]), F32)

    ext_scr[pad:pad + chunk, :] = qk_ref[...].astype(F32)
    conv = cb_ref[...] + cw_ref[CONV_WIDTH - 1:CONV_WIDTH, :] * ext_scr[pad:pad + chunk, :]
    for tap in range(1, CONV_WIDTH):
        conv = conv + (cw_ref[CONV_WIDTH - 1 - tap:CONV_WIDTH - tap, :]
                       * ext_scr[pad - tap:pad - tap + chunk, :])
    ext_scr[0:pad, :] = ext_scr[chunk:chunk + pad, :]
    qk = conv * _sigmoid(conv)

    g_rows = grow_ref[...] + brow_ref[...]
    g_cols = gcol_ref[...] + bcol_ref[...]
    r_idx = lax.broadcasted_iota(jnp.int32, (chunk, chunk), 0)
    c_idx = lax.broadcasted_iota(jnp.int32, (chunk, chunk), 1)
    causal = r_idx >= c_idx
    tril = jnp.where(causal, 1.0, 0.0).astype(F32)
    triu = jnp.where(r_idx <= c_idx, 1.0, 0.0).astype(F32)
    b_rows = jnp.dot(_log_sigmoid(g_rows), triu, preferred_element_type=F32,
                     precision=lax.Precision.HIGHEST)
    b_cols = jnp.dot(tril, _log_sigmoid(g_cols), preferred_element_type=F32,
                     precision=lax.Precision.HIGHEST)

    q_scale = ML_QK_DIM ** -0.5
    for hd in range(ML_HEADS):
        qf = qk[:, hd * ML_QK_DIM:(hd + 1) * ML_QK_DIM] * q_scale
        kf = qk[:, ML_QK_WIDTH + hd * ML_QK_DIM:ML_QK_WIDTH + (hd + 1) * ML_QK_DIM]
        qb = qf.astype(BF16)
        vb = v_ref[:, hd * ML_V_DIM:(hd + 1) * ML_V_DIM]
        fcol = ML_HEADS + hd
        bt = b_cols[:, fcol:fcol + 1]
        bs = b_rows[fcol:fcol + 1, :]
        i_row = g_rows[hd:hd + 1, :]
        i_col = g_cols[:, hd:hd + 1]
        m_prev = m_scr[hd]
        c_prev = c_scr[hd]
        n_prev = n_scr[hd]

        logd = jnp.where(causal, bt - bs + i_row, NEG_BIG)
        inter = bt + m_prev
        m_t = jnp.maximum(inter, jnp.max(logd, axis=1, keepdims=True))
        dmat = jnp.exp(logd - m_t)
        sc = lax.dot_general(qb, kf.astype(BF16), NT_DIMS, preferred_element_type=F32) * dmat
        w_inter = jnp.exp(inter - m_t)
        num = (w_inter * jnp.dot(qb, c_prev.astype(BF16), preferred_element_type=F32)
               + jnp.dot(sc.astype(BF16), vb, preferred_element_type=F32))
        den = (w_inter * jnp.sum(qf * n_prev, axis=1, keepdims=True)
               + jnp.sum(sc, axis=1, keepdims=True))
        hh = num / jnp.maximum(jnp.abs(den), jnp.exp(-m_t))

        g_last = bt[chunk - 1:chunk, :]
        log_w = g_last - bt + i_col
        m_new = jnp.maximum(g_last + m_prev, jnp.max(log_w, axis=0, keepdims=True))
        kw = kf * jnp.exp(log_w - m_new)
        decay = jnp.exp(g_last + m_prev - m_new)
        c_scr[hd] = decay * c_prev + lax.dot_general(
            kw.astype(BF16), vb, TN_DIMS, preferred_element_type=F32)
        n_scr[hd] = decay * n_prev + jnp.sum(kw, axis=0, keepdims=True)
        m_scr[hd] = m_new

        sl = slice(hd * ML_V_DIM, (hd + 1) * ML_V_DIM)
        hn = hh * lax.rsqrt(jnp.mean(hh * hh, axis=1, keepdims=True) + EPS) * ng_ref[:, sl]
        out_ref[:, sl] = (_sigmoid(og_ref[:, sl].astype(F32)) * hn).astype(out_ref.dtype)


def _mlstm(proj3, g_rows, g_cols3, conv_w, conv_b, b_row, b_col, norm_g, batch, seq, chunk):
    kern = functools.partial(_mlstm_kernel, chunk=chunk)
    return pl.pallas_call(
        kern,
        grid=(batch, seq // chunk),
        in_specs=[
            pl.BlockSpec((None, chunk, D_MODEL), lambda b, c: (b, c, COL_ML_QK)),
            pl.BlockSpec((None, chunk, D_MODEL), lambda b, c: (b, c, COL_ML_V)),
            pl.BlockSpec((None, chunk, D_MODEL), lambda b, c: (b, c, COL_ML_O)),
            pl.BlockSpec((None, 2 * ML_HEADS, chunk), lambda b, c: (b, 0, c)),
            pl.BlockSpec((None, chunk, LANES), lambda b, c: (b, c, 0)),
            pl.BlockSpec((CONV_WIDTH, D_MODEL), lambda b, c: (0, 0)),
            pl.BlockSpec((1, D_MODEL), lambda b, c: (0, 0)),
            pl.BlockSpec((2 * ML_HEADS, 1), lambda b, c: (0, 0)),
            pl.BlockSpec((1, LANES), lambda b, c: (0, 0)),
            pl.BlockSpec((1, D_MODEL), lambda b, c: (0, 0)),
        ],
        out_specs=pl.BlockSpec((None, chunk, D_MODEL), lambda b, c: (b, c, 0)),
        out_shape=jax.ShapeDtypeStruct((batch, seq, D_MODEL), BF16),
        scratch_shapes=[
            pltpu.VMEM((ML_HEADS, ML_QK_DIM, ML_V_DIM), F32),
            pltpu.VMEM((ML_HEADS, 1, ML_QK_DIM), F32),
            pltpu.VMEM((ML_HEADS, 1, 1), F32),
            pltpu.VMEM((chunk + 2 * SUBLANES, D_MODEL), F32),
        ],
        compiler_params=pltpu.CompilerParams(
            dimension_semantics=("parallel", "arbitrary"), vmem_limit_bytes=VMEM_LIMIT),
    )(proj3, proj3, proj3, g_rows, g_cols3, conv_w, conv_b, b_row, b_col, norm_g)


def _rms(x, g):
    return x * lax.rsqrt(jnp.mean(x * x, axis=-1, keepdims=True) + EPS) * g


def _tail_kernel(x_ref, a_ref, m_ref, ga_ref, gm_ref, bm_ref, wa_ref, wm_ref, wo_ref,
                 gmlp_ref, w1_ref, w2_ref, gfin_ref, o_ref):
    ya = jnp.dot(a_ref[...], wa_ref[...], preferred_element_type=F32)
    ym = jnp.dot(m_ref[...], wm_ref[...], preferred_element_type=F32)
    gate_a = _sigmoid(ga_ref[...].astype(F32) + bm_ref[:, :D_MODEL])
    gate_m = _sigmoid(gm_ref[...].astype(F32) + bm_ref[:, D_MODEL:])
    merged = (gate_a * ya + gate_m * ym).astype(BF16)
    x1 = x_ref[...] + jnp.dot(merged, wo_ref[...], preferred_element_type=F32)
    hm = _rms(x1, gmlp_ref[...]).astype(BF16)
    acc = x1
    for c in range(D_FF // D_MODEL):
        cols = slice(c * D_MODEL, (c + 1) * D_MODEL)
        u = jnp.maximum(jnp.dot(hm, w1_ref[:, cols], preferred_element_type=F32), 0.0)
        acc = acc + jnp.dot((u * u).astype(BF16), w2_ref[cols, :], preferred_element_type=F32)
    o_ref[...] = _rms(acc, gfin_ref[...])


def _tail(x2, a2, m2, proj, b_merge, wa, wm, wo, g_mlp, w1, w2, g_fin, tm):
    tokens = x2.shape[0]
    const = lambda i: (0, 0)

    def resident(shape):
        return pl.BlockSpec(shape, const, pipeline_mode=pl.Buffered(1))

    return pl.pallas_call(
        _tail_kernel,
        grid=(tokens // tm,),
        in_specs=[
            pl.BlockSpec((tm, D_MODEL), lambda i: (i, 0)),
            pl.BlockSpec((tm, D_MODEL), lambda i: (i, 0)),
            pl.BlockSpec((tm, D_MODEL), lambda i: (i, 0)),
            pl.BlockSpec((tm, D_MODEL), lambda i: (i, COL_MG_A)),
            pl.BlockSpec((tm, D_MODEL), lambda i: (i, COL_MG_M)),
            resident((1, 2 * D_MODEL)),
            resident((D_MODEL, D_MODEL)),
            resident((D_MODEL, D_MODEL)),
            resident((D_MODEL, D_MODEL)),
            resident((1, D_MODEL)),
            resident((D_MODEL, D_FF)),
            resident((D_FF, D_MODEL)),
            resident((1, D_MODEL)),
        ],
        out_specs=pl.BlockSpec((tm, D_MODEL), lambda i: (i, 0)),
        out_shape=jax.ShapeDtypeStruct((tokens, D_MODEL), F32),
        compiler_params=pltpu.CompilerParams(
            dimension_semantics=("parallel",), vmem_limit_bytes=VMEM_LIMIT),
    )(x2, a2, m2, proj, proj, b_merge, wa, wm, wo, g_mlp, w1, w2, g_fin)


def kernel(x, norm_mix_g, w_in, b_gates, conv_w, conv_b, lam, da_norm_g, ml_norm_g, b_merge,
           w_branch_a, w_branch_m, w_out, norm_mlp_g, w_ff1, w_ff2, norm_final_g):
    batch, seq, _ = x.shape
    tokens = batch * seq
    x2 = x.reshape(tokens, D_MODEL)

    w = w_in[0]
    o_q, o_k, o_v = 0, D_MODEL, 2 * D_MODEL
    o_mq = 3 * D_MODEL
    o_mv = o_mq + 2 * ML_QK_WIDTH
    o_if = o_mv + D_MODEL
    o_mo = o_if + 2 * ML_HEADS
    o_mg = o_mo + D_MODEL
    q_scale = DA_HEAD_DIM ** -0.5
    w_main = jnp.concatenate(
        [w[:, o_q:o_k] * q_scale, w[:, o_k:o_v], w[:, o_mq:o_if], w[:, o_mo:]], axis=1).astype(BF16)
    wvt = w[:, o_v:o_mq].T.astype(BF16)
    wif = jnp.pad(w[:, o_if:o_mo], ((0, 0), (0, LANES - 2 * ML_HEADS))).astype(BF16)

    tm_proj = min(1024, seq)
    proj, vt, ifg = _in_proj(x2, norm_mix_g, w_main, wvt, wif, batch, seq, tm_proj)
    proj3 = proj.reshape(batch, seq, N_PROJ_BLOCKS * D_MODEL)

    slopes = 2.0 ** (-8.0 * jnp.arange(1, DA_HEADS + 1, dtype=F32) / DA_HEADS)
    attn_tile = min(512, seq)
    a_out = _attention(slopes, lam[0], proj3, vt, da_norm_g, batch, seq, attn_tile)

    ifg3 = ifg.reshape(batch, seq, LANES)
    g_rows = jnp.swapaxes(ifg3[:, :, :2 * ML_HEADS], 1, 2)
    b_row = b_gates[0].reshape(2 * ML_HEADS, 1)
    b_col = jnp.pad(b_gates, ((0, 0), (0, LANES - 2 * ML_HEADS)))
    chunk = min(256, seq)
    m_out = _mlstm(proj3, g_rows, ifg3, conv_w[0], conv_b, b_row, b_col, ml_norm_g,
                   batch, seq, chunk)

    out = _tail(x2, a_out.reshape(tokens, D_MODEL), m_out.reshape(tokens, D_MODEL), proj,
                b_merge, w_branch_a[0].astype(BF16), w_branch_m[0].astype(BF16),
                w_out[0].astype(BF16), norm_mlp_g, w_ff1[0].astype(BF16), w_ff2[0].astype(BF16),
                norm_final_g.reshape(1, D_MODEL), min(256, tokens))
    return out.reshape(batch, seq, D_MODEL)
```

```python
import functools
import math

import jax
import jax.numpy as jnp
from jax import lax
from jax.experimental import pallas as pl
from jax.experimental.pallas import tpu as pltpu

F32 = jnp.float32
BF16 = jnp.bfloat16

D_MODEL = 1024
DA_HEADS = 8
DA_HEAD_DIM = 64
DA_V_DIM = 2 * DA_HEAD_DIM
ML_HEADS = 4
ML_V_DIM = D_MODEL // ML_HEADS
ML_QK_DIM = ML_V_DIM // 2
ML_QK_WIDTH = ML_HEADS * ML_QK_DIM
CONV_WIDTH = 4
D_FF = 4 * D_MODEL
EPS = 1e-6
LAM_INIT = 0.8 - 0.6 * math.exp(-0.3 * 0)
NEG_BIG = -1e30
LOG2E = math.log2(math.e)
ONES_ROWS = 16

LANES = 128
SUBLANES = 8
VMEM_LIMIT = 56 * 1024 * 1024

COL_DA_Q, COL_DA_K, COL_ML_QK, COL_ML_V, COL_ML_O, COL_MG_A, COL_MG_M = range(7)
N_PROJ_BLOCKS = 7

NT_DIMS = (((1,), (1,)), ((), ()))
TN_DIMS = (((0,), (0,)), ((), ()))


def _sigmoid(x):
    return 1.0 / (1.0 + jnp.exp(-x))


def _log_sigmoid(x):
    return jnp.minimum(x, 0.0) - jnp.log(1.0 + jnp.exp(-jnp.abs(x)))


def _in_proj_kernel(x_ref, g_ref, w_ref, wvt_ref, wif_ref, proj_ref, vt_ref, if_ref, h_scr):
    j = pl.program_id(1)

    @pl.when(j == 0)
    def _():
        x = x_ref[...]
        h = x * lax.rsqrt(jnp.mean(x * x, axis=-1, keepdims=True) + EPS) * g_ref[...]
        hb = h.astype(BF16)
        h_scr[...] = hb
        if_ref[...] = jnp.dot(hb, wif_ref[...], preferred_element_type=F32)
        vt = lax.dot_general(wvt_ref[...], hb, NT_DIMS, preferred_element_type=F32)
        vt_ref[...] = vt.astype(BF16)

    proj_ref[...] = jnp.dot(h_scr[...], w_ref[...], preferred_element_type=F32).astype(BF16)


def _in_proj(x2, g, w_main, wvt, wif, batch, seq, tm):
    tokens = batch * seq
    nsb = seq // tm
    return pl.pallas_call(
        _in_proj_kernel,
        grid=(tokens // tm, N_PROJ_BLOCKS),
        in_specs=[
            pl.BlockSpec((tm, D_MODEL), lambda i, j: (i, 0)),
            pl.BlockSpec((1, D_MODEL), lambda i, j: (0, 0)),
            pl.BlockSpec((D_MODEL, D_MODEL), lambda i, j: (0, j)),
            pl.BlockSpec((D_MODEL, D_MODEL), lambda i, j: (0, 0)),
            pl.BlockSpec((D_MODEL, LANES), lambda i, j: (0, 0)),
        ],
        out_specs=[
            pl.BlockSpec((tm, D_MODEL), lambda i, j: (i, j)),
            pl.BlockSpec((None, D_MODEL, tm), lambda i, j: (i // nsb, 0, i % nsb)),
            pl.BlockSpec((tm, LANES), lambda i, j: (i, 0)),
        ],
        out_shape=[
            jax.ShapeDtypeStruct((tokens, N_PROJ_BLOCKS * D_MODEL), BF16),
            jax.ShapeDtypeStruct((batch, D_MODEL, seq), BF16),
            jax.ShapeDtypeStruct((tokens, LANES), F32),
        ],
        scratch_shapes=[pltpu.VMEM((tm, D_MODEL), BF16)],
        compiler_params=pltpu.CompilerParams(
            dimension_semantics=("parallel", "arbitrary"), vmem_limit_bytes=VMEM_LIMIT),
    )(x2, g, w_main, wvt, wif)


def _attn_kernel(slopes_ref, lam_ref, q_ref, k_ref, vt_ref, g_ref, o_ref,
                 qq_scr, pos_scr, s_scr, p_scr, acc_scr, m_scr, *, tile, heads):
    group = pl.program_id(1)
    qi = pl.program_id(2)
    kt_size = tile // 2
    slope2 = [slopes_ref[group * heads + hh] * LOG2E for hh in range(heads)]

    def head_cols(hh):
        return slice(hh * DA_V_DIM, (hh + 1) * DA_V_DIM)

    key_off = lax.broadcasted_iota(jnp.int32, pos_scr.shape, 0).astype(F32)
    lane = lax.broadcasted_iota(jnp.int32, pos_scr.shape, 1)
    pos_scr[...] = jnp.where(lane < 2, key_off, 0.0).astype(BF16)
    for hh in range(heads):
        qt = q_ref[:, head_cols(hh)].astype(F32).T * LOG2E
        d_idx = lax.broadcasted_iota(jnp.int32, qt.shape, 0)
        q0 = jnp.where(d_idx < DA_HEAD_DIM, qt, 0.0)
        q1 = jnp.where(d_idx >= DA_HEAD_DIM, qt, 0.0)
        qq_scr[hh, 0:DA_V_DIM, :] = jnp.concatenate([q0, q1], axis=1).astype(BF16)
        slope_vec = jnp.full((DA_V_DIM, 2 * tile), slope2[hh], F32)
        slope_hi = slope_vec.astype(BF16).astype(F32)
        feat = lax.broadcasted_iota(jnp.int32, slope_vec.shape, 0)
        qq_scr[hh, DA_V_DIM:, :] = jnp.where(
            feat == 0, slope_hi, jnp.where(feat == 1, slope_vec - slope_hi, 0.0)).astype(BF16)
        m_scr[hh] = jnp.full(m_scr.shape[1:], NEG_BIG, F32)
        acc_scr[hh] = jnp.zeros(acc_scr.shape[1:], F32)
        p_scr[2 * hh + 1] = jnp.zeros(p_scr.shape[1:], BF16)

    def scores(hh, t, slot):
        start = pl.multiple_of(t * kt_size, kt_size)
        kt = jnp.concatenate([k_ref[pl.ds(start, kt_size), head_cols(hh)], pos_scr[...]], axis=1)
        s_scr[2 * hh + slot] = jnp.dot(kt, qq_scr[hh], preferred_element_type=F32)

    ones_rows = jnp.ones((ONES_ROWS, kt_size), BF16)

    def weighted_values(hh, t, slot):
        start = pl.multiple_of(jnp.maximum(t, 0) * kt_size, kt_size)
        vt = jnp.concatenate([vt_ref[head_cols(hh), pl.ds(start, kt_size)], ones_rows], axis=0)
        return jnp.dot(vt, p_scr[2 * hh + slot], preferred_element_type=F32)

    def step(hh, t, slot, mask=None, prefetch=True):
        if prefetch:
            scores(hh, t + 1, 1 - slot)
        s = s_scr[2 * hh + slot]
        if mask is not None:
            s = jnp.where(mask, s, NEG_BIG)
        c = slope2[hh] * (t * kt_size - qi * tile).astype(F32)
        m_old = m_scr[hh]
        m_new = jnp.maximum(m_old, jnp.max(s, axis=0, keepdims=True) + c)
        p = jnp.exp2(s - (m_new - c))
        alpha = jnp.exp2(m_old - m_new)
        p_scr[2 * hh + slot] = p.astype(BF16)
        m_scr[hh] = m_new
        acc_scr[hh] = alpha * (acc_scr[hh] + weighted_values(hh, t - 1, 1 - slot))

    for hh in range(heads):
        scores(hh, 0, 0)

    def body(i, carry):
        for slot in range(2):
            for hh in range(heads):
                step(hh, 2 * i + slot, slot)
        return carry

    lax.fori_loop(0, qi, body, 0)

    kk = lax.broadcasted_iota(jnp.int32, (kt_size, 2 * tile), 0)
    qpos = lax.broadcasted_iota(jnp.int32, (kt_size, 2 * tile), 1)
    qpos = jnp.where(qpos >= tile, qpos - tile, qpos)
    for hh in range(heads):
        step(hh, 2 * qi, 0, mask=kk <= qpos)
    for hh in range(heads):
        step(hh, 2 * qi + 1, 1, mask=kk + kt_size <= qpos, prefetch=False)

    lam = lam_ref[...]
    lam_full = (jnp.exp(jnp.sum(lam[0:1] * lam[1:2], axis=1, keepdims=True))
                - jnp.exp(jnp.sum(lam[2:3] * lam[3:4], axis=1, keepdims=True)) + LAM_INIT)
    for hh in range(heads):
        acc = acc_scr[hh] + weighted_values(hh, 2 * qi + 1, 1)
        l = acc[DA_V_DIM:DA_V_DIM + 1, :]
        acc = acc[:DA_V_DIM, :]
        o = acc[:, :tile] / l[:, :tile] - lam_full * (acc[:, tile:] / l[:, tile:])
        o = o * lax.rsqrt(jnp.mean(o * o, axis=0, keepdims=True) + EPS) * (1.0 - LAM_INIT)
        o_ref[:, head_cols(hh)] = (o.T * g_ref[:, head_cols(hh)]).astype(o_ref.dtype)


def _attention(slopes, lam, proj3, vt, g, batch, seq, tile, heads):
    kern = functools.partial(_attn_kernel, tile=tile, heads=heads)
    width = heads * DA_V_DIM
    groups = DA_HEADS // heads
    return pl.pallas_call(
        kern,
        grid=(batch, groups, seq // tile),
        in_specs=[
            pl.BlockSpec(memory_space=pltpu.SMEM),
            pl.BlockSpec((4, DA_HEAD_DIM), lambda b, h, i: (0, 0)),
            pl.BlockSpec((None, tile, width), lambda b, h, i: (b, i, COL_DA_Q * groups + h)),
            pl.BlockSpec((None, seq, width), lambda b, h, i: (b, 0, COL_DA_K * groups + h)),
            pl.BlockSpec((None, width, seq), lambda b, h, i: (b, h, 0)),
            pl.BlockSpec((1, width), lambda b, h, i: (0, h)),
        ],
        out_specs=pl.BlockSpec((None, tile, width), lambda b, h, i: (b, i, h)),
        out_shape=jax.ShapeDtypeStruct((batch, seq, D_MODEL), BF16),
        scratch_shapes=[
            pltpu.VMEM((heads, 2 * DA_V_DIM, 2 * tile), BF16),
            pltpu.VMEM((tile // 2, DA_V_DIM), BF16),
            pltpu.VMEM((2 * heads, tile // 2, 2 * tile), F32),
            pltpu.VMEM((2 * heads, tile // 2, 2 * tile), BF16),
            pltpu.VMEM((heads, DA_V_DIM + ONES_ROWS, 2 * tile), F32),
            pltpu.VMEM((heads, 1, 2 * tile), F32),
        ],
        compiler_params=pltpu.CompilerParams(
            dimension_semantics=("parallel", "parallel", "arbitrary"),
            vmem_limit_bytes=VMEM_LIMIT),
    )(slopes, lam, proj3, proj3, vt, g)


def _mlstm_kernel(qk_ref, v_ref, og_ref, grow_ref, gcol_ref, cw_ref, cb_ref, brow_ref, bcol_ref,
                  ng_ref, out_ref, c_scr, n_scr, m_scr, ext_scr, *, chunk):
    ci = pl.program_id(1)
    pad = SUBLANES

    @pl.when(ci == 0)
    def _():
        c_scr[...] = jnp.zeros(c_scr.shape, F32)
        n_scr[...] = jnp.zeros(n_scr.shape, F32)
        m_scr[...] = jnp.zeros(m_scr.shape, F32)
        ext_scr[0:pad, :] = jnp.zeros((pad, ext_scr.shape[1]), F32)

    ext_scr[pad:pad + chunk, :] = qk_ref[...].astype(F32)
    conv = cb_ref[...] + cw_ref[CONV_WIDTH - 1:CONV_WIDTH, :] * ext_scr[pad:pad + chunk, :]
    for tap in range(1, CONV_WIDTH):
        conv = conv + (cw_ref[CONV_WIDTH - 1 - tap:CONV_WIDTH - tap, :]
                       * ext_scr[pad - tap:pad - tap + chunk, :])
    ext_scr[0:pad, :] = ext_scr[chunk:chunk + pad, :]
    qk = conv * _sigmoid(conv)

    g_rows = grow_ref[...] + brow_ref[...]
    g_cols = gcol_ref[...] + bcol_ref[...]
    r_idx = lax.broadcasted_iota(jnp.int32, (chunk, chunk), 0)
    c_idx = lax.broadcasted_iota(jnp.int32, (chunk, chunk), 1)
    causal = r_idx >= c_idx
    tril = jnp.where(causal, 1.0, 0.0).astype(F32)
    triu = jnp.where(r_idx <= c_idx, 1.0, 0.0).astype(F32)
    b_rows = jnp.dot(_log_sigmoid(g_rows), triu, preferred_element_type=F32,
                     precision=lax.Precision.HIGHEST)
    b_cols = jnp.dot(tril, _log_sigmoid(g_cols), preferred_element_type=F32,
                     precision=lax.Precision.HIGHEST)

    q_scale = ML_QK_DIM ** -0.5
    for hd in range(ML_HEADS):
        qf = qk[:, hd * ML_QK_DIM:(hd + 1) * ML_QK_DIM] * q_scale
        kf = qk[:, ML_QK_WIDTH + hd * ML_QK_DIM:ML_QK_WIDTH + (hd + 1) * ML_QK_DIM]
        qb = qf.astype(BF16)
        vb = v_ref[:, hd * ML_V_DIM:(hd + 1) * ML_V_DIM]
        fcol = ML_HEADS + hd
        bt = b_cols[:, fcol:fcol + 1]
        bs = b_rows[fcol:fcol + 1, :]
        i_row = g_rows[hd:hd + 1, :]
        i_col = g_cols[:, hd:hd + 1]
        m_prev = m_scr[hd]
        c_prev = c_scr[hd]
        n_prev = n_scr[hd]

        logd = jnp.where(causal, bt - bs + i_row, NEG_BIG)
        inter = bt + m_prev
        m_t = jnp.maximum(inter, jnp.max(logd, axis=1, keepdims=True))
        dmat = jnp.exp(logd - m_t)
        sc = lax.dot_general(qb, kf.astype(BF16), NT_DIMS, preferred_element_type=F32) * dmat
        w_inter = jnp.exp(inter - m_t)
        num = (w_inter * jnp.dot(qb, c_prev.astype(BF16), preferred_element_type=F32)
               + jnp.dot(sc.astype(BF16), vb, preferred_element_type=F32))
        den = (w_inter * jnp.sum(qf * n_prev, axis=1, keepdims=True)
               + jnp.sum(sc, axis=1, keepdims=True))
        hh = num / jnp.maximum(jnp.abs(den), jnp.exp(-m_t))

        g_last = bt[chunk - 1:chunk, :]
        log_w = g_last - bt + i_col
        m_new = jnp.maximum(g_last + m_prev, jnp.max(log_w, axis=0, keepdims=True))
        kw = kf * jnp.exp(log_w - m_new)
        decay = jnp.exp(g_last + m_prev - m_new)
        c_scr[hd] = decay * c_prev + lax.dot_general(
            kw.astype(BF16), vb, TN_DIMS, preferred_element_type=F32)
        n_scr[hd] = decay * n_prev + jnp.sum(kw, axis=0, keepdims=True)
        m_scr[hd] = m_new

        sl = slice(hd * ML_V_DIM, (hd + 1) * ML_V_DIM)
        hn = hh * lax.rsqrt(jnp.mean(hh * hh, axis=1, keepdims=True) + EPS) * ng_ref[:, sl]
        out_ref[:, sl] = (_sigmoid(og_ref[:, sl].astype(F32)) * hn).astype(out_ref.dtype)


def _mlstm(proj3, g_rows, g_cols3, conv_w, conv_b, b_row, b_col, norm_g, batch, seq, chunk):
    kern = functools.partial(_mlstm_kernel, chunk=chunk)
    return pl.pallas_call(
        kern,
        grid=(batch, seq // chunk),
        in_specs=[
            pl.BlockSpec((None, chunk, D_MODEL), lambda b, c: (b, c, COL_ML_QK)),
            pl.BlockSpec((None, chunk, D_MODEL), lambda b, c: (b, c, COL_ML_V)),
            pl.BlockSpec((None, chunk, D_MODEL), lambda b, c: (b, c, COL_ML_O)),
            pl.BlockSpec((None, 2 * ML_HEADS, chunk), lambda b, c: (b, 0, c)),
            pl.BlockSpec((None, chunk, LANES), lambda b, c: (b, c, 0)),
            pl.BlockSpec((CONV_WIDTH, D_MODEL), lambda b, c: (0, 0)),
            pl.BlockSpec((1, D_MODEL), lambda b, c: (0, 0)),
            pl.BlockSpec((2 * ML_HEADS, 1), lambda b, c: (0, 0)),
            pl.BlockSpec((1, LANES), lambda b, c: (0, 0)),
            pl.BlockSpec((1, D_MODEL), lambda b, c: (0, 0)),
        ],
        out_specs=pl.BlockSpec((None, chunk, D_MODEL), lambda b, c: (b, c, 0)),
        out_shape=jax.ShapeDtypeStruct((batch, seq, D_MODEL), BF16),
        scratch_shapes=[
            pltpu.VMEM((ML_HEADS, ML_QK_DIM, ML_V_DIM), F32),
            pltpu.VMEM((ML_HEADS, 1, ML_QK_DIM), F32),
            pltpu.VMEM((ML_HEADS, 1, 1), F32),
            pltpu.VMEM((chunk + 2 * SUBLANES, D_MODEL), F32),
        ],
        compiler_params=pltpu.CompilerParams(
            dimension_semantics=("parallel", "arbitrary"), vmem_limit_bytes=VMEM_LIMIT),
    )(proj3, proj3, proj3, g_rows, g_cols3, conv_w, conv_b, b_row, b_col, norm_g)


def _rms(x, g):
    return x * lax.rsqrt(jnp.mean(x * x, axis=-1, keepdims=True) + EPS) * g


def _tail_kernel(x_ref, a_ref, m_ref, ga_ref, gm_ref, bm_ref, wa_ref, wm_ref, wo_ref,
                 gmlp_ref, w1_ref, w2_ref, gfin_ref, o_ref):
    ya = jnp.dot(a_ref[...], wa_ref[...], preferred_element_type=F32)
    ym = jnp.dot(m_ref[...], wm_ref[...], preferred_element_type=F32)
    gate_a = _sigmoid(ga_ref[...].astype(F32) + bm_ref[:, :D_MODEL])
    gate_m = _sigmoid(gm_ref[...].astype(F32) + bm_ref[:, D_MODEL:])
    merged = (gate_a * ya + gate_m * ym).astype(BF16)
    x1 = x_ref[...] + jnp.dot(merged, wo_ref[...], preferred_element_type=F32)
    hm = _rms(x1, gmlp_ref[...]).astype(BF16)
    acc = x1
    for c in range(D_FF // D_MODEL):
        cols = slice(c * D_MODEL, (c + 1) * D_MODEL)
        u = jnp.maximum(jnp.dot(hm, w1_ref[:, cols], preferred_element_type=F32), 0.0)
        acc = acc + jnp.dot((u * u).astype(BF16), w2_ref[cols, :], preferred_element_type=F32)
    o_ref[...] = _rms(acc, gfin_ref[...])


def _tail(x2, a2, m2, proj, b_merge, wa, wm, wo, g_mlp, w1, w2, g_fin, tm):
    tokens = x2.shape[0]
    const = lambda i: (0, 0)

    def resident(shape):
        return pl.BlockSpec(shape, const, pipeline_mode=pl.Buffered(1))

    return pl.pallas_call(
        _tail_kernel,
        grid=(tokens // tm,),
        in_specs=[
            pl.BlockSpec((tm, D_MODEL), lambda i: (i, 0)),
            pl.BlockSpec((tm, D_MODEL), lambda i: (i, 0)),
            pl.BlockSpec((tm, D_MODEL), lambda i: (i, 0)),
            pl.BlockSpec((tm, D_MODEL), lambda i: (i, COL_MG_A)),
            pl.BlockSpec((tm, D_MODEL), lambda i: (i, COL_MG_M)),
            resident((1, 2 * D_MODEL)),
            resident((D_MODEL, D_MODEL)),
            resident((D_MODEL, D_MODEL)),
            resident((D_MODEL, D_MODEL)),
            resident((1, D_MODEL)),
            resident((D_MODEL, D_FF)),
            resident((D_FF, D_MODEL)),
            resident((1, D_MODEL)),
        ],
        out_specs=pl.BlockSpec((tm, D_MODEL), lambda i: (i, 0)),
        out_shape=jax.ShapeDtypeStruct((tokens, D_MODEL), F32),
        compiler_params=pltpu.CompilerParams(
            dimension_semantics=("parallel",), vmem_limit_bytes=VMEM_LIMIT),
    )(x2, a2, m2, proj, proj, b_merge, wa, wm, wo, g_mlp, w1, w2, g_fin)


def kernel(x, norm_mix_g, w_in, b_gates, conv_w, conv_b, lam, da_norm_g, ml_norm_g, b_merge,
           w_branch_a, w_branch_m, w_out, norm_mlp_g, w_ff1, w_ff2, norm_final_g):
    batch, seq, _ = x.shape
    tokens = batch * seq
    x2 = x.reshape(tokens, D_MODEL)

    w = w_in[0]
    o_q, o_k, o_v = 0, D_MODEL, 2 * D_MODEL
    o_mq = 3 * D_MODEL
    o_mv = o_mq + 2 * ML_QK_WIDTH
    o_if = o_mv + D_MODEL
    o_mo = o_if + 2 * ML_HEADS
    o_mg = o_mo + D_MODEL
    q_scale = DA_HEAD_DIM ** -0.5
    w_main = jnp.concatenate(
        [w[:, o_q:o_k] * q_scale, w[:, o_k:o_v], w[:, o_mq:o_if], w[:, o_mo:]], axis=1).astype(BF16)
    wvt = w[:, o_v:o_mq].T.astype(BF16)
    wif = jnp.pad(w[:, o_if:o_mo], ((0, 0), (0, LANES - 2 * ML_HEADS))).astype(BF16)

    tm_proj = min(1024, seq)
    proj, vt, ifg = _in_proj(x2, norm_mix_g, w_main, wvt, wif, batch, seq, tm_proj)
    proj3 = proj.reshape(batch, seq, N_PROJ_BLOCKS * D_MODEL)

    slopes = 2.0 ** (-8.0 * jnp.arange(1, DA_HEADS + 1, dtype=F32) / DA_HEADS)
    attn_tile = min(512, seq)
    a_out = _attention(slopes, lam[0], proj3, vt, da_norm_g, batch, seq, attn_tile, 2)

    ifg3 = ifg.reshape(batch, seq, LANES)
    g_rows = jnp.swapaxes(ifg3[:, :, :2 * ML_HEADS], 1, 2)
    b_row = b_gates[0].reshape(2 * ML_HEADS, 1)
    b_col = jnp.pad(b_gates, ((0, 0), (0, LANES - 2 * ML_HEADS)))
    chunk = min(256, seq)
    m_out = _mlstm(proj3, g_rows, ifg3, conv_w[0], conv_b, b_row, b_col, ml_norm_g,
                   batch, seq, chunk)

    out = _tail(x2, a_out.reshape(tokens, D_MODEL), m_out.reshape(tokens, D_MODEL), proj,
                b_merge, w_branch_a[0].astype(BF16), w_branch_m[0].astype(BF16),
                w_out[0].astype(BF16), norm_mlp_g, w_ff1[0].astype(BF16), w_ff2[0].astype(BF16),
                norm_final_g.reshape(1, D_MODEL), min(256, tokens))
    return out.reshape(batch, seq, D_MODEL)
```

```python
import functools
import math

import jax
import jax.numpy as jnp
from jax import lax
from jax.experimental import pallas as pl
from jax.experimental.pallas import tpu as pltpu

F32 = jnp.float32
BF16 = jnp.bfloat16

D_MODEL = 1024
DA_HEADS = 8
DA_HEAD_DIM = 64
DA_V_DIM = 2 * DA_HEAD_DIM
ML_HEADS = 4
ML_V_DIM = D_MODEL // ML_HEADS
ML_QK_DIM = ML_V_DIM // 2
ML_QK_WIDTH = ML_HEADS * ML_QK_DIM
CONV_WIDTH = 4
D_FF = 4 * D_MODEL
EPS = 1e-6
LAM_INIT = 0.8 - 0.6 * math.exp(-0.3 * 0)
NEG_BIG = -1e30
LOG2E = math.log2(math.e)
ONES_ROWS = 16

LANES = 128
SUBLANES = 8
VMEM_LIMIT = 56 * 1024 * 1024

COL_DA_Q, COL_DA_K, COL_ML_QK, COL_ML_V, COL_ML_O, COL_MG_A, COL_MG_M = range(7)
N_PROJ_BLOCKS = 7

OFF_DA_Q = 0
OFF_DA_K = OFF_DA_Q + D_MODEL
OFF_DA_V = OFF_DA_K + D_MODEL
OFF_ML_Q = OFF_DA_V + D_MODEL
OFF_ML_V = OFF_ML_Q + 2 * ML_QK_WIDTH
OFF_ML_IF = OFF_ML_V + D_MODEL
OFF_ML_O = OFF_ML_IF + 2 * ML_HEADS
D_IN = OFF_ML_O + 3 * D_MODEL

NT_DIMS = (((1,), (1,)), ((), ()))
TN_DIMS = (((0,), (0,)), ((), ()))
TT_DIMS = (((0,), (1,)), ((), ()))


def _sigmoid(x):
    return 1.0 / (1.0 + jnp.exp(-x))


def _log_sigmoid(x):
    return jnp.minimum(x, 0.0) - jnp.log(1.0 + jnp.exp(-jnp.abs(x)))


def _in_proj_kernel(x_ref, g_ref, w_ref, wtail_ref, proj_ref, vt_ref, if_ref, ift_ref):
    x = x_ref[...]
    hb = (x * lax.rsqrt(jnp.mean(x * x, axis=-1, keepdims=True) + EPS) * g_ref[...]).astype(BF16)

    def project(w_cols):
        return jnp.dot(hb, w_cols, preferred_element_type=F32)

    def project_t(w_cols):
        return lax.dot_general(w_cols, hb, TT_DIMS, preferred_element_type=F32)

    proj_ref[:, 0:D_MODEL] = (project(w_ref[:, OFF_DA_Q:OFF_DA_K]) * DA_HEAD_DIM ** -0.5).astype(BF16)
    proj_ref[:, D_MODEL:2 * D_MODEL] = project(w_ref[:, OFF_DA_K:OFF_DA_V]).astype(BF16)
    vt_ref[...] = project_t(w_ref[:, OFF_DA_V:OFF_ML_Q]).astype(BF16)
    for blk in range(2):
        src = OFF_ML_Q + blk * D_MODEL
        dst = (COL_ML_QK + blk) * D_MODEL
        proj_ref[:, dst:dst + D_MODEL] = project(w_ref[:, src:src + D_MODEL]).astype(BF16)
    w_if = w_ref[:, OFF_ML_IF:OFF_ML_IF + LANES]
    if_ref[...] = project(w_if)
    ift_ref[...] = project_t(w_if)
    for blk in range(3):
        dst = (COL_ML_O + blk) * D_MODEL
        proj_ref[:, dst:dst + D_MODEL] = project(
            wtail_ref[:, blk * D_MODEL:(blk + 1) * D_MODEL]).astype(BF16)


def _in_proj(x2, g, w_bf, w_tail, batch, seq, tm):
    tokens = batch * seq
    nsb = seq // tm

    def resident(arr):
        return pl.BlockSpec(arr.shape, lambda i: (0, 0), pipeline_mode=pl.Buffered(1))

    return pl.pallas_call(
        _in_proj_kernel,
        grid=(tokens // tm,),
        in_specs=[
            pl.BlockSpec((tm, D_MODEL), lambda i: (i, 0)),
            resident(g),
            resident(w_bf),
            resident(w_tail),
        ],
        out_specs=[
            pl.BlockSpec((tm, N_PROJ_BLOCKS * D_MODEL), lambda i: (i, 0)),
            pl.BlockSpec((None, D_MODEL, tm), lambda i: (i // nsb, 0, i % nsb)),
            pl.BlockSpec((tm, LANES), lambda i: (i, 0)),
            pl.BlockSpec((None, LANES, tm), lambda i: (i // nsb, 0, i % nsb)),
        ],
        out_shape=[
            jax.ShapeDtypeStruct((tokens, N_PROJ_BLOCKS * D_MODEL), BF16),
            jax.ShapeDtypeStruct((batch, D_MODEL, seq), BF16),
            jax.ShapeDtypeStruct((tokens, LANES), F32),
            jax.ShapeDtypeStruct((batch, LANES, seq), F32),
        ],
        compiler_params=pltpu.CompilerParams(
            dimension_semantics=("parallel",), vmem_limit_bytes=VMEM_LIMIT),
    )(x2, g, w_bf, w_tail)


def _attn_kernel(slopes_ref, lam_ref, q_ref, k_ref, vt_ref, g_ref, o_ref,
                 qq_scr, pos_scr, s_scr, p_scr, acc_scr, m_scr, *, tile, heads):
    group = pl.program_id(1)
    qi = pl.program_id(2)
    kt_size = tile // 2
    slope2 = [slopes_ref[group * heads + hh] * LOG2E for hh in range(heads)]

    def head_cols(hh):
        return slice(hh * DA_V_DIM, (hh + 1) * DA_V_DIM)

    key_off = lax.broadcasted_iota(jnp.int32, pos_scr.shape, 0).astype(F32)
    lane = lax.broadcasted_iota(jnp.int32, pos_scr.shape, 1)
    pos_scr[...] = jnp.where(lane < 2, key_off, 0.0).astype(BF16)
    for hh in range(heads):
        qt = q_ref[:, head_cols(hh)].astype(F32).T * LOG2E
        d_idx = lax.broadcasted_iota(jnp.int32, qt.shape, 0)
        q0 = jnp.where(d_idx < DA_HEAD_DIM, qt, 0.0)
        q1 = jnp.where(d_idx >= DA_HEAD_DIM, qt, 0.0)
        qq_scr[hh, 0:DA_V_DIM, :] = jnp.concatenate([q0, q1], axis=1).astype(BF16)
        slope_vec = jnp.full((DA_V_DIM, 2 * tile), slope2[hh], F32)
        slope_hi = slope_vec.astype(BF16).astype(F32)
        feat = lax.broadcasted_iota(jnp.int32, slope_vec.shape, 0)
        qq_scr[hh, DA_V_DIM:, :] = jnp.where(
            feat == 0, slope_hi, jnp.where(feat == 1, slope_vec - slope_hi, 0.0)).astype(BF16)
        m_scr[hh] = jnp.full(m_scr.shape[1:], NEG_BIG, F32)
        acc_scr[hh] = jnp.zeros(acc_scr.shape[1:], F32)
        p_scr[2 * hh + 1] = jnp.zeros(p_scr.shape[1:], BF16)

    def scores(hh, t, slot):
        start = pl.multiple_of(t * kt_size, kt_size)
        kt = jnp.concatenate([k_ref[pl.ds(start, kt_size), head_cols(hh)], pos_scr[...]], axis=1)
        s_scr[2 * hh + slot] = jnp.dot(kt, qq_scr[hh], preferred_element_type=F32)

    ones_rows = jnp.ones((ONES_ROWS, kt_size), BF16)

    def weighted_values(hh, t, slot):
        start = pl.multiple_of(jnp.maximum(t, 0) * kt_size, kt_size)
        vt = jnp.concatenate([vt_ref[head_cols(hh), pl.ds(start, kt_size)], ones_rows], axis=0)
        return jnp.dot(vt, p_scr[2 * hh + slot], preferred_element_type=F32)

    def step(hh, t, slot, mask=None, prefetch=True):
        if prefetch:
            scores(hh, t + 1, 1 - slot)
        s = s_scr[2 * hh + slot]
        if mask is not None:
            s = jnp.where(mask, s, NEG_BIG)
        c = slope2[hh] * (t * kt_size - qi * tile).astype(F32)
        m_old = m_scr[hh]
        m_new = jnp.maximum(m_old, jnp.max(s, axis=0, keepdims=True) + c)
        p = jnp.exp2(s - (m_new - c))
        alpha = jnp.exp2(m_old - m_new)
        p_scr[2 * hh + slot] = p.astype(BF16)
        m_scr[hh] = m_new
        acc_scr[hh] = alpha * (acc_scr[hh] + weighted_values(hh, t - 1, 1 - slot))

    for hh in range(heads):
        scores(hh, 0, 0)

    def body(i, carry):
        for slot in range(2):
            for hh in range(heads):
                step(hh, 2 * i + slot, slot)
        return carry

    lax.fori_loop(0, qi, body, 0)

    kk = lax.broadcasted_iota(jnp.int32, (kt_size, 2 * tile), 0)
    qpos = lax.broadcasted_iota(jnp.int32, (kt_size, 2 * tile), 1)
    qpos = jnp.where(qpos >= tile, qpos - tile, qpos)
    for hh in range(heads):
        step(hh, 2 * qi, 0, mask=kk <= qpos)
    for hh in range(heads):
        step(hh, 2 * qi + 1, 1, mask=kk + kt_size <= qpos, prefetch=False)

    lam = lam_ref[...]
    lam_full = (jnp.exp(jnp.sum(lam[0:1] * lam[1:2], axis=1, keepdims=True))
                - jnp.exp(jnp.sum(lam[2:3] * lam[3:4], axis=1, keepdims=True)) + LAM_INIT)
    for hh in range(heads):
        acc = acc_scr[hh] + weighted_values(hh, 2 * qi + 1, 1)
        l = acc[DA_V_DIM:DA_V_DIM + 1, :]
        acc = acc[:DA_V_DIM, :]
        o = acc[:, :tile] / l[:, :tile] - lam_full * (acc[:, tile:] / l[:, tile:])
        o = o * lax.rsqrt(jnp.mean(o * o, axis=0, keepdims=True) + EPS) * (1.0 - LAM_INIT)
        o_ref[:, head_cols(hh)] = (o.T * g_ref[:, head_cols(hh)]).astype(o_ref.dtype)


def _attention(slopes, lam, proj3, vt, g, batch, seq, tile, heads):
    kern = functools.partial(_attn_kernel, tile=tile, heads=heads)
    width = heads * DA_V_DIM
    groups = DA_HEADS // heads
    return pl.pallas_call(
        kern,
        grid=(batch, groups, seq // tile),
        in_specs=[
            pl.BlockSpec(memory_space=pltpu.SMEM),
            pl.BlockSpec((4, DA_HEAD_DIM), lambda b, h, i: (0, 0)),
            pl.BlockSpec((None, tile, width), lambda b, h, i: (b, i, COL_DA_Q * groups + h)),
            pl.BlockSpec((None, seq, width), lambda b, h, i: (b, 0, COL_DA_K * groups + h)),
            pl.BlockSpec((None, width, seq), lambda b, h, i: (b, h, 0)),
            pl.BlockSpec((1, width), lambda b, h, i: (0, h)),
        ],
        out_specs=pl.BlockSpec((None, tile, width), lambda b, h, i: (b, i, h)),
        out_shape=jax.ShapeDtypeStruct((batch, seq, D_MODEL), BF16),
        scratch_shapes=[
            pltpu.VMEM((heads, 2 * DA_V_DIM, 2 * tile), BF16),
            pltpu.VMEM((tile // 2, DA_V_DIM), BF16),
            pltpu.VMEM((2 * heads, tile // 2, 2 * tile), F32),
            pltpu.VMEM((2 * heads, tile // 2, 2 * tile), BF16),
            pltpu.VMEM((heads, DA_V_DIM + ONES_ROWS, 2 * tile), F32),
            pltpu.VMEM((heads, 1, 2 * tile), F32),
        ],
        compiler_params=pltpu.CompilerParams(
            dimension_semantics=("parallel", "parallel", "arbitrary"),
            vmem_limit_bytes=VMEM_LIMIT),
    )(slopes, lam, proj3, proj3, vt, g)


def _mlstm_kernel(qk_ref, v_ref, og_ref, grow_ref, gcol_ref, cw_ref, cb_ref, brow_ref, bcol_ref,
                  ng_ref, out_ref, c_scr, n_scr, m_scr, ext_scr, *, chunk):
    ci = pl.program_id(1)
    pad = SUBLANES

    @pl.when(ci == 0)
    def _():
        c_scr[...] = jnp.zeros(c_scr.shape, F32)
        n_scr[...] = jnp.zeros(n_scr.shape, F32)
        m_scr[...] = jnp.zeros(m_scr.shape, F32)
        ext_scr[0:pad, :] = jnp.zeros((pad, ext_scr.shape[1]), F32)

    ext_scr[pad:pad + chunk, :] = qk_ref[...].astype(F32)
    conv = cb_ref[...] + cw_ref[CONV_WIDTH - 1:CONV_WIDTH, :] * ext_scr[pad:pad + chunk, :]
    for tap in range(1, CONV_WIDTH):
        conv = conv + (cw_ref[CONV_WIDTH - 1 - tap:CONV_WIDTH - tap, :]
                       * ext_scr[pad - tap:pad - tap + chunk, :])
    ext_scr[0:pad, :] = ext_scr[chunk:chunk + pad, :]
    qk = conv * _sigmoid(conv)

    g_rows = grow_ref[...] + brow_ref[...]
    g_cols = gcol_ref[...] + bcol_ref[...]
    r_idx = lax.broadcasted_iota(jnp.int32, (chunk, chunk), 0)
    c_idx = lax.broadcasted_iota(jnp.int32, (chunk, chunk), 1)
    causal = r_idx >= c_idx
    tril = jnp.where(causal, 1.0, 0.0).astype(F32)
    triu = jnp.where(r_idx <= c_idx, 1.0, 0.0).astype(F32)
    b_rows = jnp.dot(_log_sigmoid(g_rows), triu, preferred_element_type=F32,
                     precision=lax.Precision.HIGHEST)
    b_cols = jnp.dot(tril, _log_sigmoid(g_cols), preferred_element_type=F32,
                     precision=lax.Precision.HIGHEST)

    q_scale = ML_QK_DIM ** -0.5
    for hd in range(ML_HEADS):
        qf = qk[:, hd * ML_QK_DIM:(hd + 1) * ML_QK_DIM] * q_scale
        kf = qk[:, ML_QK_WIDTH + hd * ML_QK_DIM:ML_QK_WIDTH + (hd + 1) * ML_QK_DIM]
        qb = qf.astype(BF16)
        vb = v_ref[:, hd * ML_V_DIM:(hd + 1) * ML_V_DIM]
        fcol = ML_HEADS + hd
        bt = b_cols[:, fcol:fcol + 1]
        bs = b_rows[fcol:fcol + 1, :]
        i_row = g_rows[hd:hd + 1, :]
        i_col = g_cols[:, hd:hd + 1]
        m_prev = m_scr[hd]
        c_prev = c_scr[hd]
        n_prev = n_scr[hd]

        logd = jnp.where(causal, bt - bs + i_row, NEG_BIG)
        inter = bt + m_prev
        m_t = jnp.maximum(inter, jnp.max(logd, axis=1, keepdims=True))
        dmat = jnp.exp(logd - m_t)
        sc = lax.dot_general(qb, kf.astype(BF16), NT_DIMS, preferred_element_type=F32) * dmat
        w_inter = jnp.exp(inter - m_t)
        num = (w_inter * jnp.dot(qb, c_prev.astype(BF16), preferred_element_type=F32)
               + jnp.dot(sc.astype(BF16), vb, preferred_element_type=F32))
        den = (w_inter * jnp.sum(qf * n_prev, axis=1, keepdims=True)
               + jnp.sum(sc, axis=1, keepdims=True))
        hh = num / jnp.maximum(jnp.abs(den), jnp.exp(-m_t))

        g_last = bt[chunk - 1:chunk, :]
        log_w = g_last - bt + i_col
        m_new = jnp.maximum(g_last + m_prev, jnp.max(log_w, axis=0, keepdims=True))
        kw = kf * jnp.exp(log_w - m_new)
        decay = jnp.exp(g_last + m_prev - m_new)
        c_scr[hd] = decay * c_prev + lax.dot_general(
            kw.astype(BF16), vb, TN_DIMS, preferred_element_type=F32)
        n_scr[hd] = decay * n_prev + jnp.sum(kw, axis=0, keepdims=True)
        m_scr[hd] = m_new

        sl = slice(hd * ML_V_DIM, (hd + 1) * ML_V_DIM)
        hn = hh * lax.rsqrt(jnp.mean(hh * hh, axis=1, keepdims=True) + EPS) * ng_ref[:, sl]
        out_ref[:, sl] = (_sigmoid(og_ref[:, sl].astype(F32)) * hn).astype(out_ref.dtype)


def _mlstm(proj3, g_rows, g_cols3, conv_w, conv_b, b_row, b_col, norm_g, batch, seq, chunk):
    kern = functools.partial(_mlstm_kernel, chunk=chunk)
    return pl.pallas_call(
        kern,
        grid=(batch, seq // chunk),
        in_specs=[
            pl.BlockSpec((None, chunk, D_MODEL), lambda b, c: (b, c, COL_ML_QK)),
            pl.BlockSpec((None, chunk, D_MODEL), lambda b, c: (b, c, COL_ML_V)),
            pl.BlockSpec((None, chunk, D_MODEL), lambda b, c: (b, c, COL_ML_O)),
            pl.BlockSpec((None, 2 * ML_HEADS, chunk), lambda b, c: (b, 0, c)),
            pl.BlockSpec((None, chunk, LANES), lambda b, c: (b, c, 0)),
            pl.BlockSpec((CONV_WIDTH, D_MODEL), lambda b, c: (0, 0)),
            pl.BlockSpec((1, D_MODEL), lambda b, c: (0, 0)),
            pl.BlockSpec((2 * ML_HEADS, 1), lambda b, c: (0, 0)),
            pl.BlockSpec((1, LANES), lambda b, c: (0, 0)),
            pl.BlockSpec((1, D_MODEL), lambda b, c: (0, 0)),
        ],
        out_specs=pl.BlockSpec((None, chunk, D_MODEL), lambda b, c: (b, c, 0)),
        out_shape=jax.ShapeDtypeStruct((batch, seq, D_MODEL), BF16),
        scratch_shapes=[
            pltpu.VMEM((ML_HEADS, ML_QK_DIM, ML_V_DIM), F32),
            pltpu.VMEM((ML_HEADS, 1, ML_QK_DIM), F32),
            pltpu.VMEM((ML_HEADS, 1, 1), F32),
            pltpu.VMEM((chunk + 2 * SUBLANES, D_MODEL), F32),
        ],
        compiler_params=pltpu.CompilerParams(
            dimension_semantics=("parallel", "arbitrary"), vmem_limit_bytes=VMEM_LIMIT),
    )(proj3, proj3, proj3, g_rows, g_cols3, conv_w, conv_b, b_row, b_col, norm_g)


def _rms(x, g):
    return x * lax.rsqrt(jnp.mean(x * x, axis=-1, keepdims=True) + EPS) * g


def _tail_kernel(x_ref, a_ref, m_ref, ga_ref, gm_ref, bm_ref, wa_ref, wm_ref, wo_ref,
                 gmlp_ref, w1_ref, w2_ref, gfin_ref, o_ref):
    ya = jnp.dot(a_ref[...], wa_ref[...], preferred_element_type=F32)
    ym = jnp.dot(m_ref[...], wm_ref[...], preferred_element_type=F32)
    gate_a = _sigmoid(ga_ref[...].astype(F32) + bm_ref[:, :D_MODEL])
    gate_m = _sigmoid(gm_ref[...].astype(F32) + bm_ref[:, D_MODEL:])
    merged = (gate_a * ya + gate_m * ym).astype(BF16)
    x1 = x_ref[...] + jnp.dot(merged, wo_ref[...], preferred_element_type=F32)
    hm = _rms(x1, gmlp_ref[...]).astype(BF16)
    acc = x1
    for c in range(D_FF // D_MODEL):
        cols = slice(c * D_MODEL, (c + 1) * D_MODEL)
        u = jnp.maximum(jnp.dot(hm, w1_ref[:, cols], preferred_element_type=F32), 0.0)
        acc = acc + jnp.dot((u * u).astype(BF16), w2_ref[cols, :], preferred_element_type=F32)
    o_ref[...] = _rms(acc, gfin_ref[...])


def _tail(x2, a2, m2, proj, b_merge, wa, wm, wo, g_mlp, w1, w2, g_fin, tm):
    tokens = x2.shape[0]
    const = lambda i: (0, 0)

    def resident(shape):
        return pl.BlockSpec(shape, const, pipeline_mode=pl.Buffered(1))

    return pl.pallas_call(
        _tail_kernel,
        grid=(tokens // tm,),
        in_specs=[
            pl.BlockSpec((tm, D_MODEL), lambda i: (i, 0)),
            pl.BlockSpec((tm, D_MODEL), lambda i: (i, 0)),
            pl.BlockSpec((tm, D_MODEL), lambda i: (i, 0)),
            pl.BlockSpec((tm, D_MODEL), lambda i: (i, COL_MG_A)),
            pl.BlockSpec((tm, D_MODEL), lambda i: (i, COL_MG_M)),
            resident((1, 2 * D_MODEL)),
            resident((D_MODEL, D_MODEL)),
            resident((D_MODEL, D_MODEL)),
            resident((D_MODEL, D_MODEL)),
            resident((1, D_MODEL)),
            resident((D_MODEL, D_FF)),
            resident((D_FF, D_MODEL)),
            resident((1, D_MODEL)),
        ],
        out_specs=pl.BlockSpec((tm, D_MODEL), lambda i: (i, 0)),
        out_shape=jax.ShapeDtypeStruct((tokens, D_MODEL), F32),
        compiler_params=pltpu.CompilerParams(
            dimension_semantics=("parallel",), vmem_limit_bytes=VMEM_LIMIT),
    )(x2, a2, m2, proj, proj, b_merge, wa, wm, wo, g_mlp, w1, w2, g_fin)


def kernel(x, norm_mix_g, w_in, b_gates, conv_w, conv_b, lam, da_norm_g, ml_norm_g, b_merge,
           w_branch_a, w_branch_m, w_out, norm_mlp_g, w_ff1, w_ff2, norm_final_g):
    batch, seq, _ = x.shape
    tokens = batch * seq
    x2 = x.reshape(tokens, D_MODEL)

    assert w_in.shape == (1, D_MODEL, D_IN)
    w_bf = w_in[0].astype(BF16)
    w_tail = w_bf[:, OFF_ML_O:]

    tm_proj = min(512, seq)
    proj, vt, ifg, ifg_t = _in_proj(x2, norm_mix_g, w_bf, w_tail, batch, seq, tm_proj)
    proj3 = proj.reshape(batch, seq, N_PROJ_BLOCKS * D_MODEL)

    slopes = 2.0 ** (-8.0 * jnp.arange(1, DA_HEADS + 1, dtype=F32) / DA_HEADS)
    attn_tile = min(512, seq)
    a_out = _attention(slopes, lam[0], proj3, vt, da_norm_g, batch, seq, attn_tile, 2)

    ifg3 = ifg.reshape(batch, seq, LANES)
    b_row = b_gates[0].reshape(2 * ML_HEADS, 1)
    b_col = jnp.pad(b_gates, ((0, 0), (0, LANES - 2 * ML_HEADS)))
    chunk = min(256, seq)
    m_out = _mlstm(proj3, ifg_t, ifg3, conv_w[0], conv_b, b_row, b_col, ml_norm_g,
                   batch, seq, chunk)

    out = _tail(x2, a_out.reshape(tokens, D_MODEL), m_out.reshape(tokens, D_MODEL), proj,
                b_merge, w_branch_a[0].astype(BF16), w_branch_m[0].astype(BF16),
                w_out[0].astype(BF16), norm_mlp_g, w_ff1[0].astype(BF16), w_ff2[0].astype(BF16),
                norm_final_g.reshape(1, D_MODEL), min(512, tokens))
    return out.reshape(batch, seq, D_MODEL)
```

```python
import functools
import math

import jax
import jax.numpy as jnp
from jax import lax
from jax.experimental import pallas as pl
from jax.experimental.pallas import tpu as pltpu

F32 = jnp.float32
BF16 = jnp.bfloat16

D_MODEL = 1024
DA_HEADS = 8
DA_HEAD_DIM = 64
DA_V_DIM = 2 * DA_HEAD_DIM
ML_HEADS = 4
ML_V_DIM = D_MODEL // ML_HEADS
ML_QK_DIM = ML_V_DIM // 2
ML_QK_WIDTH = ML_HEADS * ML_QK_DIM
CONV_WIDTH = 4
D_FF = 4 * D_MODEL
EPS = 1e-6
LAM_INIT = 0.8 - 0.6 * math.exp(-0.3 * 0)
NEG_BIG = -1e30
LOG2E = math.log2(math.e)
ONES_ROWS = 16

LANES = 128
SUBLANES = 8
VMEM_LIMIT = 56 * 1024 * 1024

COL_DA_Q, COL_DA_K, COL_ML_QK, COL_ML_V, COL_ML_O, COL_MG_A, COL_MG_M = range(7)
N_PROJ_BLOCKS = 7

OFF_DA_Q = 0
OFF_DA_K = OFF_DA_Q + D_MODEL
OFF_DA_V = OFF_DA_K + D_MODEL
OFF_ML_Q = OFF_DA_V + D_MODEL
OFF_ML_V = OFF_ML_Q + 2 * ML_QK_WIDTH
OFF_ML_IF = OFF_ML_V + D_MODEL
OFF_ML_O = OFF_ML_IF + 2 * ML_HEADS
D_IN = OFF_ML_O + 3 * D_MODEL

NT_DIMS = (((1,), (1,)), ((), ()))
TN_DIMS = (((0,), (0,)), ((), ()))
TT_DIMS = (((0,), (1,)), ((), ()))


def _sigmoid(x):
    return 1.0 / (1.0 + jnp.exp(-x))


def _log_sigmoid(x):
    return jnp.minimum(x, 0.0) - jnp.log(1.0 + jnp.exp(-jnp.abs(x)))


def _in_proj_kernel(x_ref, g_ref, w_ref, wtail_ref, proj_ref, vt_ref, if_ref, ift_ref):
    x = x_ref[...]
    hb = (x * lax.rsqrt(jnp.mean(x * x, axis=-1, keepdims=True) + EPS) * g_ref[...]).astype(BF16)

    def project(w_cols):
        return jnp.dot(hb, w_cols, preferred_element_type=F32)

    def project_t(w_cols):
        return lax.dot_general(w_cols, hb, TT_DIMS, preferred_element_type=F32)

    proj_ref[:, 0:D_MODEL] = (project(w_ref[:, OFF_DA_Q:OFF_DA_K]) * DA_HEAD_DIM ** -0.5).astype(BF16)
    proj_ref[:, D_MODEL:2 * D_MODEL] = project(w_ref[:, OFF_DA_K:OFF_DA_V]).astype(BF16)
    vt_ref[...] = project_t(w_ref[:, OFF_DA_V:OFF_ML_Q]).astype(BF16)
    for blk in range(2):
        src = OFF_ML_Q + blk * D_MODEL
        dst = (COL_ML_QK + blk) * D_MODEL
        proj_ref[:, dst:dst + D_MODEL] = project(w_ref[:, src:src + D_MODEL]).astype(BF16)
    w_if = w_ref[:, OFF_ML_IF:OFF_ML_IF + LANES]
    if_ref[...] = project(w_if)
    ift_ref[...] = project_t(w_if)
    for blk in range(3):
        dst = (COL_ML_O + blk) * D_MODEL
        proj_ref[:, dst:dst + D_MODEL] = project(
            wtail_ref[:, blk * D_MODEL:(blk + 1) * D_MODEL]).astype(BF16)


def _in_proj(x2, g, w_bf, w_tail, batch, seq, tm):
    tokens = batch * seq
    nsb = seq // tm

    def resident(arr):
        return pl.BlockSpec(arr.shape, lambda i: (0, 0), pipeline_mode=pl.Buffered(1))

    return pl.pallas_call(
        _in_proj_kernel,
        grid=(tokens // tm,),
        in_specs=[
            pl.BlockSpec((tm, D_MODEL), lambda i: (i, 0)),
            resident(g),
            resident(w_bf),
            resident(w_tail),
        ],
        out_specs=[
            pl.BlockSpec((tm, N_PROJ_BLOCKS * D_MODEL), lambda i: (i, 0)),
            pl.BlockSpec((None, D_MODEL, tm), lambda i: (i // nsb, 0, i % nsb)),
            pl.BlockSpec((tm, LANES), lambda i: (i, 0)),
            pl.BlockSpec((None, LANES, tm), lambda i: (i // nsb, 0, i % nsb)),
        ],
        out_shape=[
            jax.ShapeDtypeStruct((tokens, N_PROJ_BLOCKS * D_MODEL), BF16),
            jax.ShapeDtypeStruct((batch, D_MODEL, seq), BF16),
            jax.ShapeDtypeStruct((tokens, LANES), F32),
            jax.ShapeDtypeStruct((batch, LANES, seq), F32),
        ],
        compiler_params=pltpu.CompilerParams(
            dimension_semantics=("parallel",), vmem_limit_bytes=VMEM_LIMIT),
    )(x2, g, w_bf, w_tail)


def _attn_kernel(slopes_ref, lam_ref, q_ref, k_ref, vt_ref, g_ref, o_ref,
                 qq_scr, pos_scr, s_scr, p_scr, acc_scr, m_scr, *, tile, heads):
    group = pl.program_id(1)
    qi = pl.program_id(2)
    kt_size = tile // 2
    slope2 = [slopes_ref[group * heads + hh] * LOG2E for hh in range(heads)]

    def head_cols(hh):
        return slice(hh * DA_V_DIM, (hh + 1) * DA_V_DIM)

    key_off = lax.broadcasted_iota(jnp.int32, pos_scr.shape, 0).astype(F32)
    lane = lax.broadcasted_iota(jnp.int32, pos_scr.shape, 1)
    pos_scr[...] = jnp.where(lane < 2, key_off, 0.0).astype(BF16)
    for hh in range(heads):
        qt = q_ref[:, head_cols(hh)].astype(F32).T * LOG2E
        d_idx = lax.broadcasted_iota(jnp.int32, qt.shape, 0)
        q0 = jnp.where(d_idx < DA_HEAD_DIM, qt, 0.0)
        q1 = jnp.where(d_idx >= DA_HEAD_DIM, qt, 0.0)
        qq_scr[hh, 0:DA_V_DIM, :] = jnp.concatenate([q0, q1], axis=1).astype(BF16)
        slope_vec = jnp.full((DA_V_DIM, 2 * tile), slope2[hh], F32)
        slope_hi = slope_vec.astype(BF16).astype(F32)
        feat = lax.broadcasted_iota(jnp.int32, slope_vec.shape, 0)
        qq_scr[hh, DA_V_DIM:, :] = jnp.where(
            feat == 0, slope_hi, jnp.where(feat == 1, slope_vec - slope_hi, 0.0)).astype(BF16)
        m_scr[hh] = jnp.full(m_scr.shape[1:], NEG_BIG, F32)
        acc_scr[hh] = jnp.zeros(acc_scr.shape[1:], F32)
        p_scr[2 * hh + 1] = jnp.zeros(p_scr.shape[1:], BF16)

    def lanes(blk):
        return slice(blk * kt_size, (blk + 1) * kt_size)

    upper_blocks = (1, 3)

    def scores(hh, t, slot, blocks=None):
        start = pl.multiple_of(t * kt_size, kt_size)
        kt = jnp.concatenate([k_ref[pl.ds(start, kt_size), head_cols(hh)], pos_scr[...]], axis=1)
        if blocks is None:
            s_scr[2 * hh + slot] = jnp.dot(kt, qq_scr[hh], preferred_element_type=F32)
        else:
            for blk in blocks:
                s_scr[2 * hh + slot, :, lanes(blk)] = jnp.dot(
                    kt, qq_scr[hh, :, lanes(blk)], preferred_element_type=F32)

    ones_rows = jnp.ones((ONES_ROWS, kt_size), BF16)

    def values_t(hh, t):
        start = pl.multiple_of(jnp.maximum(t, 0) * kt_size, kt_size)
        return jnp.concatenate([vt_ref[head_cols(hh), pl.ds(start, kt_size)], ones_rows], axis=0)

    def weighted_values(hh, t, slot):
        return jnp.dot(values_t(hh, t), p_scr[2 * hh + slot], preferred_element_type=F32)

    def step(hh, t, slot, mask=None, prefetch_blocks=None):
        scores(hh, t + 1, 1 - slot, prefetch_blocks)
        s = s_scr[2 * hh + slot]
        if mask is not None:
            s = jnp.where(mask, s, NEG_BIG)
        c = slope2[hh] * (t * kt_size - qi * tile).astype(F32)
        m_old = m_scr[hh]
        m_new = jnp.maximum(m_old, jnp.max(s, axis=0, keepdims=True) + c)
        p = jnp.exp2(s - (m_new - c))
        alpha = jnp.exp2(m_old - m_new)
        p_scr[2 * hh + slot] = p.astype(BF16)
        m_scr[hh] = m_new
        acc_scr[hh] = alpha * (acc_scr[hh] + weighted_values(hh, t - 1, 1 - slot))

    for hh in range(heads):
        scores(hh, 0, 0)

    def body(i, carry):
        for slot in range(2):
            for hh in range(heads):
                step(hh, 2 * i + slot, slot)
        return carry

    lax.fori_loop(0, qi, body, 0)

    kk = lax.broadcasted_iota(jnp.int32, (kt_size, 2 * tile), 0)
    qpos = lax.broadcasted_iota(jnp.int32, (kt_size, 2 * tile), 1)
    qpos = jnp.where(qpos >= tile, qpos - tile, qpos)
    for hh in range(heads):
        step(hh, 2 * qi, 0, mask=kk <= qpos, prefetch_blocks=upper_blocks)

    tri = (lax.broadcasted_iota(jnp.int32, (kt_size, kt_size), 0)
           <= lax.broadcasted_iota(jnp.int32, (kt_size, kt_size), 1))

    def last_tile(hh):
        t = 2 * qi + 1
        c = slope2[hh] * kt_size
        acc = acc_scr[hh] + weighted_values(hh, t - 1, 0)
        vt = values_t(hh, t)
        parts = []
        for blk in range(2 * tile // kt_size):
            part = acc[:, lanes(blk)]
            if blk in upper_blocks:
                s = jnp.where(tri, s_scr[2 * hh + 1, :, lanes(blk)], NEG_BIG)
                m_old = m_scr[hh, :, lanes(blk)]
                m_new = jnp.maximum(m_old, jnp.max(s, axis=0, keepdims=True) + c)
                p = jnp.exp2(s - (m_new - c)).astype(BF16)
                part = (jnp.exp2(m_old - m_new) * part
                        + jnp.dot(vt, p, preferred_element_type=F32))
            parts.append(part)
        return jnp.concatenate(parts, axis=1)

    lam = lam_ref[...]
    lam_full = (jnp.exp(jnp.sum(lam[0:1] * lam[1:2], axis=1, keepdims=True))
                - jnp.exp(jnp.sum(lam[2:3] * lam[3:4], axis=1, keepdims=True)) + LAM_INIT)
    for hh in range(heads):
        acc = last_tile(hh)
        l = acc[DA_V_DIM:DA_V_DIM + 1, :]
        acc = acc[:DA_V_DIM, :]
        o = acc[:, :tile] / l[:, :tile] - lam_full * (acc[:, tile:] / l[:, tile:])
        o = o * lax.rsqrt(jnp.mean(o * o, axis=0, keepdims=True) + EPS) * (1.0 - LAM_INIT)
        o_ref[:, head_cols(hh)] = (o.T * g_ref[:, head_cols(hh)]).astype(o_ref.dtype)


def _attention(slopes, lam, proj3, vt, g, batch, seq, tile, heads):
    kern = functools.partial(_attn_kernel, tile=tile, heads=heads)
    width = heads * DA_V_DIM
    groups = DA_HEADS // heads
    return pl.pallas_call(
        kern,
        grid=(batch, groups, seq // tile),
        in_specs=[
            pl.BlockSpec(memory_space=pltpu.SMEM),
            pl.BlockSpec((4, DA_HEAD_DIM), lambda b, h, i: (0, 0)),
            pl.BlockSpec((None, tile, width), lambda b, h, i: (b, i, COL_DA_Q * groups + h)),
            pl.BlockSpec((None, seq, width), lambda b, h, i: (b, 0, COL_DA_K * groups + h)),
            pl.BlockSpec((None, width, seq), lambda b, h, i: (b, h, 0)),
            pl.BlockSpec((1, width), lambda b, h, i: (0, h)),
        ],
        out_specs=pl.BlockSpec((None, tile, width), lambda b, h, i: (b, i, h)),
        out_shape=jax.ShapeDtypeStruct((batch, seq, D_MODEL), BF16),
        scratch_shapes=[
            pltpu.VMEM((heads, 2 * DA_V_DIM, 2 * tile), BF16),
            pltpu.VMEM((tile // 2, DA_V_DIM), BF16),
            pltpu.VMEM((2 * heads, tile // 2, 2 * tile), F32),
            pltpu.VMEM((2 * heads, tile // 2, 2 * tile), BF16),
            pltpu.VMEM((heads, DA_V_DIM + ONES_ROWS, 2 * tile), F32),
            pltpu.VMEM((heads, 1, 2 * tile), F32),
        ],
        compiler_params=pltpu.CompilerParams(
            dimension_semantics=("parallel", "parallel", "arbitrary"),
            vmem_limit_bytes=VMEM_LIMIT),
    )(slopes, lam, proj3, proj3, vt, g)


def _mlstm_kernel(qk_ref, v_ref, og_ref, grow_ref, gcol_ref, cw_ref, cb_ref, brow_ref, bcol_ref,
                  ng_ref, out_ref, c_scr, n_scr, m_scr, ext_scr, *, chunk):
    ci = pl.program_id(1)
    pad = SUBLANES

    @pl.when(ci == 0)
    def _():
        c_scr[...] = jnp.zeros(c_scr.shape, F32)
        n_scr[...] = jnp.zeros(n_scr.shape, F32)
        m_scr[...] = jnp.zeros(m_scr.shape, F32)
        ext_scr[0:pad, :] = jnp.zeros((pad, ext_scr.shape[1]), F32)

    ext_scr[pad:pad + chunk, :] = qk_ref[...].astype(F32)
    conv = cb_ref[...] + cw_ref[CONV_WIDTH - 1:CONV_WIDTH, :] * ext_scr[pad:pad + chunk, :]
    for tap in range(1, CONV_WIDTH):
        conv = conv + (cw_ref[CONV_WIDTH - 1 - tap:CONV_WIDTH - tap, :]
                       * ext_scr[pad - tap:pad - tap + chunk, :])
    ext_scr[0:pad, :] = ext_scr[chunk:chunk + pad, :]
    qk = conv * _sigmoid(conv)

    g_rows = grow_ref[...] + brow_ref[...]
    g_cols = gcol_ref[...] + bcol_ref[...]
    r_idx = lax.broadcasted_iota(jnp.int32, (chunk, chunk), 0)
    c_idx = lax.broadcasted_iota(jnp.int32, (chunk, chunk), 1)
    causal = r_idx >= c_idx
    tril = jnp.where(causal, 1.0, 0.0).astype(F32)
    triu = jnp.where(r_idx <= c_idx, 1.0, 0.0).astype(F32)
    b_rows = jnp.dot(_log_sigmoid(g_rows), triu, preferred_element_type=F32,
                     precision=lax.Precision.HIGHEST)
    b_cols = jnp.dot(tril, _log_sigmoid(g_cols), preferred_element_type=F32,
                     precision=lax.Precision.HIGHEST)

    q_scale = ML_QK_DIM ** -0.5
    for hd in range(ML_HEADS):
        qf = qk[:, hd * ML_QK_DIM:(hd + 1) * ML_QK_DIM] * q_scale
        kf = qk[:, ML_QK_WIDTH + hd * ML_QK_DIM:ML_QK_WIDTH + (hd + 1) * ML_QK_DIM]
        qb = qf.astype(BF16)
        vb = v_ref[:, hd * ML_V_DIM:(hd + 1) * ML_V_DIM]
        fcol = ML_HEADS + hd
        bt = b_cols[:, fcol:fcol + 1]
        bs = b_rows[fcol:fcol + 1, :]
        i_row = g_rows[hd:hd + 1, :]
        i_col = g_cols[:, hd:hd + 1]
        m_prev = m_scr[hd]
        c_prev = c_scr[hd]
        n_prev = n_scr[hd]

        logd = jnp.where(causal, bt - bs + i_row, NEG_BIG)
        inter = bt + m_prev
        m_t = jnp.maximum(inter, jnp.max(logd, axis=1, keepdims=True))
        dmat = jnp.exp(logd - m_t)
        sc = lax.dot_general(qb, kf.astype(BF16), NT_DIMS, preferred_element_type=F32) * dmat
        w_inter = jnp.exp(inter - m_t)
        num = (w_inter * jnp.dot(qb, c_prev.astype(BF16), preferred_element_type=F32)
               + jnp.dot(sc.astype(BF16), vb, preferred_element_type=F32))
        den = (w_inter * jnp.sum(qf * n_prev, axis=1, keepdims=True)
               + jnp.sum(sc, axis=1, keepdims=True))
        hh = num / jnp.maximum(jnp.abs(den), jnp.exp(-m_t))

        g_last = bt[chunk - 1:chunk, :]
        log_w = g_last - bt + i_col
        m_new = jnp.maximum(g_last + m_prev, jnp.max(log_w, axis=0, keepdims=True))
        kw = kf * jnp.exp(log_w - m_new)
        decay = jnp.exp(g_last + m_prev - m_new)
        c_scr[hd] = decay * c_prev + lax.dot_general(
            kw.astype(BF16), vb, TN_DIMS, preferred_element_type=F32)
        n_scr[hd] = decay * n_prev + jnp.sum(kw, axis=0, keepdims=True)
        m_scr[hd] = m_new

        sl = slice(hd * ML_V_DIM, (hd + 1) * ML_V_DIM)
        hn = hh * lax.rsqrt(jnp.mean(hh * hh, axis=1, keepdims=True) + EPS) * ng_ref[:, sl]
        out_ref[:, sl] = (_sigmoid(og_ref[:, sl].astype(F32)) * hn).astype(out_ref.dtype)


def _mlstm(proj3, g_rows, g_cols3, conv_w, conv_b, b_row, b_col, norm_g, batch, seq, chunk):
    kern = functools.partial(_mlstm_kernel, chunk=chunk)
    return pl.pallas_call(
        kern,
        grid=(batch, seq // chunk),
        in_specs=[
            pl.BlockSpec((None, chunk, D_MODEL), lambda b, c: (b, c, COL_ML_QK)),
            pl.BlockSpec((None, chunk, D_MODEL), lambda b, c: (b, c, COL_ML_V)),
            pl.BlockSpec((None, chunk, D_MODEL), lambda b, c: (b, c, COL_ML_O)),
            pl.BlockSpec((None, 2 * ML_HEADS, chunk), lambda b, c: (b, 0, c)),
            pl.BlockSpec((None, chunk, LANES), lambda b, c: (b, c, 0)),
            pl.BlockSpec((CONV_WIDTH, D_MODEL), lambda b, c: (0, 0)),
            pl.BlockSpec((1, D_MODEL), lambda b, c: (0, 0)),
            pl.BlockSpec((2 * ML_HEADS, 1), lambda b, c: (0, 0)),
            pl.BlockSpec((1, LANES), lambda b, c: (0, 0)),
            pl.BlockSpec((1, D_MODEL), lambda b, c: (0, 0)),
        ],
        out_specs=pl.BlockSpec((None, chunk, D_MODEL), lambda b, c: (b, c, 0)),
        out_shape=jax.ShapeDtypeStruct((batch, seq, D_MODEL), BF16),
        scratch_shapes=[
            pltpu.VMEM((ML_HEADS, ML_QK_DIM, ML_V_DIM), F32),
            pltpu.VMEM((ML_HEADS, 1, ML_QK_DIM), F32),
            pltpu.VMEM((ML_HEADS, 1, 1), F32),
            pltpu.VMEM((chunk + 2 * SUBLANES, D_MODEL), F32),
        ],
        compiler_params=pltpu.CompilerParams(
            dimension_semantics=("parallel", "arbitrary"), vmem_limit_bytes=VMEM_LIMIT),
    )(proj3, proj3, proj3, g_rows, g_cols3, conv_w, conv_b, b_row, b_col, norm_g)


def _rms(x, g):
    return x * lax.rsqrt(jnp.mean(x * x, axis=-1, keepdims=True) + EPS) * g


def _tail_kernel(x_ref, a_ref, m_ref, ga_ref, gm_ref, bm_ref, wa_ref, wm_ref, wo_ref,
                 gmlp_ref, w1_ref, w2_ref, gfin_ref, o_ref):
    ya = jnp.dot(a_ref[...], wa_ref[...], preferred_element_type=F32)
    ym = jnp.dot(m_ref[...], wm_ref[...], preferred_element_type=F32)
    gate_a = _sigmoid(ga_ref[...].astype(F32) + bm_ref[:, :D_MODEL])
    gate_m = _sigmoid(gm_ref[...].astype(F32) + bm_ref[:, D_MODEL:])
    merged = (gate_a * ya + gate_m * ym).astype(BF16)
    x1 = x_ref[...] + jnp.dot(merged, wo_ref[...], preferred_element_type=F32)
    hm = _rms(x1, gmlp_ref[...]).astype(BF16)
    acc = x1
    for c in range(D_FF // D_MODEL):
        cols = slice(c * D_MODEL, (c + 1) * D_MODEL)
        u = jnp.maximum(jnp.dot(hm, w1_ref[:, cols], preferred_element_type=F32), 0.0)
        acc = acc + jnp.dot((u * u).astype(BF16), w2_ref[cols, :], preferred_element_type=F32)
    o_ref[...] = _rms(acc, gfin_ref[...])


def _tail(x2, a2, m2, proj, b_merge, wa, wm, wo, g_mlp, w1, w2, g_fin, tm):
    tokens = x2.shape[0]
    const = lambda i: (0, 0)

    def resident(shape):
        return pl.BlockSpec(shape, const, pipeline_mode=pl.Buffered(1))

    return pl.pallas_call(
        _tail_kernel,
        grid=(tokens // tm,),
        in_specs=[
            pl.BlockSpec((tm, D_MODEL), lambda i: (i, 0)),
            pl.BlockSpec((tm, D_MODEL), lambda i: (i, 0)),
            pl.BlockSpec((tm, D_MODEL), lambda i: (i, 0)),
            pl.BlockSpec((tm, D_MODEL), lambda i: (i, COL_MG_A)),
            pl.BlockSpec((tm, D_MODEL), lambda i: (i, COL_MG_M)),
            resident((1, 2 * D_MODEL)),
            resident((D_MODEL, D_MODEL)),
            resident((D_MODEL, D_MODEL)),
            resident((D_MODEL, D_MODEL)),
            resident((1, D_MODEL)),
            resident((D_MODEL, D_FF)),
            resident((D_FF, D_MODEL)),
            resident((1, D_MODEL)),
        ],
        out_specs=pl.BlockSpec((tm, D_MODEL), lambda i: (i, 0)),
        out_shape=jax.ShapeDtypeStruct((tokens, D_MODEL), F32),
        compiler_params=pltpu.CompilerParams(
            dimension_semantics=("parallel",), vmem_limit_bytes=VMEM_LIMIT),
    )(x2, a2, m2, proj, proj, b_merge, wa, wm, wo, g_mlp, w1, w2, g_fin)


def kernel(x, norm_mix_g, w_in, b_gates, conv_w, conv_b, lam, da_norm_g, ml_norm_g, b_merge,
           w_branch_a, w_branch_m, w_out, norm_mlp_g, w_ff1, w_ff2, norm_final_g):
    batch, seq, _ = x.shape
    tokens = batch * seq
    x2 = x.reshape(tokens, D_MODEL)

    assert w_in.shape == (1, D_MODEL, D_IN)
    w_bf = w_in[0].astype(BF16)
    w_tail = w_bf[:, OFF_ML_O:]

    tm_proj = min(512, seq)
    proj, vt, ifg, ifg_t = _in_proj(x2, norm_mix_g, w_bf, w_tail, batch, seq, tm_proj)
    proj3 = proj.reshape(batch, seq, N_PROJ_BLOCKS * D_MODEL)

    slopes = 2.0 ** (-8.0 * jnp.arange(1, DA_HEADS + 1, dtype=F32) / DA_HEADS)
    attn_tile = min(512, seq)
    a_out = _attention(slopes, lam[0], proj3, vt, da_norm_g, batch, seq, attn_tile, 4)

    ifg3 = ifg.reshape(batch, seq, LANES)
    b_row = b_gates[0].reshape(2 * ML_HEADS, 1)
    b_col = jnp.pad(b_gates, ((0, 0), (0, LANES - 2 * ML_HEADS)))
    chunk = min(256, seq)
    m_out = _mlstm(proj3, ifg_t, ifg3, conv_w[0], conv_b, b_row, b_col, ml_norm_g,
                   batch, seq, chunk)

    out = _tail(x2, a_out.reshape(tokens, D_MODEL), m_out.reshape(tokens, D_MODEL), proj,
                b_merge, w_branch_a[0].astype(BF16), w_branch_m[0].astype(BF16),
                w_out[0].astype(BF16), norm_mlp_g, w_ff1[0].astype(BF16), w_ff2[0].astype(BF16),
                norm_final_g.reshape(1, D_MODEL), min(512, tokens))
    return out.reshape(batch, seq, D_MODEL)
```

```python
import functools
import math

import jax
import jax.numpy as jnp
from jax import lax
from jax.experimental import pallas as pl
from jax.experimental.pallas import tpu as pltpu

F32 = jnp.float32
BF16 = jnp.bfloat16

D_MODEL = 1024
DA_HEADS = 8
DA_HEAD_DIM = 64
DA_V_DIM = 2 * DA_HEAD_DIM
ML_HEADS = 4
ML_V_DIM = D_MODEL // ML_HEADS
ML_QK_DIM = ML_V_DIM // 2
ML_QK_WIDTH = ML_HEADS * ML_QK_DIM
CONV_WIDTH = 4
D_FF = 4 * D_MODEL
EPS = 1e-6
LAM_INIT = 0.8 - 0.6 * math.exp(-0.3 * 0)
NEG_BIG = -1e30
LOG2E = math.log2(math.e)
ONES_ROWS = 16
FF_BLOCK = 512

LANES = 128
SUBLANES = 8
VMEM_LIMIT = 60 * 1024 * 1024

COL_DA_Q, COL_DA_K, COL_ML_QK, COL_ML_V, COL_ML_O, COL_MG_A, COL_MG_M = range(7)
N_PROJ_BLOCKS = 7

OFF_DA_Q = 0
OFF_DA_K = OFF_DA_Q + D_MODEL
OFF_DA_V = OFF_DA_K + D_MODEL
OFF_ML_Q = OFF_DA_V + D_MODEL
OFF_ML_V = OFF_ML_Q + 2 * ML_QK_WIDTH
OFF_ML_IF = OFF_ML_V + D_MODEL
OFF_ML_O = OFF_ML_IF + 2 * ML_HEADS
D_IN = OFF_ML_O + 3 * D_MODEL

NT_DIMS = (((1,), (1,)), ((), ()))
TN_DIMS = (((0,), (0,)), ((), ()))
TT_DIMS = (((0,), (1,)), ((), ()))


def _sigmoid(x):
    return 1.0 / (1.0 + jnp.exp(-x))


def _log_sigmoid(x):
    return jnp.minimum(x, 0.0) - jnp.log(1.0 + jnp.exp(-jnp.abs(x)))


def _in_proj_kernel(x_ref, g_ref, w_ref, wtail_ref, proj_ref, vt_ref, if_ref, ift_ref):
    x = x_ref[...]
    hb = (x * lax.rsqrt(jnp.mean(x * x, axis=-1, keepdims=True) + EPS) * g_ref[...]).astype(BF16)

    def project(w_cols):
        return jnp.dot(hb, w_cols, preferred_element_type=F32)

    def project_t(w_cols):
        return lax.dot_general(w_cols, hb, TT_DIMS, preferred_element_type=F32)

    proj_ref[:, 0:D_MODEL] = (project(w_ref[:, OFF_DA_Q:OFF_DA_K]) * DA_HEAD_DIM ** -0.5).astype(BF16)
    proj_ref[:, D_MODEL:2 * D_MODEL] = project(w_ref[:, OFF_DA_K:OFF_DA_V]).astype(BF16)
    vt_ref[...] = project_t(w_ref[:, OFF_DA_V:OFF_ML_Q]).astype(BF16)
    for blk in range(2):
        src = OFF_ML_Q + blk * D_MODEL
        dst = (COL_ML_QK + blk) * D_MODEL
        proj_ref[:, dst:dst + D_MODEL] = project(w_ref[:, src:src + D_MODEL]).astype(BF16)
    w_if = w_ref[:, OFF_ML_IF:OFF_ML_IF + LANES]
    if_ref[...] = project(w_if)
    ift_ref[...] = project_t(w_if)
    for blk in range(3):
        dst = (COL_ML_O + blk) * D_MODEL
        proj_ref[:, dst:dst + D_MODEL] = project(
            wtail_ref[:, blk * D_MODEL:(blk + 1) * D_MODEL]).astype(BF16)


def _in_proj(x2, g, w_bf, w_tail, batch, seq, tm):
    tokens = batch * seq
    nsb = seq // tm

    def resident(arr):
        return pl.BlockSpec(arr.shape, lambda i: (0, 0), pipeline_mode=pl.Buffered(1))

    return pl.pallas_call(
        _in_proj_kernel,
        grid=(tokens // tm,),
        in_specs=[
            pl.BlockSpec((tm, D_MODEL), lambda i: (i, 0)),
            resident(g),
            resident(w_bf),
            resident(w_tail),
        ],
        out_specs=[
            pl.BlockSpec((tm, N_PROJ_BLOCKS * D_MODEL), lambda i: (i, 0)),
            pl.BlockSpec((None, D_MODEL, tm), lambda i: (i // nsb, 0, i % nsb)),
            pl.BlockSpec((tm, LANES), lambda i: (i, 0)),
            pl.BlockSpec((None, LANES, tm), lambda i: (i // nsb, 0, i % nsb)),
        ],
        out_shape=[
            jax.ShapeDtypeStruct((tokens, N_PROJ_BLOCKS * D_MODEL), BF16),
            jax.ShapeDtypeStruct((batch, D_MODEL, seq), BF16),
            jax.ShapeDtypeStruct((tokens, LANES), F32),
            jax.ShapeDtypeStruct((batch, LANES, seq), F32),
        ],
        compiler_params=pltpu.CompilerParams(
            dimension_semantics=("parallel",), vmem_limit_bytes=VMEM_LIMIT),
    )(x2, g, w_bf, w_tail)


def _attn_kernel(slopes_ref, lam_ref, q_ref, k_ref, vt_ref, g_ref, o_ref,
                 qq_scr, pos_scr, s_scr, p_scr, acc_scr, m_scr, *, tile, heads):
    group = pl.program_id(1)
    qi = pl.program_id(2)
    kt_size = tile // 2
    slope2 = [slopes_ref[group * heads + hh] * LOG2E for hh in range(heads)]

    def head_cols(hh):
        return slice(hh * DA_V_DIM, (hh + 1) * DA_V_DIM)

    key_off = lax.broadcasted_iota(jnp.int32, pos_scr.shape, 0).astype(F32)
    lane = lax.broadcasted_iota(jnp.int32, pos_scr.shape, 1)
    pos_scr[...] = jnp.where(lane < 2, key_off, 0.0).astype(BF16)
    for hh in range(heads):
        qt = q_ref[:, head_cols(hh)].astype(F32).T * LOG2E
        d_idx = lax.broadcasted_iota(jnp.int32, qt.shape, 0)
        q0 = jnp.where(d_idx < DA_HEAD_DIM, qt, 0.0)
        q1 = jnp.where(d_idx >= DA_HEAD_DIM, qt, 0.0)
        qq_scr[hh, 0:DA_V_DIM, :] = jnp.concatenate([q0, q1], axis=1).astype(BF16)
        slope_vec = jnp.full((DA_V_DIM, 2 * tile), slope2[hh], F32)
        slope_hi = slope_vec.astype(BF16).astype(F32)
        feat = lax.broadcasted_iota(jnp.int32, slope_vec.shape, 0)
        qq_scr[hh, DA_V_DIM:, :] = jnp.where(
            feat == 0, slope_hi, jnp.where(feat == 1, slope_vec - slope_hi, 0.0)).astype(BF16)
        m_scr[hh] = jnp.full(m_scr.shape[1:], NEG_BIG, F32)
        acc_scr[hh] = jnp.zeros(acc_scr.shape[1:], F32)
        p_scr[2 * hh + 1] = jnp.zeros(p_scr.shape[1:], BF16)

    def lanes(blk):
        return slice(blk * kt_size, (blk + 1) * kt_size)

    upper_blocks = (1, 3)

    def scores(hh, t, slot, blocks=None):
        start = pl.multiple_of(t * kt_size, kt_size)
        kt = jnp.concatenate([k_ref[pl.ds(start, kt_size), head_cols(hh)], pos_scr[...]], axis=1)
        if blocks is None:
            s_scr[2 * hh + slot] = jnp.dot(kt, qq_scr[hh], preferred_element_type=F32)
        else:
            for blk in blocks:
                s_scr[2 * hh + slot, :, lanes(blk)] = jnp.dot(
                    kt, qq_scr[hh, :, lanes(blk)], preferred_element_type=F32)

    ones_rows = jnp.ones((ONES_ROWS, kt_size), BF16)

    def values_t(hh, t):
        start = pl.multiple_of(jnp.maximum(t, 0) * kt_size, kt_size)
        return jnp.concatenate([vt_ref[head_cols(hh), pl.ds(start, kt_size)], ones_rows], axis=0)

    def weighted_values(hh, t, slot):
        return jnp.dot(values_t(hh, t), p_scr[2 * hh + slot], preferred_element_type=F32)

    def step(hh, t, slot, mask=None, prefetch_blocks=None):
        scores(hh, t + 1, 1 - slot, prefetch_blocks)
        c = slope2[hh] * (t * kt_size - qi * tile).astype(F32)
        pv = weighted_values(hh, t - 1, 1 - slot)
        for col in range(2 * tile // LANES):
            sl = slice(col * LANES, (col + 1) * LANES)
            s = s_scr[2 * hh + slot, :, sl]
            if mask is not None:
                s = jnp.where(mask[:, sl], s, NEG_BIG)
            m_old = m_scr[hh, :, sl]
            m_new = jnp.maximum(m_old, jnp.max(s, axis=0, keepdims=True) + c)
            p_scr[2 * hh + slot, :, sl] = jnp.exp2(s - (m_new - c)).astype(BF16)
            m_scr[hh, :, sl] = m_new
            acc_scr[hh, :, sl] = jnp.exp2(m_old - m_new) * (acc_scr[hh, :, sl] + pv[:, sl])

    for hh in range(heads):
        scores(hh, 0, 0)

    def body(i, carry):
        for slot in range(2):
            for hh in range(heads):
                step(hh, 2 * i + slot, slot)
        return carry

    lax.fori_loop(0, qi, body, 0)

    kk = lax.broadcasted_iota(jnp.int32, (kt_size, 2 * tile), 0)
    qpos = lax.broadcasted_iota(jnp.int32, (kt_size, 2 * tile), 1)
    qpos = jnp.where(qpos >= tile, qpos - tile, qpos)
    for hh in range(heads):
        step(hh, 2 * qi, 0, mask=kk <= qpos, prefetch_blocks=upper_blocks)

    tri = (lax.broadcasted_iota(jnp.int32, (kt_size, kt_size), 0)
           <= lax.broadcasted_iota(jnp.int32, (kt_size, kt_size), 1))

    def last_tile(hh):
        t = 2 * qi + 1
        c = slope2[hh] * kt_size
        acc = acc_scr[hh] + weighted_values(hh, t - 1, 0)
        vt = values_t(hh, t)
        parts = []
        for blk in range(2 * tile // kt_size):
            part = acc[:, lanes(blk)]
            if blk in upper_blocks:
                s = jnp.where(tri, s_scr[2 * hh + 1, :, lanes(blk)], NEG_BIG)
                m_old = m_scr[hh, :, lanes(blk)]
                m_new = jnp.maximum(m_old, jnp.max(s, axis=0, keepdims=True) + c)
                p = jnp.exp2(s - (m_new - c)).astype(BF16)
                part = (jnp.exp2(m_old - m_new) * part
                        + jnp.dot(vt, p, preferred_element_type=F32))
            parts.append(part)
        return jnp.concatenate(parts, axis=1)

    lam = lam_ref[...]
    lam_full = (jnp.exp(jnp.sum(lam[0:1] * lam[1:2], axis=1, keepdims=True))
                - jnp.exp(jnp.sum(lam[2:3] * lam[3:4], axis=1, keepdims=True)) + LAM_INIT)
    for hh in range(heads):
        acc = last_tile(hh)
        l = acc[DA_V_DIM:DA_V_DIM + 1, :]
        acc = acc[:DA_V_DIM, :]
        o = acc[:, :tile] / l[:, :tile] - lam_full * (acc[:, tile:] / l[:, tile:])
        o = o * lax.rsqrt(jnp.mean(o * o, axis=0, keepdims=True) + EPS) * (1.0 - LAM_INIT)
        o_ref[:, head_cols(hh)] = (o.T * g_ref[:, head_cols(hh)]).astype(o_ref.dtype)


def _attention(slopes, lam, proj3, vt, g, batch, seq, tile, heads):
    kern = functools.partial(_attn_kernel, tile=tile, heads=heads)
    width = heads * DA_V_DIM
    groups = DA_HEADS // heads
    return pl.pallas_call(
        kern,
        grid=(batch, groups, seq // tile),
        in_specs=[
            pl.BlockSpec(memory_space=pltpu.SMEM),
            pl.BlockSpec((4, DA_HEAD_DIM), lambda b, h, i: (0, 0)),
            pl.BlockSpec((None, tile, width), lambda b, h, i: (b, i, COL_DA_Q * groups + h)),
            pl.BlockSpec((None, seq, width), lambda b, h, i: (b, 0, COL_DA_K * groups + h)),
            pl.BlockSpec((None, width, seq), lambda b, h, i: (b, h, 0)),
            pl.BlockSpec((1, width), lambda b, h, i: (0, h)),
        ],
        out_specs=pl.BlockSpec((None, tile, width), lambda b, h, i: (b, i, h)),
        out_shape=jax.ShapeDtypeStruct((batch, seq, D_MODEL), BF16),
        scratch_shapes=[
            pltpu.VMEM((heads, 2 * DA_V_DIM, 2 * tile), BF16),
            pltpu.VMEM((tile // 2, DA_V_DIM), BF16),
            pltpu.VMEM((2 * heads, tile // 2, 2 * tile), F32),
            pltpu.VMEM((2 * heads, tile // 2, 2 * tile), BF16),
            pltpu.VMEM((heads, DA_V_DIM + ONES_ROWS, 2 * tile), F32),
            pltpu.VMEM((heads, 1, 2 * tile), F32),
        ],
        compiler_params=pltpu.CompilerParams(
            dimension_semantics=("parallel", "parallel", "arbitrary"),
            vmem_limit_bytes=VMEM_LIMIT),
    )(slopes, lam, proj3, proj3, vt, g)


def _rms(x, g):
    return x * lax.rsqrt(jnp.mean(x * x, axis=-1, keepdims=True) + EPS) * g


def _mlstm_chunk(r0, chunk, qk_ref, v_ref, og_ref, grow_ref, gcol_ref, cw_ref, cb_ref, brow_ref,
                 bcol_ref, ng_ref, c_scr, n_scr, m_scr, ext_scr, mout_scr):
    pad = SUBLANES
    rows = slice(r0, r0 + chunk)

    ext_scr[pad:pad + chunk, :] = qk_ref[rows, :].astype(F32)
    conv = cb_ref[...] + cw_ref[CONV_WIDTH - 1:CONV_WIDTH, :] * ext_scr[pad:pad + chunk, :]
    for tap in range(1, CONV_WIDTH):
        conv = conv + (cw_ref[CONV_WIDTH - 1 - tap:CONV_WIDTH - tap, :]
                       * ext_scr[pad - tap:pad - tap + chunk, :])
    ext_scr[0:pad, :] = ext_scr[chunk:chunk + pad, :]
    qk = conv * _sigmoid(conv)

    g_rows = grow_ref[:, rows] + brow_ref[...]
    g_cols = gcol_ref[rows, :] + bcol_ref[...]
    r_idx = lax.broadcasted_iota(jnp.int32, (chunk, chunk), 0)
    c_idx = lax.broadcasted_iota(jnp.int32, (chunk, chunk), 1)
    causal = r_idx >= c_idx
    tril = jnp.where(causal, 1.0, 0.0).astype(F32)
    triu = jnp.where(r_idx <= c_idx, 1.0, 0.0).astype(F32)
    b_rows = jnp.dot(_log_sigmoid(g_rows), triu, preferred_element_type=F32,
                     precision=lax.Precision.HIGHEST)
    b_cols = jnp.dot(tril, _log_sigmoid(g_cols), preferred_element_type=F32,
                     precision=lax.Precision.HIGHEST)

    q_scale = ML_QK_DIM ** -0.5
    for hd in range(ML_HEADS):
        yield
        qf = qk[:, hd * ML_QK_DIM:(hd + 1) * ML_QK_DIM] * q_scale
        kf = qk[:, ML_QK_WIDTH + hd * ML_QK_DIM:ML_QK_WIDTH + (hd + 1) * ML_QK_DIM]
        qb = qf.astype(BF16)
        sl = slice(hd * ML_V_DIM, (hd + 1) * ML_V_DIM)
        vb = v_ref[rows, sl]
        fcol = ML_HEADS + hd
        bt = b_cols[:, fcol:fcol + 1]
        bs = b_rows[fcol:fcol + 1, :]
        i_row = g_rows[hd:hd + 1, :]
        i_col = g_cols[:, hd:hd + 1]
        m_prev = m_scr[hd]
        c_prev = c_scr[hd]
        n_prev = n_scr[hd]

        logd = jnp.where(causal, bt - bs + i_row, NEG_BIG)
        inter = bt + m_prev
        m_t = jnp.maximum(inter, jnp.max(logd, axis=1, keepdims=True))
        dmat = jnp.exp(logd - m_t)
        sc = lax.dot_general(qb, kf.astype(BF16), NT_DIMS, preferred_element_type=F32) * dmat
        w_inter = jnp.exp(inter - m_t)
        num = (w_inter * jnp.dot(qb, c_prev.astype(BF16), preferred_element_type=F32)
               + jnp.dot(sc.astype(BF16), vb, preferred_element_type=F32))
        den = (w_inter * jnp.sum(qf * n_prev, axis=1, keepdims=True)
               + jnp.sum(sc, axis=1, keepdims=True))
        hh = num / jnp.maximum(jnp.abs(den), jnp.exp(-m_t))

        g_last = bt[chunk - 1:chunk, :]
        log_w = g_last - bt + i_col
        m_new = jnp.maximum(g_last + m_prev, jnp.max(log_w, axis=0, keepdims=True))
        kw = kf * jnp.exp(log_w - m_new)
        decay = jnp.exp(g_last + m_prev - m_new)
        c_scr[hd] = decay * c_prev + lax.dot_general(
            kw.astype(BF16), vb, TN_DIMS, preferred_element_type=F32)
        n_scr[hd] = decay * n_prev + jnp.sum(kw, axis=0, keepdims=True)
        m_scr[hd] = m_new

        hn = hh * lax.rsqrt(jnp.mean(hh * hh, axis=1, keepdims=True) + EPS) * ng_ref[:, sl]
        mout_scr[rows, sl] = (_sigmoid(og_ref[rows, sl].astype(F32)) * hn).astype(mout_scr.dtype)


def _mlstm_tail_kernel(qk_ref, v_ref, og_ref, grow_ref, gcol_ref, cw_ref, cb_ref, brow_ref,
                       bcol_ref, ng_ref,
                       x_ref, a_ref, ga_ref, gm_ref, bm_ref, wa_ref, wm_ref, wo_ref,
                       gmlp_ref, w1_ref, w2_ref, gfin_ref,
                       o_ref, c_scr, n_scr, m_scr, ext_scr, mout_scr, *, chunk, tiles_per_seq):
    i = pl.program_id(0)

    @pl.when(i == 0)
    def _():
        mout_scr[...] = jnp.zeros(mout_scr.shape, mout_scr.dtype)

    @pl.when(i % tiles_per_seq == 0)
    def _():
        c_scr[...] = jnp.zeros(c_scr.shape, F32)
        n_scr[...] = jnp.zeros(n_scr.shape, F32)
        m_scr[...] = jnp.zeros(m_scr.shape, F32)
        ext_scr[0:SUBLANES, :] = jnp.zeros((SUBLANES, ext_scr.shape[1]), F32)

    def tail_stages():
        ya = jnp.dot(a_ref[...], wa_ref[...], preferred_element_type=F32)
        ym = jnp.dot(mout_scr[...], wm_ref[...], preferred_element_type=F32)
        gate_a = _sigmoid(ga_ref[...].astype(F32) + bm_ref[:, :D_MODEL])
        gate_m = _sigmoid(gm_ref[...].astype(F32) + bm_ref[:, D_MODEL:])
        merged = (gate_a * ya + gate_m * ym).astype(BF16)
        yield
        x1 = x_ref[...] + jnp.dot(merged, wo_ref[...], preferred_element_type=F32)
        hm = _rms(x1, gmlp_ref[...]).astype(BF16)
        acc = x1
        for c in range(D_FF // FF_BLOCK):
            yield
            cols = slice(c * FF_BLOCK, (c + 1) * FF_BLOCK)
            u = jnp.maximum(jnp.dot(hm, w1_ref[:, cols], preferred_element_type=F32), 0.0)
            acc = acc + jnp.dot((u * u).astype(BF16), w2_ref[cols, :], preferred_element_type=F32)
        o_ref[...] = _rms(acc, gfin_ref[...])

    def mlstm_stages():
        for r0 in range(0, mout_scr.shape[0], chunk):
            yield from _mlstm_chunk(r0, chunk, qk_ref, v_ref, og_ref, grow_ref, gcol_ref, cw_ref,
                                    cb_ref, brow_ref, bcol_ref, ng_ref, c_scr, n_scr, m_scr,
                                    ext_scr, mout_scr)

    pending = [tail_stages(), mlstm_stages()]
    while pending:
        for stage in list(pending):
            if next(stage, StopIteration) is StopIteration:
                pending.remove(stage)


def _mlstm_tail(proj3, proj, g_rows, g_cols3, conv_w, conv_b, b_row, b_col, norm_g,
                x2, a2, b_merge, wa, wm, wo, g_mlp, w1, w2, g_fin, batch, seq, tm, chunk):
    tokens = batch * seq
    n_tiles = tokens // tm
    tps = seq // tm
    kern = functools.partial(_mlstm_tail_kernel, chunk=chunk, tiles_per_seq=tps)

    def cur(i):
        return jnp.minimum(i, n_tiles - 1)

    def prev(i):
        return jnp.maximum(i - 1, 0)

    def resident(arr):
        return pl.BlockSpec(arr.shape, lambda i: (0,) * arr.ndim, pipeline_mode=pl.Buffered(1))

    def seq_block(col):
        return pl.BlockSpec((None, tm, D_MODEL), lambda i: (cur(i) // tps, cur(i) % tps, col))

    return pl.pallas_call(
        kern,
        grid=(n_tiles + 1,),
        in_specs=[
            seq_block(COL_ML_QK),
            seq_block(COL_ML_V),
            seq_block(COL_ML_O),
            pl.BlockSpec((None, 2 * ML_HEADS, tm), lambda i: (cur(i) // tps, 0, cur(i) % tps)),
            pl.BlockSpec((None, tm, LANES), lambda i: (cur(i) // tps, cur(i) % tps, 0)),
            resident(conv_w), resident(conv_b), resident(b_row), resident(b_col), resident(norm_g),
            pl.BlockSpec((tm, D_MODEL), lambda i: (prev(i), 0)),
            pl.BlockSpec((tm, D_MODEL), lambda i: (prev(i), 0)),
            pl.BlockSpec((tm, D_MODEL), lambda i: (prev(i), COL_MG_A)),
            pl.BlockSpec((tm, D_MODEL), lambda i: (prev(i), COL_MG_M)),
            resident(b_merge), resident(wa), resident(wm), resident(wo), resident(g_mlp),
            resident(w1), resident(w2), resident(g_fin),
        ],
        out_specs=pl.BlockSpec((tm, D_MODEL), lambda i: (prev(i), 0)),
        out_shape=jax.ShapeDtypeStruct((tokens, D_MODEL), F32),
        scratch_shapes=[
            pltpu.VMEM((ML_HEADS, ML_QK_DIM, ML_V_DIM), F32),
            pltpu.VMEM((ML_HEADS, 1, ML_QK_DIM), F32),
            pltpu.VMEM((ML_HEADS, 1, 1), F32),
            pltpu.VMEM((chunk + 2 * SUBLANES, D_MODEL), F32),
            pltpu.VMEM((tm, D_MODEL), BF16),
        ],
        compiler_params=pltpu.CompilerParams(
            dimension_semantics=("arbitrary",), vmem_limit_bytes=VMEM_LIMIT),
    )(proj3, proj3, proj3, g_rows, g_cols3, conv_w, conv_b, b_row, b_col, norm_g,
      x2, a2, proj, proj, b_merge, wa, wm, wo, g_mlp, w1, w2, g_fin)


def kernel(x, norm_mix_g, w_in, b_gates, conv_w, conv_b, lam, da_norm_g, ml_norm_g, b_merge,
           w_branch_a, w_branch_m, w_out, norm_mlp_g, w_ff1, w_ff2, norm_final_g):
    batch, seq, _ = x.shape
    tokens = batch * seq
    x2 = x.reshape(tokens, D_MODEL)

    assert w_in.shape == (1, D_MODEL, D_IN)
    w_bf = w_in[0].astype(BF16)
    w_tail = w_bf[:, OFF_ML_O:]

    tm_proj = min(512, seq)
    proj, vt, ifg, ifg_t = _in_proj(x2, norm_mix_g, w_bf, w_tail, batch, seq, tm_proj)
    proj3 = proj.reshape(batch, seq, N_PROJ_BLOCKS * D_MODEL)

    slopes = 2.0 ** (-8.0 * jnp.arange(1, DA_HEADS + 1, dtype=F32) / DA_HEADS)
    attn_tile = min(512, seq)
    a_out = _attention(slopes, lam[0], proj3, vt, da_norm_g, batch, seq, attn_tile, 4)

    ifg3 = ifg.reshape(batch, seq, LANES)
    b_row = b_gates[0].reshape(2 * ML_HEADS, 1)
    b_col = jnp.pad(b_gates, ((0, 0), (0, LANES - 2 * ML_HEADS)))
    chunk = min(256, seq)
    out = _mlstm_tail(proj3, proj, ifg_t, ifg3, conv_w[0], conv_b, b_row, b_col, ml_norm_g,
                      x2, a_out.reshape(tokens, D_MODEL), b_merge,
                      w_branch_a[0].astype(BF16), w_branch_m[0].astype(BF16), w_out[0].astype(BF16),
                      norm_mlp_g, w_ff1[0].astype(BF16), w_ff2[0].astype(BF16),
                      norm_final_g.reshape(1, D_MODEL), batch, seq, min(512, seq), chunk)
    return out.reshape(batch, seq, D_MODEL)
```

```python
import functools
import math

import jax
import jax.numpy as jnp
from jax import lax
from jax.experimental import pallas as pl
from jax.experimental.pallas import tpu as pltpu

F32 = jnp.float32
BF16 = jnp.bfloat16

D_MODEL = 1024
DA_HEADS = 8
DA_HEAD_DIM = 64
DA_V_DIM = 2 * DA_HEAD_DIM
ML_HEADS = 4
ML_V_DIM = D_MODEL // ML_HEADS
ML_QK_DIM = ML_V_DIM // 2
ML_QK_WIDTH = ML_HEADS * ML_QK_DIM
CONV_WIDTH = 4
D_FF = 4 * D_MODEL
EPS = 1e-6
LAM_INIT = 0.8 - 0.6 * math.exp(-0.3 * 0)
NEG_BIG = -1e30
LOG2E = math.log2(math.e)
ONES_ROWS = 16
FF_BLOCK = 512

LANES = 128
SUBLANES = 8
VMEM_LIMIT = 60 * 1024 * 1024

COL_DA_Q, COL_DA_K, COL_ML_QK, COL_ML_V, COL_ML_O, COL_MG_A, COL_MG_M = range(7)
N_PROJ_BLOCKS = 7

OFF_DA_Q = 0
OFF_DA_K = OFF_DA_Q + D_MODEL
OFF_DA_V = OFF_DA_K + D_MODEL
OFF_ML_Q = OFF_DA_V + D_MODEL
OFF_ML_V = OFF_ML_Q + 2 * ML_QK_WIDTH
OFF_ML_IF = OFF_ML_V + D_MODEL
OFF_ML_O = OFF_ML_IF + 2 * ML_HEADS
D_IN = OFF_ML_O + 3 * D_MODEL

NT_DIMS = (((1,), (1,)), ((), ()))
TN_DIMS = (((0,), (0,)), ((), ()))
TT_DIMS = (((0,), (1,)), ((), ()))


def _sigmoid(x):
    return 1.0 / (1.0 + jnp.exp(-x))


def _log_sigmoid(x):
    return jnp.minimum(x, 0.0) - jnp.log(1.0 + jnp.exp(-jnp.abs(x)))


def _in_proj_kernel(x_ref, g_ref, w_ref, wtail_ref, proj_ref, vt_ref, if_ref, ift_ref):
    x = x_ref[...]
    hb = (x * lax.rsqrt(jnp.mean(x * x, axis=-1, keepdims=True) + EPS) * g_ref[...]).astype(BF16)

    def project(w_cols):
        return jnp.dot(hb, w_cols, preferred_element_type=F32)

    def project_t(w_cols):
        return lax.dot_general(w_cols, hb, TT_DIMS, preferred_element_type=F32)

    proj_ref[:, 0:D_MODEL] = (project(w_ref[:, OFF_DA_Q:OFF_DA_K]) * DA_HEAD_DIM ** -0.5).astype(BF16)
    proj_ref[:, D_MODEL:2 * D_MODEL] = project(w_ref[:, OFF_DA_K:OFF_DA_V]).astype(BF16)
    vt_ref[...] = project_t(w_ref[:, OFF_DA_V:OFF_ML_Q]).astype(BF16)
    for blk in range(2):
        src = OFF_ML_Q + blk * D_MODEL
        dst = (COL_ML_QK + blk) * D_MODEL
        proj_ref[:, dst:dst + D_MODEL] = project(w_ref[:, src:src + D_MODEL]).astype(BF16)
    w_if = w_ref[:, OFF_ML_IF:OFF_ML_IF + LANES]
    if_ref[...] = project(w_if)
    ift_ref[...] = project_t(w_if)
    for blk in range(3):
        dst = (COL_ML_O + blk) * D_MODEL
        proj_ref[:, dst:dst + D_MODEL] = project(
            wtail_ref[:, blk * D_MODEL:(blk + 1) * D_MODEL]).astype(BF16)


def _in_proj(x2, g, w_bf, w_tail, batch, seq, tm):
    tokens = batch * seq
    nsb = seq // tm

    def resident(arr):
        return pl.BlockSpec(arr.shape, lambda i: (0, 0), pipeline_mode=pl.Buffered(1))

    return pl.pallas_call(
        _in_proj_kernel,
        grid=(tokens // tm,),
        in_specs=[
            pl.BlockSpec((tm, D_MODEL), lambda i: (i, 0)),
            resident(g),
            resident(w_bf),
            resident(w_tail),
        ],
        out_specs=[
            pl.BlockSpec((tm, N_PROJ_BLOCKS * D_MODEL), lambda i: (i, 0)),
            pl.BlockSpec((None, D_MODEL, tm), lambda i: (i // nsb, 0, i % nsb)),
            pl.BlockSpec((tm, LANES), lambda i: (i, 0)),
            pl.BlockSpec((None, LANES, tm), lambda i: (i // nsb, 0, i % nsb)),
        ],
        out_shape=[
            jax.ShapeDtypeStruct((tokens, N_PROJ_BLOCKS * D_MODEL), BF16),
            jax.ShapeDtypeStruct((batch, D_MODEL, seq), BF16),
            jax.ShapeDtypeStruct((tokens, LANES), F32),
            jax.ShapeDtypeStruct((batch, LANES, seq), F32),
        ],
        compiler_params=pltpu.CompilerParams(
            dimension_semantics=("parallel",), vmem_limit_bytes=VMEM_LIMIT),
    )(x2, g, w_bf, w_tail)


def _attn_kernel(slopes_ref, lam_ref, q_ref, k_ref, vt_ref, g_ref, o_ref,
                 qq_scr, pos_scr, s_scr, p_scr, acc_scr, m_scr, *, tile, heads):
    group = pl.program_id(1)
    qi = pl.program_id(2)
    kt_size = tile // 2
    width = 2 * tile
    slope2 = [slopes_ref[group * heads + hh] * LOG2E for hh in range(heads)]

    def head_cols(hh):
        return slice(hh * DA_V_DIM, (hh + 1) * DA_V_DIM)

    key_off = lax.broadcasted_iota(jnp.int32, pos_scr.shape, 0).astype(F32)
    lane = lax.broadcasted_iota(jnp.int32, pos_scr.shape, 1)
    pos_scr[...] = jnp.where(lane < 2, key_off, 0.0).astype(BF16)
    for hh in range(heads):
        qt = q_ref[:, head_cols(hh)].astype(F32).T * LOG2E
        d_idx = lax.broadcasted_iota(jnp.int32, qt.shape, 0)
        q0 = jnp.where(d_idx < DA_HEAD_DIM, qt, 0.0)
        q1 = jnp.where(d_idx >= DA_HEAD_DIM, qt, 0.0)
        qq_scr[hh, 0:DA_V_DIM, :width] = jnp.concatenate([q0, q1], axis=1).astype(BF16)
        slope_vec = jnp.full((DA_V_DIM, 2 * tile), slope2[hh], F32)
        slope_hi = slope_vec.astype(BF16).astype(F32)
        feat = lax.broadcasted_iota(jnp.int32, slope_vec.shape, 0)
        qq_scr[hh, DA_V_DIM:, :width] = jnp.where(
            feat == 0, slope_hi, jnp.where(feat == 1, slope_vec - slope_hi, 0.0)).astype(BF16)
        m_scr[hh] = jnp.full(m_scr.shape[1:], NEG_BIG, F32)
        acc_scr[hh, :, :width] = jnp.zeros((acc_scr.shape[1], width), F32)
        p_scr[2 * hh + 1, :, :width] = jnp.zeros((kt_size, width), BF16)

    def lanes(blk):
        return slice(blk * kt_size, (blk + 1) * kt_size)

    upper_blocks = (1, 3)

    def scores(hh, t, slot, blocks=None):
        start = pl.multiple_of(t * kt_size, kt_size)
        kt = jnp.concatenate([k_ref[pl.ds(start, kt_size), head_cols(hh)], pos_scr[...]], axis=1)
        if blocks is None:
            s_scr[2 * hh + slot, :, :width] = jnp.dot(
                kt, qq_scr[hh, :, :width], preferred_element_type=F32)
        else:
            for blk in blocks:
                s_scr[2 * hh + slot, :, lanes(blk)] = jnp.dot(
                    kt, qq_scr[hh, :, lanes(blk)], preferred_element_type=F32)

    ones_rows = jnp.ones((ONES_ROWS, kt_size), BF16)

    def values_t(hh, t):
        start = pl.multiple_of(jnp.maximum(t, 0) * kt_size, kt_size)
        return jnp.concatenate([vt_ref[head_cols(hh), pl.ds(start, kt_size)], ones_rows], axis=0)

    def weighted_values(hh, t, slot):
        return jnp.dot(values_t(hh, t), p_scr[2 * hh + slot, :, :width],
                       preferred_element_type=F32)

    def step(hh, t, slot, mask=None, prefetch_blocks=None):
        scores(hh, t + 1, 1 - slot, prefetch_blocks)
        c = slope2[hh] * (t * kt_size - qi * tile).astype(F32)
        s = s_scr[2 * hh + slot, :, :width]
        if mask is not None:
            s = jnp.where(mask, s, NEG_BIG)
        m_old = m_scr[hh]
        m_new = jnp.maximum(m_old, jnp.max(s, axis=0, keepdims=True) + c)
        p = jnp.exp2(s - (m_new - c))
        alpha = jnp.exp2(m_old - m_new)
        p_scr[2 * hh + slot, :, :width] = p.astype(BF16)
        m_scr[hh] = m_new
        acc_scr[hh, :, :width] = alpha * (acc_scr[hh, :, :width]
                                          + weighted_values(hh, t - 1, 1 - slot))

    for hh in range(heads):
        scores(hh, 0, 0)

    def body(i, carry):
        for slot in range(2):
            for hh in range(heads):
                step(hh, 2 * i + slot, slot)
        return carry

    lax.fori_loop(0, qi, body, 0)

    kk = lax.broadcasted_iota(jnp.int32, (kt_size, 2 * tile), 0)
    qpos = lax.broadcasted_iota(jnp.int32, (kt_size, 2 * tile), 1)
    qpos = jnp.where(qpos >= tile, qpos - tile, qpos)
    for hh in range(heads):
        step(hh, 2 * qi, 0, mask=kk <= qpos, prefetch_blocks=upper_blocks)

    tri = (lax.broadcasted_iota(jnp.int32, (kt_size, kt_size), 0)
           <= lax.broadcasted_iota(jnp.int32, (kt_size, kt_size), 1))

    def last_tile(hh):
        t = 2 * qi + 1
        c = slope2[hh] * kt_size
        acc = acc_scr[hh, :, :width] + weighted_values(hh, t - 1, 0)
        vt = values_t(hh, t)
        parts = []
        for blk in range(2 * tile // kt_size):
            part = acc[:, lanes(blk)]
            if blk in upper_blocks:
                s = jnp.where(tri, s_scr[2 * hh + 1, :, lanes(blk)], NEG_BIG)
                m_old = m_scr[hh, :, lanes(blk)]
                m_new = jnp.maximum(m_old, jnp.max(s, axis=0, keepdims=True) + c)
                p = jnp.exp2(s - (m_new - c)).astype(BF16)
                part = (jnp.exp2(m_old - m_new) * part
                        + jnp.dot(vt, p, preferred_element_type=F32))
            parts.append(part)
        return jnp.concatenate(parts, axis=1)

    lam = lam_ref[...]
    lam_full = (jnp.exp(jnp.sum(lam[0:1] * lam[1:2], axis=1, keepdims=True))
                - jnp.exp(jnp.sum(lam[2:3] * lam[3:4], axis=1, keepdims=True)) + LAM_INIT)
    for hh in range(heads):
        acc = last_tile(hh)
        l = acc[DA_V_DIM:DA_V_DIM + 1, :]
        acc = acc[:DA_V_DIM, :]
        o = acc[:, :tile] / l[:, :tile] - lam_full * (acc[:, tile:] / l[:, tile:])
        o = o * lax.rsqrt(jnp.mean(o * o, axis=0, keepdims=True) + EPS) * (1.0 - LAM_INIT)
        o_ref[:, head_cols(hh)] = (o.T * g_ref[:, head_cols(hh)]).astype(o_ref.dtype)


def _attention(slopes, lam, proj3, vt, g, batch, seq, tile, heads):
    kern = functools.partial(_attn_kernel, tile=tile, heads=heads)
    width = heads * DA_V_DIM
    groups = DA_HEADS // heads
    pitch = 2 * tile + LANES
    return pl.pallas_call(
        kern,
        grid=(batch, groups, seq // tile),
        in_specs=[
            pl.BlockSpec(memory_space=pltpu.SMEM),
            pl.BlockSpec((4, DA_HEAD_DIM), lambda b, h, i: (0, 0)),
            pl.BlockSpec((None, tile, width), lambda b, h, i: (b, i, COL_DA_Q * groups + h)),
            pl.BlockSpec((None, seq, width), lambda b, h, i: (b, 0, COL_DA_K * groups + h)),
            pl.BlockSpec((None, width, seq), lambda b, h, i: (b, h, 0)),
            pl.BlockSpec((1, width), lambda b, h, i: (0, h)),
        ],
        out_specs=pl.BlockSpec((None, tile, width), lambda b, h, i: (b, i, h)),
        out_shape=jax.ShapeDtypeStruct((batch, seq, D_MODEL), BF16),
        scratch_shapes=[
            pltpu.VMEM((heads, 2 * DA_V_DIM, pitch), BF16),
            pltpu.VMEM((tile // 2, DA_V_DIM), BF16),
            pltpu.VMEM((2 * heads, tile // 2, pitch), F32),
            pltpu.VMEM((2 * heads, tile // 2, pitch), BF16),
            pltpu.VMEM((heads, DA_V_DIM + ONES_ROWS, pitch), F32),
            pltpu.VMEM((heads, 1, 2 * tile), F32),
        ],
        compiler_params=pltpu.CompilerParams(
            dimension_semantics=("parallel", "parallel", "arbitrary"),
            vmem_limit_bytes=VMEM_LIMIT),
    )(slopes, lam, proj3, proj3, vt, g)


def _rms(x, g):
    return x * lax.rsqrt(jnp.mean(x * x, axis=-1, keepdims=True) + EPS) * g


def _mlstm_chunk(r0, chunk, qk_ref, v_ref, og_ref, grow_ref, gcol_ref, cw_ref, cb_ref, brow_ref,
                 bcol_ref, ng_ref, c_scr, n_scr, m_scr, ext_scr, mout_scr):
    pad = SUBLANES
    rows = slice(r0, r0 + chunk)

    ext_scr[pad:pad + chunk, :] = qk_ref[rows, :].astype(F32)
    conv = cb_ref[...] + cw_ref[CONV_WIDTH - 1:CONV_WIDTH, :] * ext_scr[pad:pad + chunk, :]
    for tap in range(1, CONV_WIDTH):
        conv = conv + (cw_ref[CONV_WIDTH - 1 - tap:CONV_WIDTH - tap, :]
                       * ext_scr[pad - tap:pad - tap + chunk, :])
    ext_scr[0:pad, :] = ext_scr[chunk:chunk + pad, :]
    qk = conv * _sigmoid(conv)

    g_rows = grow_ref[:, rows] + brow_ref[...]
    g_cols = gcol_ref[rows, :] + bcol_ref[...]
    r_idx = lax.broadcasted_iota(jnp.int32, (chunk, chunk), 0)
    c_idx = lax.broadcasted_iota(jnp.int32, (chunk, chunk), 1)
    causal = r_idx >= c_idx
    tril = jnp.where(causal, 1.0, 0.0).astype(F32)
    triu = jnp.where(r_idx <= c_idx, 1.0, 0.0).astype(F32)
    b_rows = jnp.dot(_log_sigmoid(g_rows), triu, preferred_element_type=F32,
                     precision=lax.Precision.HIGHEST)
    b_cols = jnp.dot(tril, _log_sigmoid(g_cols), preferred_element_type=F32,
                     precision=lax.Precision.HIGHEST)

    q_scale = ML_QK_DIM ** -0.5
    for hd in range(ML_HEADS):
        yield
        qf = qk[:, hd * ML_QK_DIM:(hd + 1) * ML_QK_DIM] * q_scale
        kf = qk[:, ML_QK_WIDTH + hd * ML_QK_DIM:ML_QK_WIDTH + (hd + 1) * ML_QK_DIM]
        qb = qf.astype(BF16)
        sl = slice(hd * ML_V_DIM, (hd + 1) * ML_V_DIM)
        vb = v_ref[rows, sl]
        fcol = ML_HEADS + hd
        bt = b_cols[:, fcol:fcol + 1]
        bs = b_rows[fcol:fcol + 1, :]
        i_row = g_rows[hd:hd + 1, :]
        i_col = g_cols[:, hd:hd + 1]
        m_prev = m_scr[hd]
        c_prev = c_scr[hd]
        n_prev = n_scr[hd]

        logd = jnp.where(causal, bt - bs + i_row, NEG_BIG)
        inter = bt + m_prev
        m_t = jnp.maximum(inter, jnp.max(logd, axis=1, keepdims=True))
        dmat = jnp.exp(logd - m_t)
        sc = lax.dot_general(qb, kf.astype(BF16), NT_DIMS, preferred_element_type=F32) * dmat
        w_inter = jnp.exp(inter - m_t)
        num = (w_inter * jnp.dot(qb, c_prev.astype(BF16), preferred_element_type=F32)
               + jnp.dot(sc.astype(BF16), vb, preferred_element_type=F32))
        den = (w_inter * jnp.sum(qf * n_prev, axis=1, keepdims=True)
               + jnp.sum(sc, axis=1, keepdims=True))
        hh = num / jnp.maximum(jnp.abs(den), jnp.exp(-m_t))

        g_last = bt[chunk - 1:chunk, :]
        log_w = g_last - bt + i_col
        m_new = jnp.maximum(g_last + m_prev, jnp.max(log_w, axis=0, keepdims=True))
        kw = kf * jnp.exp(log_w - m_new)
        decay = jnp.exp(g_last + m_prev - m_new)
        c_scr[hd] = decay * c_prev + lax.dot_general(
            kw.astype(BF16), vb, TN_DIMS, preferred_element_type=F32)
        n_scr[hd] = decay * n_prev + jnp.sum(kw, axis=0, keepdims=True)
        m_scr[hd] = m_new

        hn = hh * lax.rsqrt(jnp.mean(hh * hh, axis=1, keepdims=True) + EPS) * ng_ref[:, sl]
        mout_scr[rows, sl] = (_sigmoid(og_ref[rows, sl].astype(F32)) * hn).astype(mout_scr.dtype)


def _mlstm_tail_kernel(qk_ref, v_ref, og_ref, grow_ref, gcol_ref, cw_ref, cb_ref, brow_ref,
                       bcol_ref, ng_ref,
                       x_ref, a_ref, ga_ref, gm_ref, bm_ref, wa_ref, wm_ref, wo_ref,
                       gmlp_ref, w1_ref, w2_ref, gfin_ref,
                       o_ref, c_scr, n_scr, m_scr, ext_scr, mout_scr, *, chunk, tiles_per_seq):
    i = pl.program_id(0)

    @pl.when(i == 0)
    def _():
        mout_scr[...] = jnp.zeros(mout_scr.shape, mout_scr.dtype)

    @pl.when(i % tiles_per_seq == 0)
    def _():
        c_scr[...] = jnp.zeros(c_scr.shape, F32)
        n_scr[...] = jnp.zeros(n_scr.shape, F32)
        m_scr[...] = jnp.zeros(m_scr.shape, F32)
        ext_scr[0:SUBLANES, :] = jnp.zeros((SUBLANES, ext_scr.shape[1]), F32)

    def tail_stages():
        ya = jnp.dot(a_ref[...], wa_ref[...], preferred_element_type=F32)
        ym = jnp.dot(mout_scr[...], wm_ref[...], preferred_element_type=F32)
        gate_a = _sigmoid(ga_ref[...].astype(F32) + bm_ref[:, :D_MODEL])
        gate_m = _sigmoid(gm_ref[...].astype(F32) + bm_ref[:, D_MODEL:])
        merged = (gate_a * ya + gate_m * ym).astype(BF16)
        yield
        x1 = x_ref[...] + jnp.dot(merged, wo_ref[...], preferred_element_type=F32)
        hm = _rms(x1, gmlp_ref[...]).astype(BF16)
        acc = x1
        for c in range(D_FF // FF_BLOCK):
            yield
            cols = slice(c * FF_BLOCK, (c + 1) * FF_BLOCK)
            u = jnp.maximum(jnp.dot(hm, w1_ref[:, cols], preferred_element_type=F32), 0.0)
            acc = acc + jnp.dot((u * u).astype(BF16), w2_ref[cols, :], preferred_element_type=F32)
        o_ref[...] = _rms(acc, gfin_ref[...])

    def mlstm_stages():
        for r0 in range(0, mout_scr.shape[0], chunk):
            yield from _mlstm_chunk(r0, chunk, qk_ref, v_ref, og_ref, grow_ref, gcol_ref, cw_ref,
                                    cb_ref, brow_ref, bcol_ref, ng_ref, c_scr, n_scr, m_scr,
                                    ext_scr, mout_scr)

    pending = [tail_stages(), mlstm_stages()]
    while pending:
        for stage in list(pending):
            if next(stage, StopIteration) is StopIteration:
                pending.remove(stage)


def _mlstm_tail(proj3, proj, g_rows, g_cols3, conv_w, conv_b, b_row, b_col, norm_g,
                x2, a2, b_merge, wa, wm, wo, g_mlp, w1, w2, g_fin, batch, seq, tm, chunk):
    tokens = batch * seq
    n_tiles = tokens // tm
    tps = seq // tm
    kern = functools.partial(_mlstm_tail_kernel, chunk=chunk, tiles_per_seq=tps)

    def cur(i):
        return jnp.minimum(i, n_tiles - 1)

    def prev(i):
        return jnp.maximum(i - 1, 0)

    def resident(arr):
        return pl.BlockSpec(arr.shape, lambda i: (0,) * arr.ndim, pipeline_mode=pl.Buffered(1))

    def seq_block(col):
        return pl.BlockSpec((None, tm, D_MODEL), lambda i: (cur(i) // tps, cur(i) % tps, col))

    return pl.pallas_call(
        kern,
        grid=(n_tiles + 1,),
        in_specs=[
            seq_block(COL_ML_QK),
            seq_block(COL_ML_V),
            seq_block(COL_ML_O),
            pl.BlockSpec((None, 2 * ML_HEADS, tm), lambda i: (cur(i) // tps, 0, cur(i) % tps)),
            pl.BlockSpec((None, tm, LANES), lambda i: (cur(i) // tps, cur(i) % tps, 0)),
            resident(conv_w), resident(conv_b), resident(b_row), resident(b_col), resident(norm_g),
            pl.BlockSpec((tm, D_MODEL), lambda i: (prev(i), 0)),
            pl.BlockSpec((tm, D_MODEL), lambda i: (prev(i), 0)),
            pl.BlockSpec((tm, D_MODEL), lambda i: (prev(i), COL_MG_A)),
            pl.BlockSpec((tm, D_MODEL), lambda i: (prev(i), COL_MG_M)),
            resident(b_merge), resident(wa), resident(wm), resident(wo), resident(g_mlp),
            resident(w1), resident(w2), resident(g_fin),
        ],
        out_specs=pl.BlockSpec((tm, D_MODEL), lambda i: (prev(i), 0)),
        out_shape=jax.ShapeDtypeStruct((tokens, D_MODEL), F32),
        scratch_shapes=[
            pltpu.VMEM((ML_HEADS, ML_QK_DIM, ML_V_DIM), F32),
            pltpu.VMEM((ML_HEADS, 1, ML_QK_DIM), F32),
            pltpu.VMEM((ML_HEADS, 1, 1), F32),
            pltpu.VMEM((chunk + 2 * SUBLANES, D_MODEL), F32),
            pltpu.VMEM((tm, D_MODEL), BF16),
        ],
        compiler_params=pltpu.CompilerParams(
            dimension_semantics=("arbitrary",), vmem_limit_bytes=VMEM_LIMIT),
    )(proj3, proj3, proj3, g_rows, g_cols3, conv_w, conv_b, b_row, b_col, norm_g,
      x2, a2, proj, proj, b_merge, wa, wm, wo, g_mlp, w1, w2, g_fin)


def kernel(x, norm_mix_g, w_in, b_gates, conv_w, conv_b, lam, da_norm_g, ml_norm_g, b_merge,
           w_branch_a, w_branch_m, w_out, norm_mlp_g, w_ff1, w_ff2, norm_final_g):
    batch, seq, _ = x.shape
    tokens = batch * seq
    x2 = x.reshape(tokens, D_MODEL)

    assert w_in.shape == (1, D_MODEL, D_IN)
    w_bf = w_in[0].astype(BF16)
    w_tail = w_bf[:, OFF_ML_O:]

    tm_proj = min(512, seq)
    proj, vt, ifg, ifg_t = _in_proj(x2, norm_mix_g, w_bf, w_tail, batch, seq, tm_proj)
    proj3 = proj.reshape(batch, seq, N_PROJ_BLOCKS * D_MODEL)

    slopes = 2.0 ** (-8.0 * jnp.arange(1, DA_HEADS + 1, dtype=F32) / DA_HEADS)
    attn_tile = min(512, seq)
    a_out = _attention(slopes, lam[0], proj3, vt, da_norm_g, batch, seq, attn_tile, 4)

    ifg3 = ifg.reshape(batch, seq, LANES)
    b_row = b_gates[0].reshape(2 * ML_HEADS, 1)
    b_col = jnp.pad(b_gates, ((0, 0), (0, LANES - 2 * ML_HEADS)))
    chunk = min(256, seq)
    out = _mlstm_tail(proj3, proj, ifg_t, ifg3, conv_w[0], conv_b, b_row, b_col, ml_norm_g,
                      x2, a_out.reshape(tokens, D_MODEL), b_merge,
                      w_branch_a[0].astype(BF16), w_branch_m[0].astype(BF16), w_out[0].astype(BF16),
                      norm_mlp_g, w_ff1[0].astype(BF16), w_ff2[0].astype(BF16),
                      norm_final_g.reshape(1, D_MODEL), batch, seq, min(512, seq), chunk)
    return out.reshape(batch, seq, D_MODEL)
```

```python
import functools
import math

import jax
import jax.numpy as jnp
from jax import lax
from jax.experimental import pallas as pl
from jax.experimental.pallas import tpu as pltpu

F32 = jnp.float32
BF16 = jnp.bfloat16

D_MODEL = 1024
DA_HEADS = 8
DA_HEAD_DIM = 64
DA_V_DIM = 2 * DA_HEAD_DIM
ML_HEADS = 4
ML_V_DIM = D_MODEL // ML_HEADS
ML_QK_DIM = ML_V_DIM // 2
ML_QK_WIDTH = ML_HEADS * ML_QK_DIM
CONV_WIDTH = 4
D_FF = 4 * D_MODEL
EPS = 1e-6
LAM_INIT = 0.8 - 0.6 * math.exp(-0.3 * 0)
NEG_BIG = -1e30
LOG2E = math.log2(math.e)
ONES_ROWS = 16
FF_BLOCK = 512
BOUND_SLACK = 1.02
MAX_SHIFT_SPREAD = 100.0

LANES = 128
SUBLANES = 8
VMEM_LIMIT = 60 * 1024 * 1024

COL_DA_Q, COL_DA_K, COL_ML_QK, COL_ML_V, COL_ML_O, COL_MG_A, COL_MG_M = range(7)
N_PROJ_BLOCKS = 7

OFF_DA_Q = 0
OFF_DA_K = OFF_DA_Q + D_MODEL
OFF_DA_V = OFF_DA_K + D_MODEL
OFF_ML_Q = OFF_DA_V + D_MODEL
OFF_ML_V = OFF_ML_Q + 2 * ML_QK_WIDTH
OFF_ML_IF = OFF_ML_V + D_MODEL
OFF_ML_O = OFF_ML_IF + 2 * ML_HEADS
D_IN = OFF_ML_O + 3 * D_MODEL

NT_DIMS = (((1,), (1,)), ((), ()))
TN_DIMS = (((0,), (0,)), ((), ()))
TT_DIMS = (((0,), (1,)), ((), ()))


def _sigmoid(x):
    return 1.0 / (1.0 + jnp.exp(-x))


def _log_sigmoid(x):
    return jnp.minimum(x, 0.0) - jnp.log(1.0 + jnp.exp(-jnp.abs(x)))


def _in_proj_kernel(x_ref, g_ref, w_ref, wtail_ref, proj_ref, vt_ref, if_ref, ift_ref, kn_ref):
    x = x_ref[...]
    hb = (x * lax.rsqrt(jnp.mean(x * x, axis=-1, keepdims=True) + EPS) * g_ref[...]).astype(BF16)

    def project(w_cols):
        return jnp.dot(hb, w_cols, preferred_element_type=F32)

    def project_t(w_cols):
        return lax.dot_general(w_cols, hb, TT_DIMS, preferred_element_type=F32)

    proj_ref[:, 0:D_MODEL] = (project(w_ref[:, OFF_DA_Q:OFF_DA_K]) * DA_HEAD_DIM ** -0.5).astype(BF16)
    k = project(w_ref[:, OFF_DA_K:OFF_DA_V])
    proj_ref[:, D_MODEL:2 * D_MODEL] = k.astype(BF16)
    r_idx = lax.broadcasted_iota(jnp.int32, (D_MODEL, LANES), 0)
    c_idx = lax.broadcasted_iota(jnp.int32, (D_MODEL, LANES), 1)
    select = jnp.where(r_idx // DA_HEAD_DIM == c_idx, 1.0, 0.0).astype(BF16)
    k_sq = jnp.dot((k * k).astype(BF16), select, preferred_element_type=F32)
    kn_ref[...] = jnp.broadcast_to(jnp.max(k_sq, axis=0, keepdims=True), kn_ref.shape)
    vt_ref[...] = project_t(w_ref[:, OFF_DA_V:OFF_ML_Q]).astype(BF16)
    for blk in range(2):
        src = OFF_ML_Q + blk * D_MODEL
        dst = (COL_ML_QK + blk) * D_MODEL
        proj_ref[:, dst:dst + D_MODEL] = project(w_ref[:, src:src + D_MODEL]).astype(BF16)
    w_if = w_ref[:, OFF_ML_IF:OFF_ML_IF + LANES]
    if_ref[...] = project(w_if)
    ift_ref[...] = project_t(w_if)
    for blk in range(3):
        dst = (COL_ML_O + blk) * D_MODEL
        proj_ref[:, dst:dst + D_MODEL] = project(
            wtail_ref[:, blk * D_MODEL:(blk + 1) * D_MODEL]).astype(BF16)


def _in_proj(x2, g, w_bf, w_tail, batch, seq, tm):
    tokens = batch * seq
    nsb = seq // tm

    def resident(arr):
        return pl.BlockSpec(arr.shape, lambda i: (0, 0), pipeline_mode=pl.Buffered(1))

    return pl.pallas_call(
        _in_proj_kernel,
        grid=(tokens // tm,),
        in_specs=[
            pl.BlockSpec((tm, D_MODEL), lambda i: (i, 0)),
            resident(g),
            resident(w_bf),
            resident(w_tail),
        ],
        out_specs=[
            pl.BlockSpec((tm, N_PROJ_BLOCKS * D_MODEL), lambda i: (i, 0)),
            pl.BlockSpec((None, D_MODEL, tm), lambda i: (i // nsb, 0, i % nsb)),
            pl.BlockSpec((tm, LANES), lambda i: (i, 0)),
            pl.BlockSpec((None, LANES, tm), lambda i: (i // nsb, 0, i % nsb)),
            pl.BlockSpec((SUBLANES, LANES), lambda i: (i, 0)),
        ],
        out_shape=[
            jax.ShapeDtypeStruct((tokens, N_PROJ_BLOCKS * D_MODEL), BF16),
            jax.ShapeDtypeStruct((batch, D_MODEL, seq), BF16),
            jax.ShapeDtypeStruct((tokens, LANES), F32),
            jax.ShapeDtypeStruct((batch, LANES, seq), F32),
            jax.ShapeDtypeStruct((tokens // tm * SUBLANES, LANES), F32),
        ],
        compiler_params=pltpu.CompilerParams(
            dimension_semantics=("parallel",), vmem_limit_bytes=VMEM_LIMIT),
    )(x2, g, w_bf, w_tail)


def _attn_kernel(slopes_ref, kmax_ref, lam_ref, q_ref, k_ref, vt_ref, g_ref, o_ref,
                 qq_scr, pos_scr, s_scr, p_scr, acc_scr, m_scr, *, tile, heads):
    batch_idx = pl.program_id(0)
    group = pl.program_id(1)
    qi = pl.program_id(2)
    kt_size = tile // 2
    width = 2 * tile
    slope2 = [slopes_ref[group * heads + hh] * LOG2E for hh in range(heads)]

    def head_cols(hh):
        return slice(hh * DA_V_DIM, (hh + 1) * DA_V_DIM)

    key_off = lax.broadcasted_iota(jnp.int32, pos_scr.shape, 0).astype(F32)
    lane = lax.broadcasted_iota(jnp.int32, pos_scr.shape, 1)
    pos_scr[...] = jnp.where(lane < 2, key_off, 0.0).astype(BF16)
    q_lane = lax.broadcasted_iota(jnp.int32, (1, width), 1)
    q_off = jnp.where(q_lane >= tile, q_lane - tile, q_lane).astype(F32)
    spread = jnp.zeros((1, 1), F32)
    for hh in range(heads):
        qt = q_ref[:, head_cols(hh)].astype(F32).T * LOG2E
        d_idx = lax.broadcasted_iota(jnp.int32, qt.shape, 0)
        q0 = jnp.where(d_idx < DA_HEAD_DIM, qt, 0.0)
        q1 = jnp.where(d_idx >= DA_HEAD_DIM, qt, 0.0)
        qq = jnp.concatenate([q0, q1], axis=1).astype(BF16)
        qq_scr[hh, 0:DA_V_DIM, :width] = qq
        slope_vec = jnp.full((DA_V_DIM, 2 * tile), slope2[hh], F32)
        slope_hi = slope_vec.astype(BF16).astype(F32)
        feat = lax.broadcasted_iota(jnp.int32, slope_vec.shape, 0)
        qq_scr[hh, DA_V_DIM:, :width] = jnp.where(
            feat == 0, slope_hi, jnp.where(feat == 1, slope_vec - slope_hi, 0.0)).astype(BF16)
        acc_scr[hh, :, :width] = jnp.zeros((acc_scr.shape[1], width), F32)
        qf = qq.astype(F32)
        q_norm = jnp.sqrt(jnp.sum(qf * qf, axis=0, keepdims=True))
        head = group * heads + hh
        k_max = jnp.where(q_lane < tile, kmax_ref[batch_idx, 2 * head],
                          kmax_ref[batch_idx, 2 * head + 1])
        qk_bound = q_norm * k_max * BOUND_SLACK + 1.0
        spread = jnp.maximum(spread, jnp.max(qk_bound, axis=1, keepdims=True))
        m_scr[hh] = qk_bound + slope2[hh] * q_off
    bounded_ok = jnp.max(spread) * 2.0 < MAX_SHIFT_SPREAD

    def lanes(blk):
        return slice(blk * kt_size, (blk + 1) * kt_size)

    n_blocks = width // kt_size
    upper_blocks = (1, 3)

    def key_rows(hh, t):
        start = pl.multiple_of(t * kt_size, kt_size)
        return jnp.concatenate([k_ref[pl.ds(start, kt_size), head_cols(hh)], pos_scr[...]], axis=1)

    def values_t(hh, t, n_tiles=1):
        start = pl.multiple_of(jnp.maximum(t, 0) * kt_size, kt_size)
        ones_rows = jnp.ones((ONES_ROWS, n_tiles * kt_size), BF16)
        return jnp.concatenate(
            [vt_ref[head_cols(hh), pl.ds(start, n_tiles * kt_size)], ones_rows], axis=0)

    kk = lax.broadcasted_iota(jnp.int32, (kt_size, width), 0)
    qpos = lax.broadcasted_iota(jnp.int32, (kt_size, width), 1)
    qpos = jnp.where(qpos >= tile, qpos - tile, qpos)
    tri = (lax.broadcasted_iota(jnp.int32, (kt_size, kt_size), 0)
           <= lax.broadcasted_iota(jnp.int32, (kt_size, kt_size), 1))

    lam = lam_ref[...]
    lam_full = (jnp.exp(jnp.sum(lam[0:1] * lam[1:2], axis=1, keepdims=True))
                - jnp.exp(jnp.sum(lam[2:3] * lam[3:4], axis=1, keepdims=True)) + LAM_INIT)

    def finish(hh, acc):
        l = acc[DA_V_DIM:DA_V_DIM + 1, :]
        acc = acc[:DA_V_DIM, :]
        o = acc[:, :tile] / l[:, :tile] - lam_full * (acc[:, tile:] / l[:, tile:])
        o = o * lax.rsqrt(jnp.mean(o * o, axis=0, keepdims=True) + EPS) * (1.0 - LAM_INIT)
        o_ref[:, head_cols(hh)] = (o.T * g_ref[:, head_cols(hh)]).astype(o_ref.dtype)

    def interleave(stage_iters):
        pending = list(stage_iters)
        while pending:
            for it in list(pending):
                if next(it, StopIteration) is StopIteration:
                    pending.remove(it)

    @pl.when(bounded_ok)
    def _():
        def pair_stages(hh, i, diagonal):
            shift = m_scr[hh]
            for r in range(2):
                t = 2 * i + r
                c = slope2[hh] * (t * kt_size - qi * tile).astype(F32)
                kt = key_rows(hh, t)
                if diagonal and r == 1:
                    for blk in range(n_blocks):
                        if blk in upper_blocks:
                            s = jnp.dot(kt, qq_scr[hh, :, lanes(blk)], preferred_element_type=F32)
                            s = jnp.where(tri, s, NEG_BIG)
                            p = jnp.exp2(s + (c - shift[:, lanes(blk)]))
                            p_scr[2 * hh + r, :, lanes(blk)] = p.astype(BF16)
                        else:
                            p_scr[2 * hh + r, :, lanes(blk)] = jnp.zeros((kt_size, kt_size), BF16)
                else:
                    s = jnp.dot(kt, qq_scr[hh, :, :width], preferred_element_type=F32)
                    if diagonal:
                        s = jnp.where(kk <= qpos, s, NEG_BIG)
                    p_scr[2 * hh + r, :, :width] = jnp.exp2(s + (c - shift)).astype(BF16)
                yield
            p_pair = p_scr[2 * hh:2 * hh + 2, :, :width].reshape(2 * kt_size, width)
            acc_scr[hh, :, :width] = acc_scr[hh, :, :width] + jnp.dot(
                values_t(hh, 2 * i, 2), p_pair, preferred_element_type=F32)

        def body(i, carry):
            interleave([pair_stages(hh, i, False) for hh in range(heads)])
            return carry

        lax.fori_loop(0, qi, body, 0)
        interleave([pair_stages(hh, qi, True) for hh in range(heads)])
        for hh in range(heads):
            finish(hh, acc_scr[hh, :, :width])

    @pl.when(jnp.logical_not(bounded_ok))
    def _():
        for hh in range(heads):
            m_scr[hh] = jnp.full(m_scr.shape[1:], NEG_BIG, F32)
            p_scr[2 * hh + 1, :, :width] = jnp.zeros((kt_size, width), BF16)

        def scores(hh, t, slot, blocks=None):
            kt = key_rows(hh, t)
            if blocks is None:
                s_scr[2 * hh + slot, :, :width] = jnp.dot(
                    kt, qq_scr[hh, :, :width], preferred_element_type=F32)
            else:
                for blk in blocks:
                    s_scr[2 * hh + slot, :, lanes(blk)] = jnp.dot(
                        kt, qq_scr[hh, :, lanes(blk)], preferred_element_type=F32)

        def weighted_values(hh, t, slot):
            return jnp.dot(values_t(hh, t), p_scr[2 * hh + slot, :, :width],
                           preferred_element_type=F32)

        def step(hh, t, slot, mask=None, prefetch_blocks=None):
            scores(hh, t + 1, 1 - slot, prefetch_blocks)
            c = slope2[hh] * (t * kt_size - qi * tile).astype(F32)
            s = s_scr[2 * hh + slot, :, :width]
            if mask is not None:
                s = jnp.where(mask, s, NEG_BIG)
            m_old = m_scr[hh]
            m_new = jnp.maximum(m_old, jnp.max(s, axis=0, keepdims=True) + c)
            p = jnp.exp2(s - (m_new - c))
            alpha = jnp.exp2(m_old - m_new)
            p_scr[2 * hh + slot, :, :width] = p.astype(BF16)
            m_scr[hh] = m_new
            acc_scr[hh, :, :width] = alpha * (acc_scr[hh, :, :width]
                                              + weighted_values(hh, t - 1, 1 - slot))

        for hh in range(heads):
            scores(hh, 0, 0)

        def body(i, carry):
            for slot in range(2):
                for hh in range(heads):
                    step(hh, 2 * i + slot, slot)
            return carry

        lax.fori_loop(0, qi, body, 0)

        for hh in range(heads):
            step(hh, 2 * qi, 0, mask=kk <= qpos, prefetch_blocks=upper_blocks)

        def last_tile(hh):
            t = 2 * qi + 1
            c = slope2[hh] * kt_size
            acc = acc_scr[hh, :, :width] + weighted_values(hh, t - 1, 0)
            vt = values_t(hh, t)
            parts = []
            for blk in range(n_blocks):
                part = acc[:, lanes(blk)]
                if blk in upper_blocks:
                    s = jnp.where(tri, s_scr[2 * hh + 1, :, lanes(blk)], NEG_BIG)
                    m_old = m_scr[hh, :, lanes(blk)]
                    m_new = jnp.maximum(m_old, jnp.max(s, axis=0, keepdims=True) + c)
                    p = jnp.exp2(s - (m_new - c)).astype(BF16)
                    part = (jnp.exp2(m_old - m_new) * part
                            + jnp.dot(vt, p, preferred_element_type=F32))
                parts.append(part)
            return jnp.concatenate(parts, axis=1)

        for hh in range(heads):
            finish(hh, last_tile(hh))


def _attention(slopes, kmax, lam, proj3, vt, g, batch, seq, tile, heads):
    kern = functools.partial(_attn_kernel, tile=tile, heads=heads)
    width = heads * DA_V_DIM
    groups = DA_HEADS // heads
    pitch = 2 * tile + LANES
    return pl.pallas_call(
        kern,
        grid=(batch, groups, seq // tile),
        in_specs=[
            pl.BlockSpec(memory_space=pltpu.SMEM),
            pl.BlockSpec(memory_space=pltpu.SMEM),
            pl.BlockSpec((4, DA_HEAD_DIM), lambda b, h, i: (0, 0)),
            pl.BlockSpec((None, tile, width), lambda b, h, i: (b, i, COL_DA_Q * groups + h)),
            pl.BlockSpec((None, seq, width), lambda b, h, i: (b, 0, COL_DA_K * groups + h)),
            pl.BlockSpec((None, width, seq), lambda b, h, i: (b, h, 0)),
            pl.BlockSpec((1, width), lambda b, h, i: (0, h)),
        ],
        out_specs=pl.BlockSpec((None, tile, width), lambda b, h, i: (b, i, h)),
        out_shape=jax.ShapeDtypeStruct((batch, seq, D_MODEL), BF16),
        scratch_shapes=[
            pltpu.VMEM((heads, 2 * DA_V_DIM, pitch), BF16),
            pltpu.VMEM((tile // 2, DA_V_DIM), BF16),
            pltpu.VMEM((2 * heads, tile // 2, pitch), F32),
            pltpu.VMEM((2 * heads, tile // 2, pitch), BF16),
            pltpu.VMEM((heads, DA_V_DIM + ONES_ROWS, pitch), F32),
            pltpu.VMEM((heads, 1, 2 * tile), F32),
        ],
        compiler_params=pltpu.CompilerParams(
            dimension_semantics=("parallel", "parallel", "arbitrary"),
            vmem_limit_bytes=VMEM_LIMIT),
    )(slopes, kmax, lam, proj3, proj3, vt, g)


def _rms(x, g):
    return x * lax.rsqrt(jnp.mean(x * x, axis=-1, keepdims=True) + EPS) * g


def _mlstm_chunk(r0, chunk, qk_ref, v_ref, og_ref, grow_ref, gcol_ref, cw_ref, cb_ref, brow_ref,
                 bcol_ref, ng_ref, c_scr, n_scr, m_scr, ext_scr, mout_scr):
    pad = SUBLANES
    rows = slice(r0, r0 + chunk)

    ext_scr[pad:pad + chunk, :] = qk_ref[rows, :].astype(F32)
    conv = cb_ref[...] + cw_ref[CONV_WIDTH - 1:CONV_WIDTH, :] * ext_scr[pad:pad + chunk, :]
    for tap in range(1, CONV_WIDTH):
        conv = conv + (cw_ref[CONV_WIDTH - 1 - tap:CONV_WIDTH - tap, :]
                       * ext_scr[pad - tap:pad - tap + chunk, :])
    ext_scr[0:pad, :] = ext_scr[chunk:chunk + pad, :]
    qk = conv * _sigmoid(conv)

    g_rows = grow_ref[:, rows] + brow_ref[...]
    g_cols = gcol_ref[rows, :] + bcol_ref[...]
    r_idx = lax.broadcasted_iota(jnp.int32, (chunk, chunk), 0)
    c_idx = lax.broadcasted_iota(jnp.int32, (chunk, chunk), 1)
    causal = r_idx >= c_idx
    tril = jnp.where(causal, 1.0, 0.0).astype(F32)
    triu = jnp.where(r_idx <= c_idx, 1.0, 0.0).astype(F32)
    b_rows = jnp.dot(_log_sigmoid(g_rows), triu, preferred_element_type=F32,
                     precision=lax.Precision.HIGHEST)
    b_cols = jnp.dot(tril, _log_sigmoid(g_cols), preferred_element_type=F32,
                     precision=lax.Precision.HIGHEST)

    q_scale = ML_QK_DIM ** -0.5
    for hd in range(ML_HEADS):
        yield
        qf = qk[:, hd * ML_QK_DIM:(hd + 1) * ML_QK_DIM] * q_scale
        kf = qk[:, ML_QK_WIDTH + hd * ML_QK_DIM:ML_QK_WIDTH + (hd + 1) * ML_QK_DIM]
        qb = qf.astype(BF16)
        sl = slice(hd * ML_V_DIM, (hd + 1) * ML_V_DIM)
        vb = v_ref[rows, sl]
        fcol = ML_HEADS + hd
        bt = b_cols[:, fcol:fcol + 1]
        bs = b_rows[fcol:fcol + 1, :]
        i_row = g_rows[hd:hd + 1, :]
        i_col = g_cols[:, hd:hd + 1]
        m_prev = m_scr[hd]
        c_prev = c_scr[hd]
        n_prev = n_scr[hd]

        logd = jnp.where(causal, bt - bs + i_row, NEG_BIG)
        inter = bt + m_prev
        m_t = jnp.maximum(inter, jnp.max(logd, axis=1, keepdims=True))
        dmat = jnp.exp(logd - m_t)
        sc = lax.dot_general(qb, kf.astype(BF16), NT_DIMS, preferred_element_type=F32) * dmat
        w_inter = jnp.exp(inter - m_t)
        num = (w_inter * jnp.dot(qb, c_prev.astype(BF16), preferred_element_type=F32)
               + jnp.dot(sc.astype(BF16), vb, preferred_element_type=F32))
        den = (w_inter * jnp.sum(qf * n_prev, axis=1, keepdims=True)
               + jnp.sum(sc, axis=1, keepdims=True))
        hh = num / jnp.maximum(jnp.abs(den), jnp.exp(-m_t))

        g_last = bt[chunk - 1:chunk, :]
        log_w = g_last - bt + i_col
        m_new = jnp.maximum(g_last + m_prev, jnp.max(log_w, axis=0, keepdims=True))
        kw = kf * jnp.exp(log_w - m_new)
        decay = jnp.exp(g_last + m_prev - m_new)
        c_scr[hd] = decay * c_prev + lax.dot_general(
            kw.astype(BF16), vb, TN_DIMS, preferred_element_type=F32)
        n_scr[hd] = decay * n_prev + jnp.sum(kw, axis=0, keepdims=True)
        m_scr[hd] = m_new

        hn = hh * lax.rsqrt(jnp.mean(hh * hh, axis=1, keepdims=True) + EPS) * ng_ref[:, sl]
        mout_scr[rows, sl] = (_sigmoid(og_ref[rows, sl].astype(F32)) * hn).astype(mout_scr.dtype)


def _mlstm_tail_kernel(qk_ref, v_ref, og_ref, grow_ref, gcol_ref, cw_ref, cb_ref, brow_ref,
                       bcol_ref, ng_ref,
                       x_ref, a_ref, ga_ref, gm_ref, bm_ref, wa_ref, wm_ref, wo_ref,
                       gmlp_ref, w1_ref, w2_ref, gfin_ref,
                       o_ref, c_scr, n_scr, m_scr, ext_scr, mout_scr, *, chunk, tiles_per_seq):
    i = pl.program_id(0)

    @pl.when(i == 0)
    def _():
        mout_scr[...] = jnp.zeros(mout_scr.shape, mout_scr.dtype)

    @pl.when(i % tiles_per_seq == 0)
    def _():
        c_scr[...] = jnp.zeros(c_scr.shape, F32)
        n_scr[...] = jnp.zeros(n_scr.shape, F32)
        m_scr[...] = jnp.zeros(m_scr.shape, F32)
        ext_scr[0:SUBLANES, :] = jnp.zeros((SUBLANES, ext_scr.shape[1]), F32)

    def tail_stages():
        ya = jnp.dot(a_ref[...], wa_ref[...], preferred_element_type=F32)
        ym = jnp.dot(mout_scr[...], wm_ref[...], preferred_element_type=F32)
        gate_a = _sigmoid(ga_ref[...].astype(F32) + bm_ref[:, :D_MODEL])
        gate_m = _sigmoid(gm_ref[...].astype(F32) + bm_ref[:, D_MODEL:])
        merged = (gate_a * ya + gate_m * ym).astype(BF16)
        yield
        x1 = x_ref[...] + jnp.dot(merged, wo_ref[...], preferred_element_type=F32)
        hm = _rms(x1, gmlp_ref[...]).astype(BF16)
        acc = x1
        for c in range(D_FF // FF_BLOCK):
            yield
            cols = slice(c * FF_BLOCK, (c + 1) * FF_BLOCK)
            u = jnp.maximum(jnp.dot(hm, w1_ref[:, cols], preferred_element_type=F32), 0.0)
            acc = acc + jnp.dot((u * u).astype(BF16), w2_ref[cols, :], preferred_element_type=F32)
        o_ref[...] = _rms(acc, gfin_ref[...])

    def mlstm_stages():
        for r0 in range(0, mout_scr.shape[0], chunk):
            yield from _mlstm_chunk(r0, chunk, qk_ref, v_ref, og_ref, grow_ref, gcol_ref, cw_ref,
                                    cb_ref, brow_ref, bcol_ref, ng_ref, c_scr, n_scr, m_scr,
                                    ext_scr, mout_scr)

    pending = [tail_stages(), mlstm_stages()]
    while pending:
        for stage in list(pending):
            if next(stage, StopIteration) is StopIteration:
                pending.remove(stage)


def _mlstm_tail(proj3, proj, g_rows, g_cols3, conv_w, conv_b, b_row, b_col, norm_g,
                x2, a2, b_merge, wa, wm, wo, g_mlp, w1, w2, g_fin, batch, seq, tm, chunk):
    tokens = batch * seq
    n_tiles = tokens // tm
    tps = seq // tm
    kern = functools.partial(_mlstm_tail_kernel, chunk=chunk, tiles_per_seq=tps)

    def cur(i):
        return jnp.minimum(i, n_tiles - 1)

    def prev(i):
        return jnp.maximum(i - 1, 0)

    def resident(arr):
        return pl.BlockSpec(arr.shape, lambda i: (0,) * arr.ndim, pipeline_mode=pl.Buffered(1))

    def seq_block(col):
        return pl.BlockSpec((None, tm, D_MODEL), lambda i: (cur(i) // tps, cur(i) % tps, col))

    return pl.pallas_call(
        kern,
        grid=(n_tiles + 1,),
        in_specs=[
            seq_block(COL_ML_QK),
            seq_block(COL_ML_V),
            seq_block(COL_ML_O),
            pl.BlockSpec((None, 2 * ML_HEADS, tm), lambda i: (cur(i) // tps, 0, cur(i) % tps)),
            pl.BlockSpec((None, tm, LANES), lambda i: (cur(i) // tps, cur(i) % tps, 0)),
            resident(conv_w), resident(conv_b), resident(b_row), resident(b_col), resident(norm_g),
            pl.BlockSpec((tm, D_MODEL), lambda i: (prev(i), 0)),
            pl.BlockSpec((tm, D_MODEL), lambda i: (prev(i), 0)),
            pl.BlockSpec((tm, D_MODEL), lambda i: (prev(i), COL_MG_A)),
            pl.BlockSpec((tm, D_MODEL), lambda i: (prev(i), COL_MG_M)),
            resident(b_merge), resident(wa), resident(wm), resident(wo), resident(g_mlp),
            resident(w1), resident(w2), resident(g_fin),
        ],
        out_specs=pl.BlockSpec((tm, D_MODEL), lambda i: (prev(i), 0)),
        out_shape=jax.ShapeDtypeStruct((tokens, D_MODEL), F32),
        scratch_shapes=[
            pltpu.VMEM((ML_HEADS, ML_QK_DIM, ML_V_DIM), F32),
            pltpu.VMEM((ML_HEADS, 1, ML_QK_DIM), F32),
            pltpu.VMEM((ML_HEADS, 1, 1), F32),
            pltpu.VMEM((chunk + 2 * SUBLANES, D_MODEL), F32),
            pltpu.VMEM((tm, D_MODEL), BF16),
        ],
        compiler_params=pltpu.CompilerParams(
            dimension_semantics=("arbitrary",), vmem_limit_bytes=VMEM_LIMIT),
    )(proj3, proj3, proj3, g_rows, g_cols3, conv_w, conv_b, b_row, b_col, norm_g,
      x2, a2, proj, proj, b_merge, wa, wm, wo, g_mlp, w1, w2, g_fin)


def kernel(x, norm_mix_g, w_in, b_gates, conv_w, conv_b, lam, da_norm_g, ml_norm_g, b_merge,
           w_branch_a, w_branch_m, w_out, norm_mlp_g, w_ff1, w_ff2, norm_final_g):
    batch, seq, _ = x.shape
    tokens = batch * seq
    x2 = x.reshape(tokens, D_MODEL)

    assert w_in.shape == (1, D_MODEL, D_IN)
    w_bf = w_in[0].astype(BF16)
    w_tail = w_bf[:, OFF_ML_O:]

    tm_proj = min(512, seq)
    proj, vt, ifg, ifg_t, k_sq = _in_proj(x2, norm_mix_g, w_bf, w_tail, batch, seq, tm_proj)
    kmax = jnp.sqrt(jnp.max(k_sq.reshape(batch, -1, LANES), axis=1))
    proj3 = proj.reshape(batch, seq, N_PROJ_BLOCKS * D_MODEL)

    slopes = 2.0 ** (-8.0 * jnp.arange(1, DA_HEADS + 1, dtype=F32) / DA_HEADS)
    attn_tile = min(512, seq)
    a_out = _attention(slopes, kmax, lam[0], proj3, vt, da_norm_g, batch, seq, attn_tile, 4)

    ifg3 = ifg.reshape(batch, seq, LANES)
    b_row = b_gates[0].reshape(2 * ML_HEADS, 1)
    b_col = jnp.pad(b_gates, ((0, 0), (0, LANES - 2 * ML_HEADS)))
    chunk = min(256, seq)
    out = _mlstm_tail(proj3, proj, ifg_t, ifg3, conv_w[0], conv_b, b_row, b_col, ml_norm_g,
                      x2, a_out.reshape(tokens, D_MODEL), b_merge,
                      w_branch_a[0].astype(BF16), w_branch_m[0].astype(BF16), w_out[0].astype(BF16),
                      norm_mlp_g, w_ff1[0].astype(BF16), w_ff2[0].astype(BF16),
                      norm_final_g.reshape(1, D_MODEL), batch, seq, min(512, seq), chunk)
    return out.reshape(batch, seq, D_MODEL)
```

```python
import functools
import math

import jax
import jax.numpy as jnp
from jax import lax
from jax.experimental import pallas as pl
from jax.experimental.pallas import tpu as pltpu

F32 = jnp.float32
BF16 = jnp.bfloat16

D_MODEL = 1024
DA_HEADS = 8
DA_HEAD_DIM = 64
DA_V_DIM = 2 * DA_HEAD_DIM
ML_HEADS = 4
ML_V_DIM = D_MODEL // ML_HEADS
ML_QK_DIM = ML_V_DIM // 2
ML_QK_WIDTH = ML_HEADS * ML_QK_DIM
CONV_WIDTH = 4
D_FF = 4 * D_MODEL
EPS = 1e-6
LAM_INIT = 0.8 - 0.6 * math.exp(-0.3 * 0)
NEG_BIG = -1e30
LOG2E = math.log2(math.e)
ONES_ROWS = 16
FF_BLOCK = 512
BOUND_SLACK = 1.02
MAX_SHIFT_SPREAD = 100.0

LANES = 128
SUBLANES = 8
VMEM_LIMIT = 60 * 1024 * 1024

COL_DA_Q, COL_DA_K, COL_ML_QK, COL_ML_V, COL_ML_O, COL_MG_A, COL_MG_M = range(7)
N_PROJ_BLOCKS = 7

OFF_DA_Q = 0
OFF_DA_K = OFF_DA_Q + D_MODEL
OFF_DA_V = OFF_DA_K + D_MODEL
OFF_ML_Q = OFF_DA_V + D_MODEL
OFF_ML_V = OFF_ML_Q + 2 * ML_QK_WIDTH
OFF_ML_IF = OFF_ML_V + D_MODEL
OFF_ML_O = OFF_ML_IF + 2 * ML_HEADS
D_IN = OFF_ML_O + 3 * D_MODEL

NT_DIMS = (((1,), (1,)), ((), ()))
TN_DIMS = (((0,), (0,)), ((), ()))
TT_DIMS = (((0,), (1,)), ((), ()))


def _sigmoid(x):
    return 1.0 / (1.0 + jnp.exp(-x))


def _log_sigmoid(x):
    return jnp.minimum(x, 0.0) - jnp.log(1.0 + jnp.exp(-jnp.abs(x)))


def _in_proj_kernel(x_ref, g_ref, w_ref, wtail_ref, proj_ref, vt_ref, if_ref, ift_ref, kn_ref):
    x = x_ref[...]
    hb = (x * lax.rsqrt(jnp.mean(x * x, axis=-1, keepdims=True) + EPS) * g_ref[...]).astype(BF16)

    def project(w_cols):
        return jnp.dot(hb, w_cols, preferred_element_type=F32)

    def project_t(w_cols):
        return lax.dot_general(w_cols, hb, TT_DIMS, preferred_element_type=F32)

    proj_ref[:, 0:D_MODEL] = (project(w_ref[:, OFF_DA_Q:OFF_DA_K]) * DA_HEAD_DIM ** -0.5).astype(BF16)
    k = project(w_ref[:, OFF_DA_K:OFF_DA_V])
    proj_ref[:, D_MODEL:2 * D_MODEL] = k.astype(BF16)
    r_idx = lax.broadcasted_iota(jnp.int32, (D_MODEL, LANES), 0)
    c_idx = lax.broadcasted_iota(jnp.int32, (D_MODEL, LANES), 1)
    select = jnp.where(r_idx // DA_HEAD_DIM == c_idx, 1.0, 0.0).astype(BF16)
    k_sq = jnp.dot((k * k).astype(BF16), select, preferred_element_type=F32)
    kn_ref[...] = jnp.broadcast_to(jnp.max(k_sq, axis=0, keepdims=True), kn_ref.shape)
    vt_ref[...] = project_t(w_ref[:, OFF_DA_V:OFF_ML_Q]).astype(BF16)
    for blk in range(2):
        src = OFF_ML_Q + blk * D_MODEL
        dst = (COL_ML_QK + blk) * D_MODEL
        proj_ref[:, dst:dst + D_MODEL] = project(w_ref[:, src:src + D_MODEL]).astype(BF16)
    w_if = w_ref[:, OFF_ML_IF:OFF_ML_IF + LANES]
    if_ref[...] = project(w_if)
    ift_ref[...] = project_t(w_if)
    for blk in range(3):
        dst = (COL_ML_O + blk) * D_MODEL
        proj_ref[:, dst:dst + D_MODEL] = project(
            wtail_ref[:, blk * D_MODEL:(blk + 1) * D_MODEL]).astype(BF16)


def _in_proj(x2, g, w_bf, w_tail, batch, seq, tm):
    tokens = batch * seq
    nsb = seq // tm

    def resident(arr):
        return pl.BlockSpec(arr.shape, lambda i: (0, 0), pipeline_mode=pl.Buffered(1))

    return pl.pallas_call(
        _in_proj_kernel,
        grid=(tokens // tm,),
        in_specs=[
            pl.BlockSpec((tm, D_MODEL), lambda i: (i, 0)),
            resident(g),
            resident(w_bf),
            resident(w_tail),
        ],
        out_specs=[
            pl.BlockSpec((tm, N_PROJ_BLOCKS * D_MODEL), lambda i: (i, 0)),
            pl.BlockSpec((None, D_MODEL, tm), lambda i: (i // nsb, 0, i % nsb)),
            pl.BlockSpec((tm, LANES), lambda i: (i, 0)),
            pl.BlockSpec((None, LANES, tm), lambda i: (i // nsb, 0, i % nsb)),
            pl.BlockSpec((SUBLANES, LANES), lambda i: (i, 0)),
        ],
        out_shape=[
            jax.ShapeDtypeStruct((tokens, N_PROJ_BLOCKS * D_MODEL), BF16),
            jax.ShapeDtypeStruct((batch, D_MODEL, seq), BF16),
            jax.ShapeDtypeStruct((tokens, LANES), F32),
            jax.ShapeDtypeStruct((batch, LANES, seq), F32),
            jax.ShapeDtypeStruct((tokens // tm * SUBLANES, LANES), F32),
        ],
        compiler_params=pltpu.CompilerParams(
            dimension_semantics=("parallel",), vmem_limit_bytes=VMEM_LIMIT),
    )(x2, g, w_bf, w_tail)


def _attn_kernel(slopes_ref, kmax_ref, lam_ref, q_ref, k_ref, vt_ref, g_ref, o_ref,
                 qq_scr, pos_scr, s_scr, p_scr, acc_scr, m_scr, l_scr, *, tile, heads):
    batch_idx = pl.program_id(0)
    group = pl.program_id(1)
    qi = pl.program_id(2)
    kt_size = tile // 2
    width = 2 * tile
    slope2 = [slopes_ref[group * heads + hh] * LOG2E for hh in range(heads)]

    def head_cols(hh):
        return slice(hh * DA_V_DIM, (hh + 1) * DA_V_DIM)

    key_off = lax.broadcasted_iota(jnp.int32, pos_scr.shape, 0).astype(F32)
    lane = lax.broadcasted_iota(jnp.int32, pos_scr.shape, 1)
    pos_scr[...] = jnp.where(lane < 2, key_off, 0.0).astype(BF16)
    q_lane = lax.broadcasted_iota(jnp.int32, (1, width), 1)
    q_off = jnp.where(q_lane >= tile, q_lane - tile, q_lane).astype(F32)
    spread = jnp.zeros((1, 1), F32)
    for hh in range(heads):
        qt = q_ref[:, head_cols(hh)].astype(F32).T * LOG2E
        d_idx = lax.broadcasted_iota(jnp.int32, qt.shape, 0)
        q0 = jnp.where(d_idx < DA_HEAD_DIM, qt, 0.0)
        q1 = jnp.where(d_idx >= DA_HEAD_DIM, qt, 0.0)
        qq = jnp.concatenate([q0, q1], axis=1).astype(BF16)
        qq_scr[hh, 0:DA_V_DIM, :width] = qq
        slope_vec = jnp.full((ONES_ROWS, width), slope2[hh], F32)
        slope_hi = slope_vec.astype(BF16).astype(F32)
        feat = lax.broadcasted_iota(jnp.int32, slope_vec.shape, 0)
        qq_scr[hh, DA_V_DIM:DA_V_DIM + ONES_ROWS, :width] = jnp.where(
            feat == 0, slope_hi, jnp.where(feat == 1, slope_vec - slope_hi, 0.0)).astype(BF16)
        qq_scr[hh, DA_V_DIM + ONES_ROWS:, :width] = jnp.zeros((DA_V_DIM - ONES_ROWS, width), BF16)
        acc_scr[hh, :, :width] = jnp.zeros((acc_scr.shape[1], width), F32)
        qf = qq.astype(F32)
        q_norm = jnp.sqrt(jnp.sum(qf * qf, axis=0, keepdims=True))
        head = group * heads + hh
        k_max = jnp.where(q_lane < tile, kmax_ref[batch_idx, 2 * head],
                          kmax_ref[batch_idx, 2 * head + 1])
        qk_bound = q_norm * k_max * BOUND_SLACK + 1.0
        spread = jnp.maximum(spread, jnp.max(qk_bound, axis=1, keepdims=True))
        m_scr[hh] = qk_bound + slope2[hh] * q_off
    bounded_ok = jnp.max(spread) * 2.0 < MAX_SHIFT_SPREAD

    def lanes(blk):
        return slice(blk * kt_size, (blk + 1) * kt_size)

    n_blocks = width // kt_size
    upper_blocks = (1, 3)

    def key_rows(hh, t):
        start = pl.multiple_of(t * kt_size, kt_size)
        return jnp.concatenate([k_ref[pl.ds(start, kt_size), head_cols(hh)], pos_scr[...]], axis=1)

    def values_t(hh, t, n_tiles=1, with_ones=True):
        start = pl.multiple_of(jnp.maximum(t, 0) * kt_size, kt_size)
        vt = vt_ref[head_cols(hh), pl.ds(start, n_tiles * kt_size)]
        if not with_ones:
            return vt
        return jnp.concatenate([vt, jnp.ones((ONES_ROWS, n_tiles * kt_size), BF16)], axis=0)

    kk = lax.broadcasted_iota(jnp.int32, (kt_size, width), 0)
    qpos = lax.broadcasted_iota(jnp.int32, (kt_size, width), 1)
    qpos = jnp.where(qpos >= tile, qpos - tile, qpos)
    tri = (lax.broadcasted_iota(jnp.int32, (kt_size, kt_size), 0)
           <= lax.broadcasted_iota(jnp.int32, (kt_size, kt_size), 1))

    lam = lam_ref[...]
    lam_full = (jnp.exp(jnp.sum(lam[0:1] * lam[1:2], axis=1, keepdims=True))
                - jnp.exp(jnp.sum(lam[2:3] * lam[3:4], axis=1, keepdims=True)) + LAM_INIT)

    def finish(hh, acc, l):
        o = acc[:, :tile] / l[:, :tile] - lam_full * (acc[:, tile:] / l[:, tile:])
        o = o * lax.rsqrt(jnp.mean(o * o, axis=0, keepdims=True) + EPS) * (1.0 - LAM_INIT)
        o_ref[:, head_cols(hh)] = (o.T * g_ref[:, head_cols(hh)]).astype(o_ref.dtype)

    def interleave(stage_iters):
        pending = list(stage_iters)
        while pending:
            for it in list(pending):
                if next(it, StopIteration) is StopIteration:
                    pending.remove(it)

    @pl.when(bounded_ok)
    def _():
        for hh in range(heads):
            l_scr[hh] = jnp.zeros(l_scr.shape[1:], F32)

        def pair_stages(hh, i, diagonal):
            shift = m_scr[hh]
            for r in range(2):
                t = 2 * i + r
                c = slope2[hh] * (t * kt_size - qi * tile).astype(F32)
                kt = key_rows(hh, t)
                if diagonal and r == 1:
                    for blk in range(n_blocks):
                        if blk in upper_blocks:
                            s = jnp.dot(kt, qq_scr[hh, :, lanes(blk)], preferred_element_type=F32)
                            s = jnp.where(tri, s, NEG_BIG)
                            p = jnp.exp2(s + (c - shift[:, lanes(blk)]))
                            l_scr[hh, :, lanes(blk)] += jnp.sum(p, axis=0, keepdims=True)
                            p_scr[2 * hh + r, :, lanes(blk)] = p.astype(BF16)
                        else:
                            p_scr[2 * hh + r, :, lanes(blk)] = jnp.zeros((kt_size, kt_size), BF16)
                else:
                    s = jnp.dot(kt, qq_scr[hh, :, :width], preferred_element_type=F32)
                    if diagonal:
                        s = jnp.where(kk <= qpos, s, NEG_BIG)
                    p = jnp.exp2(s + (c - shift))
                    l_scr[hh] += jnp.sum(p, axis=0, keepdims=True)
                    p_scr[2 * hh + r, :, :width] = p.astype(BF16)
                yield
            p_pair = p_scr[2 * hh:2 * hh + 2, :, :width].reshape(2 * kt_size, width)
            acc_scr[hh, :DA_V_DIM, :width] += jnp.dot(
                values_t(hh, 2 * i, 2, with_ones=False), p_pair, preferred_element_type=F32)

        def body(i, carry):
            interleave([pair_stages(hh, i, False) for hh in range(heads)])
            return carry

        lax.fori_loop(0, qi, body, 0)
        interleave([pair_stages(hh, qi, True) for hh in range(heads)])
        for hh in range(heads):
            finish(hh, acc_scr[hh, :DA_V_DIM, :width], l_scr[hh])

    @pl.when(jnp.logical_not(bounded_ok))
    def _():
        for hh in range(heads):
            m_scr[hh] = jnp.full(m_scr.shape[1:], NEG_BIG, F32)
            p_scr[2 * hh + 1, :, :width] = jnp.zeros((kt_size, width), BF16)

        def scores(hh, t, slot, blocks=None):
            kt = key_rows(hh, t)
            if blocks is None:
                s_scr[2 * hh + slot, :, :width] = jnp.dot(
                    kt, qq_scr[hh, :, :width], preferred_element_type=F32)
            else:
                for blk in blocks:
                    s_scr[2 * hh + slot, :, lanes(blk)] = jnp.dot(
                        kt, qq_scr[hh, :, lanes(blk)], preferred_element_type=F32)

        def weighted_values(hh, t, slot):
            return jnp.dot(values_t(hh, t), p_scr[2 * hh + slot, :, :width],
                           preferred_element_type=F32)

        def step(hh, t, slot, mask=None, prefetch_blocks=None):
            scores(hh, t + 1, 1 - slot, prefetch_blocks)
            c = slope2[hh] * (t * kt_size - qi * tile).astype(F32)
            s = s_scr[2 * hh + slot, :, :width]
            if mask is not None:
                s = jnp.where(mask, s, NEG_BIG)
            m_old = m_scr[hh]
            m_new = jnp.maximum(m_old, jnp.max(s, axis=0, keepdims=True) + c)
            p = jnp.exp2(s - (m_new - c))
            alpha = jnp.exp2(m_old - m_new)
            p_scr[2 * hh + slot, :, :width] = p.astype(BF16)
            m_scr[hh] = m_new
            acc_scr[hh, :, :width] = alpha * (acc_scr[hh, :, :width]
                                              + weighted_values(hh, t - 1, 1 - slot))

        for hh in range(heads):
            scores(hh, 0, 0)

        def body(i, carry):
            for slot in range(2):
                for hh in range(heads):
                    step(hh, 2 * i + slot, slot)
            return carry

        lax.fori_loop(0, qi, body, 0)

        for hh in range(heads):
            step(hh, 2 * qi, 0, mask=kk <= qpos, prefetch_blocks=upper_blocks)

        def last_tile(hh):
            t = 2 * qi + 1
            c = slope2[hh] * kt_size
            acc = acc_scr[hh, :, :width] + weighted_values(hh, t - 1, 0)
            vt = values_t(hh, t)
            parts = []
            for blk in range(n_blocks):
                part = acc[:, lanes(blk)]
                if blk in upper_blocks:
                    s = jnp.where(tri, s_scr[2 * hh + 1, :, lanes(blk)], NEG_BIG)
                    m_old = m_scr[hh, :, lanes(blk)]
                    m_new = jnp.maximum(m_old, jnp.max(s, axis=0, keepdims=True) + c)
                    p = jnp.exp2(s - (m_new - c)).astype(BF16)
                    part = (jnp.exp2(m_old - m_new) * part
                            + jnp.dot(vt, p, preferred_element_type=F32))
                parts.append(part)
            return jnp.concatenate(parts, axis=1)

        for hh in range(heads):
            acc = last_tile(hh)
            finish(hh, acc[:DA_V_DIM, :], acc[DA_V_DIM:DA_V_DIM + 1, :])


def _attention(slopes, kmax, lam, proj3, vt, g, batch, seq, tile, heads):
    kern = functools.partial(_attn_kernel, tile=tile, heads=heads)
    width = heads * DA_V_DIM
    groups = DA_HEADS // heads
    pitch = 2 * tile + LANES
    return pl.pallas_call(
        kern,
        grid=(batch, groups, seq // tile),
        in_specs=[
            pl.BlockSpec(memory_space=pltpu.SMEM),
            pl.BlockSpec(memory_space=pltpu.SMEM),
            pl.BlockSpec((4, DA_HEAD_DIM), lambda b, h, i: (0, 0)),
            pl.BlockSpec((None, tile, width), lambda b, h, i: (b, i, COL_DA_Q * groups + h)),
            pl.BlockSpec((None, seq, width), lambda b, h, i: (b, 0, COL_DA_K * groups + h)),
            pl.BlockSpec((None, width, seq), lambda b, h, i: (b, h, 0)),
            pl.BlockSpec((1, width), lambda b, h, i: (0, h)),
        ],
        out_specs=pl.BlockSpec((None, tile, width), lambda b, h, i: (b, i, h)),
        out_shape=jax.ShapeDtypeStruct((batch, seq, D_MODEL), BF16),
        scratch_shapes=[
            pltpu.VMEM((heads, 2 * DA_V_DIM, pitch), BF16),
            pltpu.VMEM((tile // 2, DA_V_DIM), BF16),
            pltpu.VMEM((2 * heads, tile // 2, pitch), F32),
            pltpu.VMEM((2 * heads, tile // 2, pitch), BF16),
            pltpu.VMEM((heads, DA_V_DIM + ONES_ROWS, pitch), F32),
            pltpu.VMEM((heads, 1, 2 * tile), F32),
            pltpu.VMEM((heads, 1, 2 * tile), F32),
        ],
        compiler_params=pltpu.CompilerParams(
            dimension_semantics=("parallel", "parallel", "arbitrary"),
            vmem_limit_bytes=VMEM_LIMIT),
    )(slopes, kmax, lam, proj3, proj3, vt, g)


def _rms(x, g):
    return x * lax.rsqrt(jnp.mean(x * x, axis=-1, keepdims=True) + EPS) * g


def _mlstm_chunk(r0, chunk, qk_ref, v_ref, og_ref, grow_ref, gcol_ref, cw_ref, cb_ref, brow_ref,
                 bcol_ref, ng_ref, c_scr, n_scr, m_scr, ext_scr, mout_scr):
    pad = SUBLANES
    rows = slice(r0, r0 + chunk)

    ext_scr[pad:pad + chunk, :] = qk_ref[rows, :].astype(F32)
    conv = cb_ref[...] + cw_ref[CONV_WIDTH - 1:CONV_WIDTH, :] * ext_scr[pad:pad + chunk, :]
    for tap in range(1, CONV_WIDTH):
        conv = conv + (cw_ref[CONV_WIDTH - 1 - tap:CONV_WIDTH - tap, :]
                       * ext_scr[pad - tap:pad - tap + chunk, :])
    ext_scr[0:pad, :] = ext_scr[chunk:chunk + pad, :]
    qk = conv * _sigmoid(conv)

    g_rows = grow_ref[:, rows] + brow_ref[...]
    g_cols = gcol_ref[rows, :] + bcol_ref[...]
    r_idx = lax.broadcasted_iota(jnp.int32, (chunk, chunk), 0)
    c_idx = lax.broadcasted_iota(jnp.int32, (chunk, chunk), 1)
    causal = r_idx >= c_idx
    tril = jnp.where(causal, 1.0, 0.0).astype(F32)
    triu = jnp.where(r_idx <= c_idx, 1.0, 0.0).astype(F32)
    b_rows = jnp.dot(_log_sigmoid(g_rows), triu, preferred_element_type=F32,
                     precision=lax.Precision.HIGHEST)
    b_cols = jnp.dot(tril, _log_sigmoid(g_cols), preferred_element_type=F32,
                     precision=lax.Precision.HIGHEST)

    q_scale = ML_QK_DIM ** -0.5
    for hd in range(ML_HEADS):
        yield
        qf = qk[:, hd * ML_QK_DIM:(hd + 1) * ML_QK_DIM] * q_scale
        kf = qk[:, ML_QK_WIDTH + hd * ML_QK_DIM:ML_QK_WIDTH + (hd + 1) * ML_QK_DIM]
        qb = qf.astype(BF16)
        sl = slice(hd * ML_V_DIM, (hd + 1) * ML_V_DIM)
        vb = v_ref[rows, sl]
        fcol = ML_HEADS + hd
        bt = b_cols[:, fcol:fcol + 1]
        bs = b_rows[fcol:fcol + 1, :]
        i_row = g_rows[hd:hd + 1, :]
        i_col = g_cols[:, hd:hd + 1]
        m_prev = m_scr[hd]
        c_prev = c_scr[hd]
        n_prev = n_scr[hd]

        logd = jnp.where(causal, bt - bs + i_row, NEG_BIG)
        inter = bt + m_prev
        m_t = jnp.maximum(inter, jnp.max(logd, axis=1, keepdims=True))
        dmat = jnp.exp(logd - m_t)
        sc = lax.dot_general(qb, kf.astype(BF16), NT_DIMS, preferred_element_type=F32) * dmat
        w_inter = jnp.exp(inter - m_t)
        num = (w_inter * jnp.dot(qb, c_prev.astype(BF16), preferred_element_type=F32)
               + jnp.dot(sc.astype(BF16), vb, preferred_element_type=F32))
        den = (w_inter * jnp.sum(qf * n_prev, axis=1, keepdims=True)
               + jnp.sum(sc, axis=1, keepdims=True))
        hh = num / jnp.maximum(jnp.abs(den), jnp.exp(-m_t))

        g_last = bt[chunk - 1:chunk, :]
        log_w = g_last - bt + i_col
        m_new = jnp.maximum(g_last + m_prev, jnp.max(log_w, axis=0, keepdims=True))
        kw = kf * jnp.exp(log_w - m_new)
        decay = jnp.exp(g_last + m_prev - m_new)
        c_scr[hd] = decay * c_prev + lax.dot_general(
            kw.astype(BF16), vb, TN_DIMS, preferred_element_type=F32)
        n_scr[hd] = decay * n_prev + jnp.sum(kw, axis=0, keepdims=True)
        m_scr[hd] = m_new

        hn = hh * lax.rsqrt(jnp.mean(hh * hh, axis=1, keepdims=True) + EPS) * ng_ref[:, sl]
        mout_scr[rows, sl] = (_sigmoid(og_ref[rows, sl].astype(F32)) * hn).astype(mout_scr.dtype)


def _mlstm_tail_kernel(qk_ref, v_ref, og_ref, grow_ref, gcol_ref, cw_ref, cb_ref, brow_ref,
                       bcol_ref, ng_ref,
                       x_ref, a_ref, ga_ref, gm_ref, bm_ref, wa_ref, wm_ref, wo_ref,
                       gmlp_ref, w1_ref, w2_ref, gfin_ref,
                       o_ref, c_scr, n_scr, m_scr, ext_scr, mout_scr, *, chunk, tiles_per_seq):
    i = pl.program_id(0)

    @pl.when(i == 0)
    def _():
        mout_scr[...] = jnp.zeros(mout_scr.shape, mout_scr.dtype)

    @pl.when(i % tiles_per_seq == 0)
    def _():
        c_scr[...] = jnp.zeros(c_scr.shape, F32)
        n_scr[...] = jnp.zeros(n_scr.shape, F32)
        m_scr[...] = jnp.zeros(m_scr.shape, F32)
        ext_scr[0:SUBLANES, :] = jnp.zeros((SUBLANES, ext_scr.shape[1]), F32)

    def tail_stages():
        ya = jnp.dot(a_ref[...], wa_ref[...], preferred_element_type=F32)
        ym = jnp.dot(mout_scr[...], wm_ref[...], preferred_element_type=F32)
        gate_a = _sigmoid(ga_ref[...].astype(F32) + bm_ref[:, :D_MODEL])
        gate_m = _sigmoid(gm_ref[...].astype(F32) + bm_ref[:, D_MODEL:])
        merged = (gate_a * ya + gate_m * ym).astype(BF16)
        yield
        x1 = x_ref[...] + jnp.dot(merged, wo_ref[...], preferred_element_type=F32)
        hm = _rms(x1, gmlp_ref[...]).astype(BF16)
        acc = x1
        for c in range(D_FF // FF_BLOCK):
            yield
            cols = slice(c * FF_BLOCK, (c + 1) * FF_BLOCK)
            u = jnp.maximum(jnp.dot(hm, w1_ref[:, cols], preferred_element_type=F32), 0.0)
            acc = acc + jnp.dot((u * u).astype(BF16), w2_ref[cols, :], preferred_element_type=F32)
        o_ref[...] = _rms(acc, gfin_ref[...])

    def mlstm_stages():
        for r0 in range(0, mout_scr.shape[0], chunk):
            yield from _mlstm_chunk(r0, chunk, qk_ref, v_ref, og_ref, grow_ref, gcol_ref, cw_ref,
                                    cb_ref, brow_ref, bcol_ref, ng_ref, c_scr, n_scr, m_scr,
                                    ext_scr, mout_scr)

    pending = [mlstm_stages(), tail_stages()]
    while pending:
        for stage in list(pending):
            if next(stage, StopIteration) is StopIteration:
                pending.remove(stage)


def _mlstm_tail(proj3, proj, g_rows, g_cols3, conv_w, conv_b, b_row, b_col, norm_g,
                x2, a2, b_merge, wa, wm, wo, g_mlp, w1, w2, g_fin, batch, seq, tm, chunk):
    tokens = batch * seq
    n_tiles = tokens // tm
    tps = seq // tm
    kern = functools.partial(_mlstm_tail_kernel, chunk=chunk, tiles_per_seq=tps)

    def cur(i):
        return jnp.minimum(i, n_tiles - 1)

    def prev(i):
        return jnp.maximum(i - 1, 0)

    def resident(arr):
        return pl.BlockSpec(arr.shape, lambda i: (0,) * arr.ndim, pipeline_mode=pl.Buffered(1))

    def seq_block(col):
        return pl.BlockSpec((None, tm, D_MODEL), lambda i: (cur(i) // tps, cur(i) % tps, col))

    return pl.pallas_call(
        kern,
        grid=(n_tiles + 1,),
        in_specs=[
            seq_block(COL_ML_QK),
            seq_block(COL_ML_V),
            seq_block(COL_ML_O),
            pl.BlockSpec((None, 2 * ML_HEADS, tm), lambda i: (cur(i) // tps, 0, cur(i) % tps)),
            pl.BlockSpec((None, tm, LANES), lambda i: (cur(i) // tps, cur(i) % tps, 0)),
            resident(conv_w), resident(conv_b), resident(b_row), resident(b_col), resident(norm_g),
            pl.BlockSpec((tm, D_MODEL), lambda i: (prev(i), 0)),
            pl.BlockSpec((tm, D_MODEL), lambda i: (prev(i), 0)),
            pl.BlockSpec((tm, D_MODEL), lambda i: (prev(i), COL_MG_A)),
            pl.BlockSpec((tm, D_MODEL), lambda i: (prev(i), COL_MG_M)),
            resident(b_merge), resident(wa), resident(wm), resident(wo), resident(g_mlp),
            resident(w1), resident(w2), resident(g_fin),
        ],
        out_specs=pl.BlockSpec((tm, D_MODEL), lambda i: (prev(i), 0)),
        out_shape=jax.ShapeDtypeStruct((tokens, D_MODEL), F32),
        scratch_shapes=[
            pltpu.VMEM((ML_HEADS, ML_QK_DIM, ML_V_DIM), F32),
            pltpu.VMEM((ML_HEADS, 1, ML_QK_DIM), F32),
            pltpu.VMEM((ML_HEADS, 1, 1), F32),
            pltpu.VMEM((chunk + 2 * SUBLANES, D_MODEL), F32),
            pltpu.VMEM((tm, D_MODEL), BF16),
        ],
        compiler_params=pltpu.CompilerParams(
            dimension_semantics=("arbitrary",), vmem_limit_bytes=VMEM_LIMIT),
    )(proj3, proj3, proj3, g_rows, g_cols3, conv_w, conv_b, b_row, b_col, norm_g,
      x2, a2, proj, proj, b_merge, wa, wm, wo, g_mlp, w1, w2, g_fin)


def kernel(x, norm_mix_g, w_in, b_gates, conv_w, conv_b, lam, da_norm_g, ml_norm_g, b_merge,
           w_branch_a, w_branch_m, w_out, norm_mlp_g, w_ff1, w_ff2, norm_final_g):
    batch, seq, _ = x.shape
    tokens = batch * seq
    x2 = x.reshape(tokens, D_MODEL)

    assert w_in.shape == (1, D_MODEL, D_IN)
    w_bf = w_in[0].astype(BF16)
    w_tail = w_bf[:, OFF_ML_O:]

    tm_proj = min(512, seq)
    proj, vt, ifg, ifg_t, k_sq = _in_proj(x2, norm_mix_g, w_bf, w_tail, batch, seq, tm_proj)
    kmax = jnp.sqrt(jnp.max(k_sq.reshape(batch, -1, LANES), axis=1))
    proj3 = proj.reshape(batch, seq, N_PROJ_BLOCKS * D_MODEL)

    slopes = 2.0 ** (-8.0 * jnp.arange(1, DA_HEADS + 1, dtype=F32) / DA_HEADS)
    attn_tile = min(512, seq)
    a_out = _attention(slopes, kmax, lam[0], proj3, vt, da_norm_g, batch, seq, attn_tile, 4)

    ifg3 = ifg.reshape(batch, seq, LANES)
    b_row = b_gates[0].reshape(2 * ML_HEADS, 1)
    b_col = jnp.pad(b_gates, ((0, 0), (0, LANES - 2 * ML_HEADS)))
    chunk = min(256, seq)
    out = _mlstm_tail(proj3, proj, ifg_t, ifg3, conv_w[0], conv_b, b_row, b_col, ml_norm_g,
                      x2, a_out.reshape(tokens, D_MODEL), b_merge,
                      w_branch_a[0].astype(BF16), w_branch_m[0].astype(BF16), w_out[0].astype(BF16),
                      norm_mlp_g, w_ff1[0].astype(BF16), w_ff2[0].astype(BF16),
                      norm_final_g.reshape(1, D_MODEL), batch, seq, min(512, seq), chunk)
    return out.reshape(batch, seq, D_MODEL)
```

```python
import functools
import math

import jax
import jax.numpy as jnp
from jax import lax
from jax.experimental import pallas as pl
from jax.experimental.pallas import tpu as pltpu

F32 = jnp.float32
BF16 = jnp.bfloat16

D_MODEL = 1024
DA_HEADS = 8
DA_HEAD_DIM = 64
DA_V_DIM = 2 * DA_HEAD_DIM
ML_HEADS = 4
ML_V_DIM = D_MODEL // ML_HEADS
ML_QK_DIM = ML_V_DIM // 2
ML_QK_WIDTH = ML_HEADS * ML_QK_DIM
CONV_WIDTH = 4
D_FF = 4 * D_MODEL
EPS = 1e-6
LAM_INIT = 0.8 - 0.6 * math.exp(-0.3 * 0)
NEG_BIG = -1e30
LOG2E = math.log2(math.e)
ONES_ROWS = 16
FF_BLOCK = 512
BOUND_SLACK = 1.02
MAX_SHIFT_SPREAD = 100.0

LANES = 128
SUBLANES = 8
VMEM_LIMIT = 60 * 1024 * 1024

COL_DA_Q, COL_DA_K, COL_ML_QK, COL_ML_V, COL_ML_O, COL_MG_A, COL_MG_M = range(7)
N_PROJ_BLOCKS = 7

OFF_DA_Q = 0
OFF_DA_K = OFF_DA_Q + D_MODEL
OFF_DA_V = OFF_DA_K + D_MODEL
OFF_ML_Q = OFF_DA_V + D_MODEL
OFF_ML_V = OFF_ML_Q + 2 * ML_QK_WIDTH
OFF_ML_IF = OFF_ML_V + D_MODEL
OFF_ML_O = OFF_ML_IF + 2 * ML_HEADS
D_IN = OFF_ML_O + 3 * D_MODEL

NT_DIMS = (((1,), (1,)), ((), ()))
TN_DIMS = (((0,), (0,)), ((), ()))
TT_DIMS = (((0,), (1,)), ((), ()))


def _sigmoid(x):
    return 1.0 / (1.0 + jnp.exp(-x))


def _log_sigmoid(x):
    return jnp.minimum(x, 0.0) - jnp.log(1.0 + jnp.exp(-jnp.abs(x)))


def _in_proj_kernel(x_ref, g_ref, w_ref, wtail_ref, proj_ref, vt_ref, if_ref, ift_ref, kn_ref):
    x = x_ref[...]
    hb = (x * lax.rsqrt(jnp.mean(x * x, axis=-1, keepdims=True) + EPS) * g_ref[...]).astype(BF16)

    def project(w_cols):
        return jnp.dot(hb, w_cols, preferred_element_type=F32)

    def project_t(w_cols):
        return lax.dot_general(w_cols, hb, TT_DIMS, preferred_element_type=F32)

    proj_ref[:, 0:D_MODEL] = (project(w_ref[:, OFF_DA_Q:OFF_DA_K]) * DA_HEAD_DIM ** -0.5).astype(BF16)
    k = project(w_ref[:, OFF_DA_K:OFF_DA_V])
    proj_ref[:, D_MODEL:2 * D_MODEL] = k.astype(BF16)
    r_idx = lax.broadcasted_iota(jnp.int32, (D_MODEL, LANES), 0)
    c_idx = lax.broadcasted_iota(jnp.int32, (D_MODEL, LANES), 1)
    select = jnp.where(r_idx // DA_HEAD_DIM == c_idx, 1.0, 0.0).astype(BF16)
    k_sq = jnp.dot((k * k).astype(BF16), select, preferred_element_type=F32)
    kn_ref[...] = jnp.broadcast_to(jnp.max(k_sq, axis=0, keepdims=True), kn_ref.shape)
    vt_ref[...] = project_t(w_ref[:, OFF_DA_V:OFF_ML_Q]).astype(BF16)
    for blk in range(2):
        src = OFF_ML_Q + blk * D_MODEL
        dst = (COL_ML_QK + blk) * D_MODEL
        proj_ref[:, dst:dst + D_MODEL] = project(w_ref[:, src:src + D_MODEL]).astype(BF16)
    w_if = w_ref[:, OFF_ML_IF:OFF_ML_IF + LANES]
    if_ref[...] = project(w_if)
    ift_ref[...] = project_t(w_if)
    for blk in range(3):
        dst = (COL_ML_O + blk) * D_MODEL
        proj_ref[:, dst:dst + D_MODEL] = project(
            wtail_ref[:, blk * D_MODEL:(blk + 1) * D_MODEL]).astype(BF16)


def _in_proj(x2, g, w_bf, w_tail, batch, seq, tm):
    tokens = batch * seq
    nsb = seq // tm

    def resident(arr):
        return pl.BlockSpec(arr.shape, lambda i: (0, 0), pipeline_mode=pl.Buffered(1))

    return pl.pallas_call(
        _in_proj_kernel,
        grid=(tokens // tm,),
        in_specs=[
            pl.BlockSpec((tm, D_MODEL), lambda i: (i, 0)),
            resident(g),
            resident(w_bf),
            resident(w_tail),
        ],
        out_specs=[
            pl.BlockSpec((tm, N_PROJ_BLOCKS * D_MODEL), lambda i: (i, 0)),
            pl.BlockSpec((None, D_MODEL, tm), lambda i: (i // nsb, 0, i % nsb)),
            pl.BlockSpec((tm, LANES), lambda i: (i, 0)),
            pl.BlockSpec((None, LANES, tm), lambda i: (i // nsb, 0, i % nsb)),
            pl.BlockSpec((SUBLANES, LANES), lambda i: (i, 0)),
        ],
        out_shape=[
            jax.ShapeDtypeStruct((tokens, N_PROJ_BLOCKS * D_MODEL), BF16),
            jax.ShapeDtypeStruct((batch, D_MODEL, seq), BF16),
            jax.ShapeDtypeStruct((tokens, LANES), F32),
            jax.ShapeDtypeStruct((batch, LANES, seq), F32),
            jax.ShapeDtypeStruct((tokens // tm * SUBLANES, LANES), F32),
        ],
        compiler_params=pltpu.CompilerParams(
            dimension_semantics=("parallel",), vmem_limit_bytes=VMEM_LIMIT),
    )(x2, g, w_bf, w_tail)


def _attn_kernel(slopes_ref, kmax_ref, lam_ref, q_ref, k_ref, vt_ref, g_ref, o_ref,
                 qq_scr, pos_scr, s_scr, p_scr, acc_scr, m_scr, l_scr, *, tile, heads):
    batch_idx = pl.program_id(0)
    group = pl.program_id(1)
    qi = pl.program_id(2)
    kt_size = tile // 2
    width = 2 * tile
    slope2 = [slopes_ref[group * heads + hh] * LOG2E for hh in range(heads)]

    def head_cols(hh):
        return slice(hh * DA_V_DIM, (hh + 1) * DA_V_DIM)

    key_off = lax.broadcasted_iota(jnp.int32, pos_scr.shape, 0).astype(F32)
    lane = lax.broadcasted_iota(jnp.int32, pos_scr.shape, 1)
    pos_scr[...] = jnp.where(lane < 2, key_off, 0.0).astype(BF16)
    q_lane = lax.broadcasted_iota(jnp.int32, (1, width), 1)
    q_off = jnp.where(q_lane >= tile, q_lane - tile, q_lane).astype(F32)
    spread = jnp.zeros((1, 1), F32)
    for hh in range(heads):
        qt = q_ref[:, head_cols(hh)].astype(F32).T * LOG2E
        d_idx = lax.broadcasted_iota(jnp.int32, qt.shape, 0)
        q0 = jnp.where(d_idx < DA_HEAD_DIM, qt, 0.0)
        q1 = jnp.where(d_idx >= DA_HEAD_DIM, qt, 0.0)
        qq = jnp.concatenate([q0, q1], axis=1).astype(BF16)
        qq_scr[hh, 0:DA_V_DIM, :width] = qq
        slope_vec = jnp.full((ONES_ROWS, width), slope2[hh], F32)
        slope_hi = slope_vec.astype(BF16).astype(F32)
        feat = lax.broadcasted_iota(jnp.int32, slope_vec.shape, 0)
        qq_scr[hh, DA_V_DIM:DA_V_DIM + ONES_ROWS, :width] = jnp.where(
            feat == 0, slope_hi, jnp.where(feat == 1, slope_vec - slope_hi, 0.0)).astype(BF16)
        qq_scr[hh, DA_V_DIM + ONES_ROWS:, :width] = jnp.zeros((DA_V_DIM - ONES_ROWS, width), BF16)
        acc_scr[hh, :, :width] = jnp.zeros((acc_scr.shape[1], width), F32)
        q_sq = qt * qt
        q_norm = jnp.sqrt(jnp.concatenate(
            [jnp.sum(q_sq[:DA_HEAD_DIM], axis=0, keepdims=True),
             jnp.sum(q_sq[DA_HEAD_DIM:], axis=0, keepdims=True)], axis=1))
        head = group * heads + hh
        k_max = jnp.where(q_lane < tile, kmax_ref[batch_idx, 2 * head],
                          kmax_ref[batch_idx, 2 * head + 1])
        qk_bound = q_norm * k_max * BOUND_SLACK + 1.0
        spread = jnp.maximum(spread, jnp.max(qk_bound, axis=1, keepdims=True))
        m_scr[hh] = qk_bound + slope2[hh] * q_off
    bounded_ok = jnp.max(spread) * 2.0 < MAX_SHIFT_SPREAD

    def lanes(blk):
        return slice(blk * kt_size, (blk + 1) * kt_size)

    n_blocks = width // kt_size
    upper_blocks = (1, 3)

    def key_rows(hh, t):
        start = pl.multiple_of(t * kt_size, kt_size)
        return jnp.concatenate([k_ref[pl.ds(start, kt_size), head_cols(hh)], pos_scr[...]], axis=1)

    def values_t(hh, t, n_tiles=1, with_ones=True):
        start = pl.multiple_of(jnp.maximum(t, 0) * kt_size, kt_size)
        vt = vt_ref[head_cols(hh), pl.ds(start, n_tiles * kt_size)]
        if not with_ones:
            return vt
        return jnp.concatenate([vt, jnp.ones((ONES_ROWS, n_tiles * kt_size), BF16)], axis=0)

    kk = lax.broadcasted_iota(jnp.int32, (kt_size, width), 0)
    qpos = lax.broadcasted_iota(jnp.int32, (kt_size, width), 1)
    qpos = jnp.where(qpos >= tile, qpos - tile, qpos)
    tri = (lax.broadcasted_iota(jnp.int32, (kt_size, kt_size), 0)
           <= lax.broadcasted_iota(jnp.int32, (kt_size, kt_size), 1))

    lam = lam_ref[...]
    lam_full = (jnp.exp(jnp.sum(lam[0:1] * lam[1:2], axis=1, keepdims=True))
                - jnp.exp(jnp.sum(lam[2:3] * lam[3:4], axis=1, keepdims=True)) + LAM_INIT)

    def finish(hh, acc, l):
        inv_l = 1.0 / l
        o = (acc[:, :tile] * inv_l[:, :tile]
             - acc[:, tile:] * (lam_full * inv_l[:, tile:]))
        o = o * lax.rsqrt(jnp.mean(o * o, axis=0, keepdims=True) + EPS) * (1.0 - LAM_INIT)
        o_ref[:, head_cols(hh)] = (o.T * g_ref[:, head_cols(hh)]).astype(o_ref.dtype)

    def interleave(stage_iters):
        pending = list(stage_iters)
        while pending:
            for it in list(pending):
                if next(it, StopIteration) is StopIteration:
                    pending.remove(it)

    @pl.when(bounded_ok)
    def _():
        for hh in range(heads):
            l_scr[hh] = jnp.zeros(l_scr.shape[1:], F32)

        def pair_stages(hh, i, diagonal):
            shift = m_scr[hh]
            for r in range(2):
                t = 2 * i + r
                c = slope2[hh] * (t * kt_size - qi * tile).astype(F32)
                kt = key_rows(hh, t)
                if diagonal and r == 1:
                    for blk in upper_blocks:
                        s = jnp.dot(kt, qq_scr[hh, :, lanes(blk)], preferred_element_type=F32)
                        s = jnp.where(tri, s, NEG_BIG)
                        p = jnp.exp2(s + (c - shift[:, lanes(blk)]))
                        l_scr[hh, :, lanes(blk)] += jnp.sum(p, axis=0, keepdims=True)
                        p_scr[2 * hh + r, :, lanes(blk)] = p.astype(BF16)
                else:
                    s = jnp.dot(kt, qq_scr[hh, :, :width], preferred_element_type=F32)
                    if diagonal:
                        s = jnp.where(kk <= qpos, s, NEG_BIG)
                    p = jnp.exp2(s + (c - shift))
                    l_scr[hh] += jnp.sum(p, axis=0, keepdims=True)
                    p_scr[2 * hh + r, :, :width] = p.astype(BF16)
                yield
            if diagonal:
                acc_scr[hh, :DA_V_DIM, :width] += jnp.dot(
                    values_t(hh, 2 * i, with_ones=False), p_scr[2 * hh, :, :width],
                    preferred_element_type=F32)
                vt = values_t(hh, 2 * i + 1, with_ones=False)
                for blk in upper_blocks:
                    acc_scr[hh, :DA_V_DIM, lanes(blk)] += jnp.dot(
                        vt, p_scr[2 * hh + 1, :, lanes(blk)], preferred_element_type=F32)
            else:
                p_pair = p_scr[2 * hh:2 * hh + 2, :, :width].reshape(2 * kt_size, width)
                acc_scr[hh, :DA_V_DIM, :width] += jnp.dot(
                    values_t(hh, 2 * i, 2, with_ones=False), p_pair, preferred_element_type=F32)

        def body(i, carry):
            interleave([pair_stages(hh, i, False) for hh in range(heads)])
            return carry

        lax.fori_loop(0, qi, body, 0)
        interleave([pair_stages(hh, qi, True) for hh in range(heads)])
        for hh in range(heads):
            finish(hh, acc_scr[hh, :DA_V_DIM, :width], l_scr[hh])

    @pl.when(jnp.logical_not(bounded_ok))
    def _():
        for hh in range(heads):
            m_scr[hh] = jnp.full(m_scr.shape[1:], NEG_BIG, F32)
            p_scr[2 * hh + 1, :, :width] = jnp.zeros((kt_size, width), BF16)

        def scores(hh, t, slot, blocks=None):
            kt = key_rows(hh, t)
            if blocks is None:
                s_scr[2 * hh + slot, :, :width] = jnp.dot(
                    kt, qq_scr[hh, :, :width], preferred_element_type=F32)
            else:
                for blk in blocks:
                    s_scr[2 * hh + slot, :, lanes(blk)] = jnp.dot(
                        kt, qq_scr[hh, :, lanes(blk)], preferred_element_type=F32)

        def weighted_values(hh, t, slot):
            return jnp.dot(values_t(hh, t), p_scr[2 * hh + slot, :, :width],
                           preferred_element_type=F32)

        def step(hh, t, slot, mask=None, prefetch_blocks=None):
            scores(hh, t + 1, 1 - slot, prefetch_blocks)
            c = slope2[hh] * (t * kt_size - qi * tile).astype(F32)
            s = s_scr[2 * hh + slot, :, :width]
            if mask is not None:
                s = jnp.where(mask, s, NEG_BIG)
            m_old = m_scr[hh]
            m_new = jnp.maximum(m_old, jnp.max(s, axis=0, keepdims=True) + c)
            p = jnp.exp2(s - (m_new - c))
            alpha = jnp.exp2(m_old - m_new)
            p_scr[2 * hh + slot, :, :width] = p.astype(BF16)
            m_scr[hh] = m_new
            acc_scr[hh, :, :width] = alpha * (acc_scr[hh, :, :width]
                                              + weighted_values(hh, t - 1, 1 - slot))

        for hh in range(heads):
            scores(hh, 0, 0)

        def body(i, carry):
            for slot in range(2):
                for hh in range(heads):
                    step(hh, 2 * i + slot, slot)
            return carry

        lax.fori_loop(0, qi, body, 0)

        for hh in range(heads):
            step(hh, 2 * qi, 0, mask=kk <= qpos, prefetch_blocks=upper_blocks)

        def last_tile(hh):
            t = 2 * qi + 1
            c = slope2[hh] * kt_size
            acc = acc_scr[hh, :, :width] + weighted_values(hh, t - 1, 0)
            vt = values_t(hh, t)
            parts = []
            for blk in range(n_blocks):
                part = acc[:, lanes(blk)]
                if blk in upper_blocks:
                    s = jnp.where(tri, s_scr[2 * hh + 1, :, lanes(blk)], NEG_BIG)
                    m_old = m_scr[hh, :, lanes(blk)]
                    m_new = jnp.maximum(m_old, jnp.max(s, axis=0, keepdims=True) + c)
                    p = jnp.exp2(s - (m_new - c)).astype(BF16)
                    part = (jnp.exp2(m_old - m_new) * part
                            + jnp.dot(vt, p, preferred_element_type=F32))
                parts.append(part)
            return jnp.concatenate(parts, axis=1)

        for hh in range(heads):
            acc = last_tile(hh)
            finish(hh, acc[:DA_V_DIM, :], acc[DA_V_DIM:DA_V_DIM + 1, :])


def _attention(slopes, kmax, lam, proj3, vt, g, batch, seq, tile, heads):
    kern = functools.partial(_attn_kernel, tile=tile, heads=heads)
    width = heads * DA_V_DIM
    groups = DA_HEADS // heads
    pitch = 2 * tile + LANES
    return pl.pallas_call(
        kern,
        grid=(batch, groups, seq // tile),
        in_specs=[
            pl.BlockSpec(memory_space=pltpu.SMEM),
            pl.BlockSpec(memory_space=pltpu.SMEM),
            pl.BlockSpec((4, DA_HEAD_DIM), lambda b, h, i: (0, 0)),
            pl.BlockSpec((None, tile, width), lambda b, h, i: (b, i, COL_DA_Q * groups + h)),
            pl.BlockSpec((None, seq, width), lambda b, h, i: (b, 0, COL_DA_K * groups + h)),
            pl.BlockSpec((None, width, seq), lambda b, h, i: (b, h, 0)),
            pl.BlockSpec((1, width), lambda b, h, i: (0, h)),
        ],
        out_specs=pl.BlockSpec((None, tile, width), lambda b, h, i: (b, i, h)),
        out_shape=jax.ShapeDtypeStruct((batch, seq, D_MODEL), BF16),
        scratch_shapes=[
            pltpu.VMEM((heads, 2 * DA_V_DIM, pitch), BF16),
            pltpu.VMEM((tile // 2, DA_V_DIM), BF16),
            pltpu.VMEM((2 * heads, tile // 2, pitch), F32),
            pltpu.VMEM((2 * heads, tile // 2, pitch), BF16),
            pltpu.VMEM((heads, DA_V_DIM + ONES_ROWS, pitch), F32),
            pltpu.VMEM((heads, 1, 2 * tile), F32),
            pltpu.VMEM((heads, 1, 2 * tile), F32),
        ],
        compiler_params=pltpu.CompilerParams(
            dimension_semantics=("parallel", "parallel", "arbitrary"),
            vmem_limit_bytes=VMEM_LIMIT),
    )(slopes, kmax, lam, proj3, proj3, vt, g)


def _rms(x, g):
    return x * lax.rsqrt(jnp.mean(x * x, axis=-1, keepdims=True) + EPS) * g


def _mlstm_chunk(r0, chunk, qk_ref, v_ref, og_ref, grow_ref, gcol_ref, cw_ref, cb_ref, brow_ref,
                 bcol_ref, ng_ref, c_scr, n_scr, m_scr, ext_scr, mout_scr):
    pad = SUBLANES
    rows = slice(r0, r0 + chunk)

    ext_scr[pad:pad + chunk, :] = qk_ref[rows, :].astype(F32)
    conv = cb_ref[...] + cw_ref[CONV_WIDTH - 1:CONV_WIDTH, :] * ext_scr[pad:pad + chunk, :]
    for tap in range(1, CONV_WIDTH):
        conv = conv + (cw_ref[CONV_WIDTH - 1 - tap:CONV_WIDTH - tap, :]
                       * ext_scr[pad - tap:pad - tap + chunk, :])
    ext_scr[0:pad, :] = ext_scr[chunk:chunk + pad, :]
    qk = conv * _sigmoid(conv)

    g_rows = grow_ref[:, rows] + brow_ref[...]
    g_cols = gcol_ref[rows, :] + bcol_ref[...]
    r_idx = lax.broadcasted_iota(jnp.int32, (chunk, chunk), 0)
    c_idx = lax.broadcasted_iota(jnp.int32, (chunk, chunk), 1)
    causal = r_idx >= c_idx
    tril = jnp.where(causal, 1.0, 0.0).astype(BF16)
    triu = jnp.where(r_idx <= c_idx, 1.0, 0.0).astype(BF16)

    def bf16_terms(x):
        hi = x.astype(BF16)
        mid = (x - hi.astype(F32)).astype(BF16)
        lo = (x - hi.astype(F32) - mid.astype(F32)).astype(BF16)
        return hi, mid, lo

    b_rows = sum(jnp.dot(term, triu, preferred_element_type=F32)
                 for term in bf16_terms(_log_sigmoid(g_rows)))
    b_cols = sum(jnp.dot(tril, term, preferred_element_type=F32)
                 for term in bf16_terms(_log_sigmoid(g_cols)))

    q_scale = ML_QK_DIM ** -0.5
    for hd in range(ML_HEADS):
        yield
        qf = qk[:, hd * ML_QK_DIM:(hd + 1) * ML_QK_DIM] * q_scale
        kf = qk[:, ML_QK_WIDTH + hd * ML_QK_DIM:ML_QK_WIDTH + (hd + 1) * ML_QK_DIM]
        qb = qf.astype(BF16)
        sl = slice(hd * ML_V_DIM, (hd + 1) * ML_V_DIM)
        vb = v_ref[rows, sl]
        fcol = ML_HEADS + hd
        bt = b_cols[:, fcol:fcol + 1]
        bs = b_rows[fcol:fcol + 1, :]
        i_row = g_rows[hd:hd + 1, :]
        i_col = g_cols[:, hd:hd + 1]
        m_prev = m_scr[hd]
        c_prev = c_scr[hd]
        n_prev = n_scr[hd]

        logd = jnp.where(causal, bt - bs + i_row, NEG_BIG)
        inter = bt + m_prev
        m_t = jnp.maximum(inter, jnp.max(logd, axis=1, keepdims=True))
        dmat = jnp.exp(logd - m_t)
        sc = lax.dot_general(qb, kf.astype(BF16), NT_DIMS, preferred_element_type=F32) * dmat
        w_inter = jnp.exp(inter - m_t)
        num = (w_inter * jnp.dot(qb, c_prev.astype(BF16), preferred_element_type=F32)
               + jnp.dot(sc.astype(BF16), vb, preferred_element_type=F32))
        den = (w_inter * jnp.sum(qf * n_prev, axis=1, keepdims=True)
               + jnp.sum(sc, axis=1, keepdims=True))
        hh = num / jnp.maximum(jnp.abs(den), jnp.exp(-m_t))

        g_last = bt[chunk - 1:chunk, :]
        log_w = g_last - bt + i_col
        m_new = jnp.maximum(g_last + m_prev, jnp.max(log_w, axis=0, keepdims=True))
        kw = kf * jnp.exp(log_w - m_new)
        decay = jnp.exp(g_last + m_prev - m_new)
        c_scr[hd] = decay * c_prev + lax.dot_general(
            kw.astype(BF16), vb, TN_DIMS, preferred_element_type=F32)
        n_scr[hd] = decay * n_prev + jnp.sum(kw, axis=0, keepdims=True)
        m_scr[hd] = m_new

        hn = hh * lax.rsqrt(jnp.mean(hh * hh, axis=1, keepdims=True) + EPS) * ng_ref[:, sl]
        mout_scr[rows, sl] = (_sigmoid(og_ref[rows, sl].astype(F32)) * hn).astype(mout_scr.dtype)


def _mlstm_tail_kernel(qk_ref, v_ref, og_ref, grow_ref, gcol_ref, cw_ref, cb_ref, brow_ref,
                       bcol_ref, ng_ref,
                       x_ref, a_ref, ga_ref, gm_ref, bm_ref, wa_ref, wm_ref, wo_ref,
                       gmlp_ref, w1_ref, w2_ref, gfin_ref,
                       o_ref, c_scr, n_scr, m_scr, ext_scr, mout_scr, *, chunk, tiles_per_seq):
    i = pl.program_id(0)

    @pl.when(i == 0)
    def _():
        mout_scr[...] = jnp.zeros(mout_scr.shape, mout_scr.dtype)

    @pl.when(i % tiles_per_seq == 0)
    def _():
        c_scr[...] = jnp.zeros(c_scr.shape, F32)
        n_scr[...] = jnp.zeros(n_scr.shape, F32)
        m_scr[...] = jnp.zeros(m_scr.shape, F32)
        ext_scr[0:SUBLANES, :] = jnp.zeros((SUBLANES, ext_scr.shape[1]), F32)

    def tail_stages():
        ya = jnp.dot(a_ref[...], wa_ref[...], preferred_element_type=F32)
        ym = jnp.dot(mout_scr[...], wm_ref[...], preferred_element_type=F32)
        gate_a = _sigmoid(ga_ref[...].astype(F32) + bm_ref[:, :D_MODEL])
        gate_m = _sigmoid(gm_ref[...].astype(F32) + bm_ref[:, D_MODEL:])
        merged = (gate_a * ya + gate_m * ym).astype(BF16)
        yield
        x1 = x_ref[...] + jnp.dot(merged, wo_ref[...], preferred_element_type=F32)
        hm = _rms(x1, gmlp_ref[...]).astype(BF16)
        acc = x1
        for c in range(D_FF // FF_BLOCK):
            yield
            cols = slice(c * FF_BLOCK, (c + 1) * FF_BLOCK)
            u = jnp.maximum(jnp.dot(hm, w1_ref[:, cols], preferred_element_type=F32), 0.0)
            acc = acc + jnp.dot((u * u).astype(BF16), w2_ref[cols, :], preferred_element_type=F32)
        o_ref[...] = _rms(acc, gfin_ref[...])

    def mlstm_stages():
        for r0 in range(0, mout_scr.shape[0], chunk):
            yield from _mlstm_chunk(r0, chunk, qk_ref, v_ref, og_ref, grow_ref, gcol_ref, cw_ref,
                                    cb_ref, brow_ref, bcol_ref, ng_ref, c_scr, n_scr, m_scr,
                                    ext_scr, mout_scr)

    pending = [mlstm_stages(), tail_stages()]
    while pending:
        for stage in list(pending):
            if next(stage, StopIteration) is StopIteration:
                pending.remove(stage)


def _mlstm_tail(proj3, proj, g_rows, g_cols3, conv_w, conv_b, b_row, b_col, norm_g,
                x2, a2, b_merge, wa, wm, wo, g_mlp, w1, w2, g_fin, batch, seq, tm, chunk):
    tokens = batch * seq
    n_tiles = tokens // tm
    tps = seq // tm
    kern = functools.partial(_mlstm_tail_kernel, chunk=chunk, tiles_per_seq=tps)

    def cur(i):
        return jnp.minimum(i, n_tiles - 1)

    def prev(i):
        return jnp.maximum(i - 1, 0)

    def resident(arr):
        return pl.BlockSpec(arr.shape, lambda i: (0,) * arr.ndim, pipeline_mode=pl.Buffered(1))

    def seq_block(col):
        return pl.BlockSpec((None, tm, D_MODEL), lambda i: (cur(i) // tps, cur(i) % tps, col))

    return pl.pallas_call(
        kern,
        grid=(n_tiles + 1,),
        in_specs=[
            seq_block(COL_ML_QK),
            seq_block(COL_ML_V),
            seq_block(COL_ML_O),
            pl.BlockSpec((None, 2 * ML_HEADS, tm), lambda i: (cur(i) // tps, 0, cur(i) % tps)),
            pl.BlockSpec((None, tm, LANES), lambda i: (cur(i) // tps, cur(i) % tps, 0)),
            resident(conv_w), resident(conv_b), resident(b_row), resident(b_col), resident(norm_g),
            pl.BlockSpec((tm, D_MODEL), lambda i: (prev(i), 0)),
            pl.BlockSpec((tm, D_MODEL), lambda i: (prev(i), 0)),
            pl.BlockSpec((tm, D_MODEL), lambda i: (prev(i), COL_MG_A)),
            pl.BlockSpec((tm, D_MODEL), lambda i: (prev(i), COL_MG_M)),
            resident(b_merge), resident(wa), resident(wm), resident(wo), resident(g_mlp),
            resident(w1), resident(w2), resident(g_fin),
        ],
        out_specs=pl.BlockSpec((tm, D_MODEL), lambda i: (prev(i), 0)),
        out_shape=jax.ShapeDtypeStruct((tokens, D_MODEL), F32),
        scratch_shapes=[
            pltpu.VMEM((ML_HEADS, ML_QK_DIM, ML_V_DIM), F32),
            pltpu.VMEM((ML_HEADS, 1, ML_QK_DIM), F32),
            pltpu.VMEM((ML_HEADS, 1, 1), F32),
            pltpu.VMEM((chunk + 2 * SUBLANES, D_MODEL), F32),
            pltpu.VMEM((tm, D_MODEL), BF16),
        ],
        compiler_params=pltpu.CompilerParams(
            dimension_semantics=("arbitrary",), vmem_limit_bytes=VMEM_LIMIT),
    )(proj3, proj3, proj3, g_rows, g_cols3, conv_w, conv_b, b_row, b_col, norm_g,
      x2, a2, proj, proj, b_merge, wa, wm, wo, g_mlp, w1, w2, g_fin)


def kernel(x, norm_mix_g, w_in, b_gates, conv_w, conv_b, lam, da_norm_g, ml_norm_g, b_merge,
           w_branch_a, w_branch_m, w_out, norm_mlp_g, w_ff1, w_ff2, norm_final_g):
    batch, seq, _ = x.shape
    tokens = batch * seq
    x2 = x.reshape(tokens, D_MODEL)

    assert w_in.shape == (1, D_MODEL, D_IN)
    w_bf = w_in[0].astype(BF16)
    w_tail = w_bf[:, OFF_ML_O:]

    tm_proj = min(512, seq)
    proj, vt, ifg, ifg_t, k_sq = _in_proj(x2, norm_mix_g, w_bf, w_tail, batch, seq, tm_proj)
    kmax = jnp.sqrt(jnp.max(k_sq.reshape(batch, -1, LANES), axis=1))
    proj3 = proj.reshape(batch, seq, N_PROJ_BLOCKS * D_MODEL)

    slopes = 2.0 ** (-8.0 * jnp.arange(1, DA_HEADS + 1, dtype=F32) / DA_HEADS)
    attn_tile = min(512, seq)
    a_out = _attention(slopes, kmax, lam[0], proj3, vt, da_norm_g, batch, seq, attn_tile, 4)

    ifg3 = ifg.reshape(batch, seq, LANES)
    b_row = b_gates[0].reshape(2 * ML_HEADS, 1)
    b_col = jnp.pad(b_gates, ((0, 0), (0, LANES - 2 * ML_HEADS)))
    chunk = min(256, seq)
    out = _mlstm_tail(proj3, proj, ifg_t, ifg3, conv_w[0], conv_b, b_row, b_col, ml_norm_g,
                      x2, a_out.reshape(tokens, D_MODEL), b_merge,
                      w_branch_a[0].astype(BF16), w_branch_m[0].astype(BF16), w_out[0].astype(BF16),
                      norm_mlp_g, w_ff1[0].astype(BF16), w_ff2[0].astype(BF16),
                      norm_final_g.reshape(1, D_MODEL), batch, seq, min(512, seq), chunk)
    return out.reshape(batch, seq, D_MODEL)
```

```python
import functools
import math

import jax
import jax.numpy as jnp
from jax import lax
from jax.experimental import pallas as pl
from jax.experimental.pallas import tpu as pltpu

F32 = jnp.float32
BF16 = jnp.bfloat16

D_MODEL = 1024
DA_HEADS = 8
DA_HEAD_DIM = 64
DA_V_DIM = 2 * DA_HEAD_DIM
ML_HEADS = 4
ML_V_DIM = D_MODEL // ML_HEADS
ML_QK_DIM = ML_V_DIM // 2
ML_QK_WIDTH = ML_HEADS * ML_QK_DIM
CONV_WIDTH = 4
D_FF = 4 * D_MODEL
EPS = 1e-6
LAM_INIT = 0.8 - 0.6 * math.exp(-0.3 * 0)
NEG_BIG = -1e30
LOG2E = math.log2(math.e)
ONES_ROWS = 16
FF_BLOCK = 512
BOUND_SLACK = 1.02
MAX_SHIFT_SPREAD = 100.0

LANES = 128
SUBLANES = 8
VMEM_LIMIT = 60 * 1024 * 1024

ROW_TILE = 512
ATTN_TILE = 512
ATTN_HEADS_PER_STEP = 4
MLSTM_CHUNK = 256

COL_DA_Q, COL_DA_K, COL_ML_QK, COL_ML_V, COL_ML_O, COL_MG_A, COL_MG_M = range(7)
N_PROJ_BLOCKS = 7

OFF_DA_Q = 0
OFF_DA_K = OFF_DA_Q + D_MODEL
OFF_DA_V = OFF_DA_K + D_MODEL
OFF_ML_Q = OFF_DA_V + D_MODEL
OFF_ML_V = OFF_ML_Q + 2 * ML_QK_WIDTH
OFF_ML_IF = OFF_ML_V + D_MODEL
OFF_ML_O = OFF_ML_IF + 2 * ML_HEADS
D_IN = OFF_ML_O + 3 * D_MODEL

NT_DIMS = (((1,), (1,)), ((), ()))
TN_DIMS = (((0,), (0,)), ((), ()))
TT_DIMS = (((0,), (1,)), ((), ()))


def _sigmoid(x):
    return 1.0 / (1.0 + jnp.exp(-x))


def _log_sigmoid(x):
    return jnp.minimum(x, 0.0) - jnp.log(1.0 + jnp.exp(-jnp.abs(x)))


def _in_proj_kernel(x_ref, g_ref, w_ref, wtail_ref, proj_ref, vt_ref, if_ref, ift_ref, kn_ref):
    x = x_ref[...]
    hb = (x * lax.rsqrt(jnp.mean(x * x, axis=-1, keepdims=True) + EPS) * g_ref[...]).astype(BF16)

    def project(w_cols):
        return jnp.dot(hb, w_cols, preferred_element_type=F32)

    def project_t(w_cols):
        return lax.dot_general(w_cols, hb, TT_DIMS, preferred_element_type=F32)

    proj_ref[:, 0:D_MODEL] = (project(w_ref[:, OFF_DA_Q:OFF_DA_K])
                              * (DA_HEAD_DIM ** -0.5 * LOG2E)).astype(BF16)
    k = project(w_ref[:, OFF_DA_K:OFF_DA_V])
    proj_ref[:, D_MODEL:2 * D_MODEL] = k.astype(BF16)
    r_idx = lax.broadcasted_iota(jnp.int32, (D_MODEL, LANES), 0)
    c_idx = lax.broadcasted_iota(jnp.int32, (D_MODEL, LANES), 1)
    select = jnp.where(r_idx // DA_HEAD_DIM == c_idx, 1.0, 0.0).astype(BF16)
    k_sq = jnp.dot((k * k).astype(BF16), select, preferred_element_type=F32)
    kn_ref[...] = jnp.broadcast_to(jnp.max(k_sq, axis=0, keepdims=True), kn_ref.shape)
    vt_ref[...] = project_t(w_ref[:, OFF_DA_V:OFF_ML_Q]).astype(BF16)
    for blk in range(2):
        src = OFF_ML_Q + blk * D_MODEL
        dst = (COL_ML_QK + blk) * D_MODEL
        proj_ref[:, dst:dst + D_MODEL] = project(w_ref[:, src:src + D_MODEL]).astype(BF16)
    w_if = w_ref[:, OFF_ML_IF:OFF_ML_IF + LANES]
    if_ref[...] = project(w_if)
    ift_ref[...] = project_t(w_if)
    for blk in range(3):
        dst = (COL_ML_O + blk) * D_MODEL
        proj_ref[:, dst:dst + D_MODEL] = project(
            wtail_ref[:, blk * D_MODEL:(blk + 1) * D_MODEL]).astype(BF16)


def _in_proj(x2, g, w_bf, w_tail, batch, seq, tm):
    tokens = batch * seq
    nsb = seq // tm

    def resident(arr):
        return pl.BlockSpec(arr.shape, lambda i: (0, 0), pipeline_mode=pl.Buffered(1))

    return pl.pallas_call(
        _in_proj_kernel,
        grid=(tokens // tm,),
        in_specs=[
            pl.BlockSpec((tm, D_MODEL), lambda i: (i, 0)),
            resident(g),
            resident(w_bf),
            resident(w_tail),
        ],
        out_specs=[
            pl.BlockSpec((tm, N_PROJ_BLOCKS * D_MODEL), lambda i: (i, 0)),
            pl.BlockSpec((None, D_MODEL, tm), lambda i: (i // nsb, 0, i % nsb)),
            pl.BlockSpec((tm, LANES), lambda i: (i, 0)),
            pl.BlockSpec((None, LANES, tm), lambda i: (i // nsb, 0, i % nsb)),
            pl.BlockSpec((SUBLANES, LANES), lambda i: (i, 0)),
        ],
        out_shape=[
            jax.ShapeDtypeStruct((tokens, N_PROJ_BLOCKS * D_MODEL), BF16),
            jax.ShapeDtypeStruct((batch, D_MODEL, seq), BF16),
            jax.ShapeDtypeStruct((tokens, LANES), F32),
            jax.ShapeDtypeStruct((batch, LANES, seq), F32),
            jax.ShapeDtypeStruct((tokens // tm * SUBLANES, LANES), F32),
        ],
        compiler_params=pltpu.CompilerParams(
            dimension_semantics=("parallel",), vmem_limit_bytes=VMEM_LIMIT),
    )(x2, g, w_bf, w_tail)


def _attn_kernel(slopes_ref, kmax_ref, lam_ref, q_ref, k_ref, vt_ref, g_ref, o_ref,
                 qq_scr, pos_scr, s_scr, p_scr, acc_scr, m_scr, l_scr, *, tile, heads):
    batch_idx = pl.program_id(0)
    group = pl.program_id(1)
    qi = pl.program_id(2)
    kt_size = tile // 2
    width = 2 * tile
    slope2 = [slopes_ref[group * heads + hh] * LOG2E for hh in range(heads)]

    def head_cols(hh):
        return slice(hh * DA_V_DIM, (hh + 1) * DA_V_DIM)

    key_off = lax.broadcasted_iota(jnp.int32, pos_scr.shape, 0).astype(F32)
    lane = lax.broadcasted_iota(jnp.int32, pos_scr.shape, 1)
    pos_scr[...] = jnp.where(lane < 2, key_off, 0.0).astype(BF16)
    q_lane = lax.broadcasted_iota(jnp.int32, (1, width), 1)
    q_off = jnp.where(q_lane >= tile, q_lane - tile, q_lane).astype(F32)
    spread = jnp.zeros((1, 1), F32)
    for hh in range(heads):
        qt = q_ref[:, head_cols(hh)].T
        d_idx = lax.broadcasted_iota(jnp.int32, qt.shape, 0)
        zero = jnp.zeros_like(qt)
        q0 = jnp.where(d_idx < DA_HEAD_DIM, qt, zero)
        q1 = jnp.where(d_idx >= DA_HEAD_DIM, qt, zero)
        qq_scr[hh, 0:DA_V_DIM, :width] = jnp.concatenate([q0, q1], axis=1)
        qt = qt.astype(F32)
        slope_vec = jnp.full((ONES_ROWS, width), slope2[hh], F32)
        slope_hi = slope_vec.astype(BF16).astype(F32)
        feat = lax.broadcasted_iota(jnp.int32, slope_vec.shape, 0)
        qq_scr[hh, DA_V_DIM:DA_V_DIM + ONES_ROWS, :width] = jnp.where(
            feat == 0, slope_hi, jnp.where(feat == 1, slope_vec - slope_hi, 0.0)).astype(BF16)
        qq_scr[hh, DA_V_DIM + ONES_ROWS:, :width] = jnp.zeros((DA_V_DIM - ONES_ROWS, width), BF16)
        acc_scr[hh, :, :width] = jnp.zeros((acc_scr.shape[1], width), F32)
        q_sq = qt * qt
        q_norm = jnp.sqrt(jnp.concatenate(
            [jnp.sum(q_sq[:DA_HEAD_DIM], axis=0, keepdims=True),
             jnp.sum(q_sq[DA_HEAD_DIM:], axis=0, keepdims=True)], axis=1))
        head = group * heads + hh
        k_max = jnp.where(q_lane < tile, kmax_ref[batch_idx, 2 * head],
                          kmax_ref[batch_idx, 2 * head + 1])
        qk_bound = q_norm * k_max * BOUND_SLACK + 1.0
        spread = jnp.maximum(spread, jnp.max(qk_bound, axis=1, keepdims=True))
        m_scr[hh] = qk_bound + slope2[hh] * q_off
    bounded_ok = jnp.max(spread) * 2.0 < MAX_SHIFT_SPREAD

    def lanes(blk):
        return slice(blk * kt_size, (blk + 1) * kt_size)

    n_blocks = width // kt_size
    upper_blocks = (1, 3)

    def key_rows(hh, t):
        start = pl.multiple_of(t * kt_size, kt_size)
        return jnp.concatenate([k_ref[pl.ds(start, kt_size), head_cols(hh)], pos_scr[...]], axis=1)

    def values_t(hh, t, n_tiles=1, with_ones=True):
        start = pl.multiple_of(jnp.maximum(t, 0) * kt_size, kt_size)
        vt = vt_ref[head_cols(hh), pl.ds(start, n_tiles * kt_size)]
        if not with_ones:
            return vt
        return jnp.concatenate([vt, jnp.ones((ONES_ROWS, n_tiles * kt_size), BF16)], axis=0)

    kk = lax.broadcasted_iota(jnp.int32, (kt_size, width), 0)
    qpos = lax.broadcasted_iota(jnp.int32, (kt_size, width), 1)
    qpos = jnp.where(qpos >= tile, qpos - tile, qpos)
    tri = (lax.broadcasted_iota(jnp.int32, (kt_size, kt_size), 0)
           <= lax.broadcasted_iota(jnp.int32, (kt_size, kt_size), 1))

    lam = lam_ref[...]
    lam_full = (jnp.exp(jnp.sum(lam[0:1] * lam[1:2], axis=1, keepdims=True))
                - jnp.exp(jnp.sum(lam[2:3] * lam[3:4], axis=1, keepdims=True)) + LAM_INIT)

    def finish(hh, acc, l):
        inv_l = 1.0 / l
        o = (acc[:, :tile] * inv_l[:, :tile]
             - acc[:, tile:] * (lam_full * inv_l[:, tile:]))
        o = o * lax.rsqrt(jnp.mean(o * o, axis=0, keepdims=True) + EPS) * (1.0 - LAM_INIT)
        o_ref[:, head_cols(hh)] = (o.T * g_ref[:, head_cols(hh)]).astype(o_ref.dtype)

    def interleave(stage_iters):
        pending = list(stage_iters)
        while pending:
            for it in list(pending):
                if next(it, StopIteration) is StopIteration:
                    pending.remove(it)

    @pl.when(bounded_ok)
    def _():
        for hh in range(heads):
            l_scr[hh] = jnp.zeros(l_scr.shape[1:], F32)

        def pair_stages(hh, i, diagonal):
            shift = m_scr[hh]
            for r in range(2):
                t = 2 * i + r
                c = slope2[hh] * (t * kt_size - qi * tile).astype(F32)
                kt = key_rows(hh, t)
                if diagonal and r == 1:
                    for blk in upper_blocks:
                        s = jnp.dot(kt, qq_scr[hh, :, lanes(blk)], preferred_element_type=F32)
                        s = jnp.where(tri, s, NEG_BIG)
                        p = jnp.exp2(s + (c - shift[:, lanes(blk)]))
                        l_scr[hh, :, lanes(blk)] += jnp.sum(p, axis=0, keepdims=True)
                        p_scr[2 * hh + r, :, lanes(blk)] = p.astype(BF16)
                else:
                    s = jnp.dot(kt, qq_scr[hh, :, :width], preferred_element_type=F32)
                    if diagonal:
                        s = jnp.where(kk <= qpos, s, NEG_BIG)
                    p = jnp.exp2(s + (c - shift))
                    l_scr[hh] += jnp.sum(p, axis=0, keepdims=True)
                    p_scr[2 * hh + r, :, :width] = p.astype(BF16)
                yield
            if diagonal:
                acc_scr[hh, :DA_V_DIM, :width] += jnp.dot(
                    values_t(hh, 2 * i, with_ones=False), p_scr[2 * hh, :, :width],
                    preferred_element_type=F32)
                vt = values_t(hh, 2 * i + 1, with_ones=False)
                for blk in upper_blocks:
                    acc_scr[hh, :DA_V_DIM, lanes(blk)] += jnp.dot(
                        vt, p_scr[2 * hh + 1, :, lanes(blk)], preferred_element_type=F32)
            else:
                p_pair = p_scr[2 * hh:2 * hh + 2, :, :width].reshape(2 * kt_size, width)
                acc_scr[hh, :DA_V_DIM, :width] += jnp.dot(
                    values_t(hh, 2 * i, 2, with_ones=False), p_pair, preferred_element_type=F32)

        def body(i, carry):
            interleave([pair_stages(hh, i, False) for hh in range(heads)])
            return carry

        lax.fori_loop(0, qi, body, 0)
        interleave([pair_stages(hh, qi, True) for hh in range(heads)])
        for hh in range(heads):
            finish(hh, acc_scr[hh, :DA_V_DIM, :width], l_scr[hh])

    @pl.when(jnp.logical_not(bounded_ok))
    def _():
        for hh in range(heads):
            m_scr[hh] = jnp.full(m_scr.shape[1:], NEG_BIG, F32)
            p_scr[2 * hh + 1, :, :width] = jnp.zeros((kt_size, width), BF16)

        def scores(hh, t, slot, blocks=None):
            kt = key_rows(hh, t)
            if blocks is None:
                s_scr[2 * hh + slot, :, :width] = jnp.dot(
                    kt, qq_scr[hh, :, :width], preferred_element_type=F32)
            else:
                for blk in blocks:
                    s_scr[2 * hh + slot, :, lanes(blk)] = jnp.dot(
                        kt, qq_scr[hh, :, lanes(blk)], preferred_element_type=F32)

        def weighted_values(hh, t, slot):
            return jnp.dot(values_t(hh, t), p_scr[2 * hh + slot, :, :width],
                           preferred_element_type=F32)

        def step(hh, t, slot, mask=None, prefetch_blocks=None):
            scores(hh, t + 1, 1 - slot, prefetch_blocks)
            c = slope2[hh] * (t * kt_size - qi * tile).astype(F32)
            s = s_scr[2 * hh + slot, :, :width]
            if mask is not None:
                s = jnp.where(mask, s, NEG_BIG)
            m_old = m_scr[hh]
            m_new = jnp.maximum(m_old, jnp.max(s, axis=0, keepdims=True) + c)
            p = jnp.exp2(s - (m_new - c))
            alpha = jnp.exp2(m_old - m_new)
            p_scr[2 * hh + slot, :, :width] = p.astype(BF16)
            m_scr[hh] = m_new
            acc_scr[hh, :, :width] = alpha * (acc_scr[hh, :, :width]
                                              + weighted_values(hh, t - 1, 1 - slot))

        for hh in range(heads):
            scores(hh, 0, 0)

        def body(i, carry):
            for slot in range(2):
                for hh in range(heads):
                    step(hh, 2 * i + slot, slot)
            return carry

        lax.fori_loop(0, qi, body, 0)

        for hh in range(heads):
            step(hh, 2 * qi, 0, mask=kk <= qpos, prefetch_blocks=upper_blocks)

        def last_tile(hh):
            t = 2 * qi + 1
            c = slope2[hh] * kt_size
            acc = acc_scr[hh, :, :width] + weighted_values(hh, t - 1, 0)
            vt = values_t(hh, t)
            parts = []
            for blk in range(n_blocks):
                part = acc[:, lanes(blk)]
                if blk in upper_blocks:
                    s = jnp.where(tri, s_scr[2 * hh + 1, :, lanes(blk)], NEG_BIG)
                    m_old = m_scr[hh, :, lanes(blk)]
                    m_new = jnp.maximum(m_old, jnp.max(s, axis=0, keepdims=True) + c)
                    p = jnp.exp2(s - (m_new - c)).astype(BF16)
                    part = (jnp.exp2(m_old - m_new) * part
                            + jnp.dot(vt, p, preferred_element_type=F32))
                parts.append(part)
            return jnp.concatenate(parts, axis=1)

        for hh in range(heads):
            acc = last_tile(hh)
            finish(hh, acc[:DA_V_DIM, :], acc[DA_V_DIM:DA_V_DIM + 1, :])


def _attention(slopes, kmax, lam, proj3, vt, g, batch, seq, tile, heads):
    kern = functools.partial(_attn_kernel, tile=tile, heads=heads)
    width = heads * DA_V_DIM
    groups = DA_HEADS // heads
    pitch = 2 * tile + LANES
    return pl.pallas_call(
        kern,
        grid=(batch, groups, seq // tile),
        in_specs=[
            pl.BlockSpec(memory_space=pltpu.SMEM),
            pl.BlockSpec(memory_space=pltpu.SMEM),
            pl.BlockSpec((4, DA_HEAD_DIM), lambda b, h, i: (0, 0)),
            pl.BlockSpec((None, tile, width), lambda b, h, i: (b, i, COL_DA_Q * groups + h)),
            pl.BlockSpec((None, seq, width), lambda b, h, i: (b, 0, COL_DA_K * groups + h)),
            pl.BlockSpec((None, width, seq), lambda b, h, i: (b, h, 0)),
            pl.BlockSpec((1, width), lambda b, h, i: (0, h)),
        ],
        out_specs=pl.BlockSpec((None, tile, width), lambda b, h, i: (b, i, h)),
        out_shape=jax.ShapeDtypeStruct((batch, seq, D_MODEL), BF16),
        scratch_shapes=[
            pltpu.VMEM((heads, 2 * DA_V_DIM, pitch), BF16),
            pltpu.VMEM((tile // 2, DA_V_DIM), BF16),
            pltpu.VMEM((2 * heads, tile // 2, pitch), F32),
            pltpu.VMEM((2 * heads, tile // 2, pitch), BF16),
            pltpu.VMEM((heads, DA_V_DIM + ONES_ROWS, pitch), F32),
            pltpu.VMEM((heads, 1, 2 * tile), F32),
            pltpu.VMEM((heads, 1, 2 * tile), F32),
        ],
        compiler_params=pltpu.CompilerParams(
            dimension_semantics=("parallel", "parallel", "arbitrary"),
            vmem_limit_bytes=VMEM_LIMIT),
    )(slopes, kmax, lam, proj3, proj3, vt, g)


def _rms(x, g):
    return x * lax.rsqrt(jnp.mean(x * x, axis=-1, keepdims=True) + EPS) * g


def _mlstm_chunk(r0, chunk, qk_ref, v_ref, og_ref, grow_ref, gcol_ref, cw_ref, cb_ref, brow_ref,
                 bcol_ref, ng_ref, c_scr, n_scr, m_scr, ext_scr, mout_scr):
    pad = SUBLANES
    rows = slice(r0, r0 + chunk)

    ext_scr[pad:pad + chunk, :] = qk_ref[rows, :].astype(F32)
    conv = cb_ref[...] + cw_ref[CONV_WIDTH - 1:CONV_WIDTH, :] * ext_scr[pad:pad + chunk, :]
    for tap in range(1, CONV_WIDTH):
        conv = conv + (cw_ref[CONV_WIDTH - 1 - tap:CONV_WIDTH - tap, :]
                       * ext_scr[pad - tap:pad - tap + chunk, :])
    ext_scr[0:pad, :] = ext_scr[chunk:chunk + pad, :]
    qk = conv * _sigmoid(conv)

    g_rows = grow_ref[:, rows] + brow_ref[...]
    g_cols = gcol_ref[rows, :] + bcol_ref[...]
    r_idx = lax.broadcasted_iota(jnp.int32, (chunk, chunk), 0)
    c_idx = lax.broadcasted_iota(jnp.int32, (chunk, chunk), 1)
    causal = r_idx >= c_idx
    tril = jnp.where(causal, 1.0, 0.0).astype(BF16)
    triu = jnp.where(r_idx <= c_idx, 1.0, 0.0).astype(BF16)

    def bf16_terms(x):
        hi = x.astype(BF16)
        mid = (x - hi.astype(F32)).astype(BF16)
        lo = (x - hi.astype(F32) - mid.astype(F32)).astype(BF16)
        return hi, mid, lo

    b_rows = sum(jnp.dot(term, triu, preferred_element_type=F32)
                 for term in bf16_terms(_log_sigmoid(g_rows)))
    b_cols = sum(jnp.dot(tril, term, preferred_element_type=F32)
                 for term in bf16_terms(_log_sigmoid(g_cols)))

    q_scale = ML_QK_DIM ** -0.5
    for hd in range(ML_HEADS):
        yield
        qf = qk[:, hd * ML_QK_DIM:(hd + 1) * ML_QK_DIM] * q_scale
        kf = qk[:, ML_QK_WIDTH + hd * ML_QK_DIM:ML_QK_WIDTH + (hd + 1) * ML_QK_DIM]
        qb = qf.astype(BF16)
        sl = slice(hd * ML_V_DIM, (hd + 1) * ML_V_DIM)
        vb = v_ref[rows, sl]
        fcol = ML_HEADS + hd
        bt = b_cols[:, fcol:fcol + 1]
        bs = b_rows[fcol:fcol + 1, :]
        i_row = g_rows[hd:hd + 1, :]
        i_col = g_cols[:, hd:hd + 1]
        m_prev = m_scr[hd]
        c_prev = c_scr[hd]
        n_prev = n_scr[hd]

        logd = jnp.where(causal, bt - bs + i_row, NEG_BIG)
        inter = bt + m_prev
        m_t = jnp.maximum(inter, jnp.max(logd, axis=1, keepdims=True))
        dmat = jnp.exp(logd - m_t)
        sc = lax.dot_general(qb, kf.astype(BF16), NT_DIMS, preferred_element_type=F32) * dmat
        w_inter = jnp.exp(inter - m_t)
        num = (w_inter * jnp.dot(qb, c_prev.astype(BF16), preferred_element_type=F32)
               + jnp.dot(sc.astype(BF16), vb, preferred_element_type=F32))
        den = (w_inter * jnp.sum(qf * n_prev, axis=1, keepdims=True)
               + jnp.sum(sc, axis=1, keepdims=True))
        hh = num / jnp.maximum(jnp.abs(den), jnp.exp(-m_t))

        g_last = bt[chunk - 1:chunk, :]
        log_w = g_last - bt + i_col
        m_new = jnp.maximum(g_last + m_prev, jnp.max(log_w, axis=0, keepdims=True))
        kw = kf * jnp.exp(log_w - m_new)
        decay = jnp.exp(g_last + m_prev - m_new)
        c_scr[hd] = decay * c_prev + lax.dot_general(
            kw.astype(BF16), vb, TN_DIMS, preferred_element_type=F32)
        n_scr[hd] = decay * n_prev + jnp.sum(kw, axis=0, keepdims=True)
        m_scr[hd] = m_new

        hn = hh * lax.rsqrt(jnp.mean(hh * hh, axis=1, keepdims=True) + EPS) * ng_ref[:, sl]
        mout_scr[rows, sl] = (_sigmoid(og_ref[rows, sl].astype(F32)) * hn).astype(mout_scr.dtype)


def _mlstm_tail_kernel(qk_ref, v_ref, og_ref, grow_ref, gcol_ref, cw_ref, cb_ref, brow_ref,
                       bcol_ref, ng_ref,
                       x_ref, a_ref, ga_ref, gm_ref, bm_ref, wa_ref, wm_ref, wo_ref,
                       gmlp_ref, w1_ref, w2_ref, gfin_ref,
                       o_ref, c_scr, n_scr, m_scr, ext_scr, mout_scr, *, chunk, tiles_per_seq):
    i = pl.program_id(0)

    @pl.when(i == 0)
    def _():
        mout_scr[...] = jnp.zeros(mout_scr.shape, mout_scr.dtype)

    @pl.when(i % tiles_per_seq == 0)
    def _():
        c_scr[...] = jnp.zeros(c_scr.shape, F32)
        n_scr[...] = jnp.zeros(n_scr.shape, F32)
        m_scr[...] = jnp.zeros(m_scr.shape, F32)
        ext_scr[0:SUBLANES, :] = jnp.zeros((SUBLANES, ext_scr.shape[1]), F32)

    def tail_stages():
        ya = jnp.dot(a_ref[...], wa_ref[...], preferred_element_type=F32)
        ym = jnp.dot(mout_scr[...], wm_ref[...], preferred_element_type=F32)
        gate_a = _sigmoid(ga_ref[...].astype(F32) + bm_ref[:, :D_MODEL])
        gate_m = _sigmoid(gm_ref[...].astype(F32) + bm_ref[:, D_MODEL:])
        merged = (gate_a * ya + gate_m * ym).astype(BF16)
        yield
        x1 = x_ref[...] + jnp.dot(merged, wo_ref[...], preferred_element_type=F32)
        hm = _rms(x1, gmlp_ref[...]).astype(BF16)
        acc = x1
        for c in range(D_FF // FF_BLOCK):
            yield
            cols = slice(c * FF_BLOCK, (c + 1) * FF_BLOCK)
            u = jnp.maximum(jnp.dot(hm, w1_ref[:, cols], preferred_element_type=F32), 0.0)
            acc = acc + jnp.dot((u * u).astype(BF16), w2_ref[cols, :], preferred_element_type=F32)
        o_ref[...] = _rms(acc, gfin_ref[...])

    def mlstm_stages():
        for r0 in range(0, mout_scr.shape[0], chunk):
            yield from _mlstm_chunk(r0, chunk, qk_ref, v_ref, og_ref, grow_ref, gcol_ref, cw_ref,
                                    cb_ref, brow_ref, bcol_ref, ng_ref, c_scr, n_scr, m_scr,
                                    ext_scr, mout_scr)

    pending = [mlstm_stages(), tail_stages()]
    while pending:
        for stage in list(pending):
            if next(stage, StopIteration) is StopIteration:
                pending.remove(stage)


def _mlstm_tail(proj3, proj, g_rows, g_cols3, conv_w, conv_b, b_row, b_col, norm_g,
                x2, a2, b_merge, wa, wm, wo, g_mlp, w1, w2, g_fin, batch, seq, tm, chunk):
    tokens = batch * seq
    n_tiles = tokens // tm
    tps = seq // tm
    kern = functools.partial(_mlstm_tail_kernel, chunk=chunk, tiles_per_seq=tps)

    def cur(i):
        return jnp.minimum(i, n_tiles - 1)

    def prev(i):
        return jnp.maximum(i - 1, 0)

    def resident(arr):
        return pl.BlockSpec(arr.shape, lambda i: (0,) * arr.ndim, pipeline_mode=pl.Buffered(1))

    def seq_block(col):
        return pl.BlockSpec((None, tm, D_MODEL), lambda i: (cur(i) // tps, cur(i) % tps, col))

    return pl.pallas_call(
        kern,
        grid=(n_tiles + 1,),
        in_specs=[
            seq_block(COL_ML_QK),
            seq_block(COL_ML_V),
            seq_block(COL_ML_O),
            pl.BlockSpec((None, 2 * ML_HEADS, tm), lambda i: (cur(i) // tps, 0, cur(i) % tps)),
            pl.BlockSpec((None, tm, LANES), lambda i: (cur(i) // tps, cur(i) % tps, 0)),
            resident(conv_w), resident(conv_b), resident(b_row), resident(b_col), resident(norm_g),
            pl.BlockSpec((tm, D_MODEL), lambda i: (prev(i), 0)),
            pl.BlockSpec((tm, D_MODEL), lambda i: (prev(i), 0)),
            pl.BlockSpec((tm, D_MODEL), lambda i: (prev(i), COL_MG_A)),
            pl.BlockSpec((tm, D_MODEL), lambda i: (prev(i), COL_MG_M)),
            resident(b_merge), resident(wa), resident(wm), resident(wo), resident(g_mlp),
            resident(w1), resident(w2), resident(g_fin),
        ],
        out_specs=pl.BlockSpec((tm, D_MODEL), lambda i: (prev(i), 0)),
        out_shape=jax.ShapeDtypeStruct((tokens, D_MODEL), F32),
        scratch_shapes=[
            pltpu.VMEM((ML_HEADS, ML_QK_DIM, ML_V_DIM), F32),
            pltpu.VMEM((ML_HEADS, 1, ML_QK_DIM), F32),
            pltpu.VMEM((ML_HEADS, 1, 1), F32),
            pltpu.VMEM((chunk + 2 * SUBLANES, D_MODEL), F32),
            pltpu.VMEM((tm, D_MODEL), BF16),
        ],
        compiler_params=pltpu.CompilerParams(
            dimension_semantics=("arbitrary",), vmem_limit_bytes=VMEM_LIMIT),
    )(proj3, proj3, proj3, g_rows, g_cols3, conv_w, conv_b, b_row, b_col, norm_g,
      x2, a2, proj, proj, b_merge, wa, wm, wo, g_mlp, w1, w2, g_fin)


def kernel(x, norm_mix_g, w_in, b_gates, conv_w, conv_b, lam, da_norm_g, ml_norm_g, b_merge,
           w_branch_a, w_branch_m, w_out, norm_mlp_g, w_ff1, w_ff2, norm_final_g):
    batch, seq, d_model = x.shape
    assert d_model == D_MODEL and w_in.shape == (1, D_MODEL, D_IN)
    assert seq % max(ROW_TILE, ATTN_TILE) == 0 and ROW_TILE % MLSTM_CHUNK == 0
    tokens = batch * seq
    x2 = x.reshape(tokens, D_MODEL)

    w_bf = w_in[0].astype(BF16)
    w_tail = w_bf[:, OFF_ML_O:]

    proj, vt, ifg, ifg_t, k_sq = _in_proj(x2, norm_mix_g, w_bf, w_tail, batch, seq, ROW_TILE)
    kmax = jnp.sqrt(jnp.max(k_sq.reshape(batch, -1, LANES), axis=1))
    proj3 = proj.reshape(batch, seq, N_PROJ_BLOCKS * D_MODEL)

    slopes = 2.0 ** (-8.0 * jnp.arange(1, DA_HEADS + 1, dtype=F32) / DA_HEADS)
    a_out = _attention(slopes, kmax, lam[0], proj3, vt, da_norm_g, batch, seq,
                       ATTN_TILE, ATTN_HEADS_PER_STEP)

    ifg3 = ifg.reshape(batch, seq, LANES)
    b_row = b_gates[0].reshape(2 * ML_HEADS, 1)
    b_col = jnp.pad(b_gates, ((0, 0), (0, LANES - 2 * ML_HEADS)))
    out = _mlstm_tail(proj3, proj, ifg_t, ifg3, conv_w[0], conv_b, b_row, b_col, ml_norm_g,
                      x2, a_out.reshape(tokens, D_MODEL), b_merge,
                      w_branch_a[0].astype(BF16), w_branch_m[0].astype(BF16), w_out[0].astype(BF16),
                      norm_mlp_g, w_ff1[0].astype(BF16), w_ff2[0].astype(BF16),
                      norm_final_g.reshape(1, D_MODEL), batch, seq, ROW_TILE, MLSTM_CHUNK)
    return out.reshape(batch, seq, D_MODEL)
```

```python
import functools
import math

import jax
import jax.numpy as jnp
from jax import lax
from jax.experimental import pallas as pl
from jax.experimental.pallas import tpu as pltpu

F32 = jnp.float32
BF16 = jnp.bfloat16

D_MODEL = 1024
DA_HEADS = 8
DA_HEAD_DIM = 64
DA_V_DIM = 2 * DA_HEAD_DIM
ML_HEADS = 4
ML_V_DIM = D_MODEL // ML_HEADS
ML_QK_DIM = ML_V_DIM // 2
ML_QK_WIDTH = ML_HEADS * ML_QK_DIM
CONV_WIDTH = 4
D_FF = 4 * D_MODEL
EPS = 1e-6
LAM_INIT = 0.8 - 0.6 * math.exp(-0.3 * 0)
NEG_BIG = -1e30
LOG2E = math.log2(math.e)
ONES_ROWS = 16
FF_BLOCK = 512
BOUND_SLACK = 1.02
MAX_SHIFT_SPREAD = 100.0
BOUNDED_GROUP = 4

LANES = 128
SUBLANES = 8
VMEM_LIMIT = 60 * 1024 * 1024

ROW_TILE = 512
ATTN_TILE = 512
ATTN_HEADS_PER_STEP = 4
MLSTM_CHUNK = 256

COL_DA_Q, COL_DA_K, COL_ML_QK, COL_ML_V, COL_ML_O, COL_MG_A, COL_MG_M = range(7)
N_PROJ_BLOCKS = 7

OFF_DA_Q = 0
OFF_DA_K = OFF_DA_Q + D_MODEL
OFF_DA_V = OFF_DA_K + D_MODEL
OFF_ML_Q = OFF_DA_V + D_MODEL
OFF_ML_V = OFF_ML_Q + 2 * ML_QK_WIDTH
OFF_ML_IF = OFF_ML_V + D_MODEL
OFF_ML_O = OFF_ML_IF + 2 * ML_HEADS
D_IN = OFF_ML_O + 3 * D_MODEL

NT_DIMS = (((1,), (1,)), ((), ()))
TN_DIMS = (((0,), (0,)), ((), ()))
TT_DIMS = (((0,), (1,)), ((), ()))


def _sigmoid(x):
    return 1.0 / (1.0 + jnp.exp(-x))


def _log_sigmoid(x):
    return jnp.minimum(x, 0.0) - jnp.log(1.0 + jnp.exp(-jnp.abs(x)))


def _in_proj_kernel(x_ref, g_ref, w_ref, wtail_ref, proj_ref, vt_ref, if_ref, ift_ref, kn_ref):
    x = x_ref[...]
    hb = (x * lax.rsqrt(jnp.mean(x * x, axis=-1, keepdims=True) + EPS) * g_ref[...]).astype(BF16)

    def project(w_cols):
        return jnp.dot(hb, w_cols, preferred_element_type=F32)

    def project_t(w_cols):
        return lax.dot_general(w_cols, hb, TT_DIMS, preferred_element_type=F32)

    proj_ref[:, 0:D_MODEL] = (project(w_ref[:, OFF_DA_Q:OFF_DA_K])
                              * (DA_HEAD_DIM ** -0.5 * LOG2E)).astype(BF16)
    k = project(w_ref[:, OFF_DA_K:OFF_DA_V])
    proj_ref[:, D_MODEL:2 * D_MODEL] = k.astype(BF16)
    r_idx = lax.broadcasted_iota(jnp.int32, (D_MODEL, LANES), 0)
    c_idx = lax.broadcasted_iota(jnp.int32, (D_MODEL, LANES), 1)
    select = jnp.where(r_idx // DA_HEAD_DIM == c_idx, 1.0, 0.0).astype(BF16)
    k_sq = jnp.dot((k * k).astype(BF16), select, preferred_element_type=F32)
    kn_ref[...] = jnp.broadcast_to(jnp.max(k_sq, axis=0, keepdims=True), kn_ref.shape)
    vt_ref[...] = project_t(w_ref[:, OFF_DA_V:OFF_ML_Q]).astype(BF16)
    for blk in range(2):
        src = OFF_ML_Q + blk * D_MODEL
        dst = (COL_ML_QK + blk) * D_MODEL
        proj_ref[:, dst:dst + D_MODEL] = project(w_ref[:, src:src + D_MODEL]).astype(BF16)
    w_if = w_ref[:, OFF_ML_IF:OFF_ML_IF + LANES]
    if_ref[...] = project(w_if)
    ift_ref[...] = project_t(w_if)
    for blk in range(3):
        dst = (COL_ML_O + blk) * D_MODEL
        proj_ref[:, dst:dst + D_MODEL] = project(
            wtail_ref[:, blk * D_MODEL:(blk + 1) * D_MODEL]).astype(BF16)


def _in_proj(x2, g, w_bf, w_tail, batch, seq, tm):
    tokens = batch * seq
    nsb = seq // tm

    def resident(arr):
        return pl.BlockSpec(arr.shape, lambda i: (0, 0), pipeline_mode=pl.Buffered(1))

    return pl.pallas_call(
        _in_proj_kernel,
        grid=(tokens // tm,),
        in_specs=[
            pl.BlockSpec((tm, D_MODEL), lambda i: (i, 0)),
            resident(g),
            resident(w_bf),
            resident(w_tail),
        ],
        out_specs=[
            pl.BlockSpec((tm, N_PROJ_BLOCKS * D_MODEL), lambda i: (i, 0)),
            pl.BlockSpec((None, D_MODEL, tm), lambda i: (i // nsb, 0, i % nsb)),
            pl.BlockSpec((tm, LANES), lambda i: (i, 0)),
            pl.BlockSpec((None, LANES, tm), lambda i: (i // nsb, 0, i % nsb)),
            pl.BlockSpec((SUBLANES, LANES), lambda i: (i, 0)),
        ],
        out_shape=[
            jax.ShapeDtypeStruct((tokens, N_PROJ_BLOCKS * D_MODEL), BF16),
            jax.ShapeDtypeStruct((batch, D_MODEL, seq), BF16),
            jax.ShapeDtypeStruct((tokens, LANES), F32),
            jax.ShapeDtypeStruct((batch, LANES, seq), F32),
            jax.ShapeDtypeStruct((tokens // tm * SUBLANES, LANES), F32),
        ],
        compiler_params=pltpu.CompilerParams(
            dimension_semantics=("parallel",), vmem_limit_bytes=VMEM_LIMIT),
    )(x2, g, w_bf, w_tail)


def _attn_kernel(slopes_ref, kmax_ref, lam_ref, q_ref, k_ref, vt_ref, g_ref, o_ref,
                 qq_scr, pos_scr, s_scr, p_scr, acc_scr, m_scr, l_scr, *, tile, heads):
    batch_idx = pl.program_id(0)
    group = pl.program_id(1)
    qi = pl.program_id(2)
    kt_size = tile // 2
    width = 2 * tile
    slope2 = [slopes_ref[group * heads + hh] * LOG2E for hh in range(heads)]

    def head_cols(hh):
        return slice(hh * DA_V_DIM, (hh + 1) * DA_V_DIM)

    key_off = lax.broadcasted_iota(jnp.int32, pos_scr.shape, 0).astype(F32)
    lane = lax.broadcasted_iota(jnp.int32, pos_scr.shape, 1)
    pos_scr[...] = jnp.where(lane < 2, key_off, 0.0).astype(BF16)
    q_lane = lax.broadcasted_iota(jnp.int32, (1, width), 1)
    q_off = jnp.where(q_lane >= tile, q_lane - tile, q_lane).astype(F32)
    spread = jnp.zeros((1, 1), F32)
    for hh in range(heads):
        qt = q_ref[:, head_cols(hh)].T
        d_idx = lax.broadcasted_iota(jnp.int32, qt.shape, 0)
        zero = jnp.zeros_like(qt)
        q0 = jnp.where(d_idx < DA_HEAD_DIM, qt, zero)
        q1 = jnp.where(d_idx >= DA_HEAD_DIM, qt, zero)
        qq_scr[hh, 0:DA_V_DIM, :width] = jnp.concatenate([q0, q1], axis=1)
        qt = qt.astype(F32)
        slope_vec = jnp.full((ONES_ROWS, width), slope2[hh], F32)
        slope_hi = slope_vec.astype(BF16).astype(F32)
        feat = lax.broadcasted_iota(jnp.int32, slope_vec.shape, 0)
        qq_scr[hh, DA_V_DIM:DA_V_DIM + ONES_ROWS, :width] = jnp.where(
            feat == 0, slope_hi, jnp.where(feat == 1, slope_vec - slope_hi, 0.0)).astype(BF16)
        qq_scr[hh, DA_V_DIM + ONES_ROWS:, :width] = jnp.zeros((DA_V_DIM - ONES_ROWS, width), BF16)
        acc_scr[hh, :, :width] = jnp.zeros((acc_scr.shape[1], width), F32)
        q_sq = qt * qt
        q_norm = jnp.sqrt(jnp.concatenate(
            [jnp.sum(q_sq[:DA_HEAD_DIM], axis=0, keepdims=True),
             jnp.sum(q_sq[DA_HEAD_DIM:], axis=0, keepdims=True)], axis=1))
        head = group * heads + hh
        k_max = jnp.where(q_lane < tile, kmax_ref[batch_idx, 2 * head],
                          kmax_ref[batch_idx, 2 * head + 1])
        qk_bound = q_norm * k_max * BOUND_SLACK + 1.0
        spread = jnp.maximum(spread, jnp.max(qk_bound, axis=1, keepdims=True))
        m_scr[hh] = qk_bound + slope2[hh] * q_off
    bounded_ok = jnp.max(spread) * 2.0 < MAX_SHIFT_SPREAD

    def lanes(blk):
        return slice(blk * kt_size, (blk + 1) * kt_size)

    n_blocks = width // kt_size
    upper_blocks = (1, 3)

    def key_rows(hh, t):
        start = pl.multiple_of(t * kt_size, kt_size)
        return jnp.concatenate([k_ref[pl.ds(start, kt_size), head_cols(hh)], pos_scr[...]], axis=1)

    def values_t(hh, t, n_tiles=1, with_ones=True):
        start = pl.multiple_of(jnp.maximum(t, 0) * kt_size, kt_size)
        vt = vt_ref[head_cols(hh), pl.ds(start, n_tiles * kt_size)]
        if not with_ones:
            return vt
        return jnp.concatenate([vt, jnp.ones((ONES_ROWS, n_tiles * kt_size), BF16)], axis=0)

    kk = lax.broadcasted_iota(jnp.int32, (kt_size, width), 0)
    qpos = lax.broadcasted_iota(jnp.int32, (kt_size, width), 1)
    qpos = jnp.where(qpos >= tile, qpos - tile, qpos)
    tri = (lax.broadcasted_iota(jnp.int32, (kt_size, kt_size), 0)
           <= lax.broadcasted_iota(jnp.int32, (kt_size, kt_size), 1))

    lam = lam_ref[...]
    lam_full = (jnp.exp(jnp.sum(lam[0:1] * lam[1:2], axis=1, keepdims=True))
                - jnp.exp(jnp.sum(lam[2:3] * lam[3:4], axis=1, keepdims=True)) + LAM_INIT)

    def finish(hh, acc, l):
        inv_l = 1.0 / l
        o = (acc[:, :tile] * inv_l[:, :tile]
             - acc[:, tile:] * (lam_full * inv_l[:, tile:]))
        o = o * lax.rsqrt(jnp.mean(o * o, axis=0, keepdims=True) + EPS) * (1.0 - LAM_INIT)
        o_ref[:, head_cols(hh)] = (o.T * g_ref[:, head_cols(hh)]).astype(o_ref.dtype)

    def interleave(stage_iters):
        pending = list(stage_iters)
        while pending:
            for it in list(pending):
                if next(it, StopIteration) is StopIteration:
                    pending.remove(it)

    @pl.when(bounded_ok)
    def _():
        for hh in range(heads):
            l_scr[hh] = jnp.zeros(l_scr.shape[1:], F32)

        def tile_group_stages(hh, t0, n_tiles, diagonal=False):
            shift = m_scr[hh]
            for r in range(n_tiles):
                t = t0 + r
                c = slope2[hh] * (t * kt_size - qi * tile).astype(F32)
                kt = key_rows(hh, t)
                if diagonal and r == 1:
                    for blk in upper_blocks:
                        s = jnp.dot(kt, qq_scr[hh, :, lanes(blk)], preferred_element_type=F32)
                        s = jnp.where(tri, s, NEG_BIG)
                        p = jnp.exp2(s + (c - shift[:, lanes(blk)]))
                        l_scr[hh, :, lanes(blk)] += jnp.sum(p, axis=0, keepdims=True)
                        p_scr[BOUNDED_GROUP * hh + r, :, lanes(blk)] = p.astype(BF16)
                else:
                    s = jnp.dot(kt, qq_scr[hh, :, :width], preferred_element_type=F32)
                    if diagonal:
                        s = jnp.where(kk <= qpos, s, NEG_BIG)
                    p = jnp.exp2(s + (c - shift))
                    l_scr[hh] += jnp.sum(p, axis=0, keepdims=True)
                    p_scr[BOUNDED_GROUP * hh + r, :, :width] = p.astype(BF16)
                yield
            if diagonal:
                acc_scr[hh, :DA_V_DIM, :width] += jnp.dot(
                    values_t(hh, t0, with_ones=False), p_scr[BOUNDED_GROUP * hh, :, :width],
                    preferred_element_type=F32)
                vt = values_t(hh, t0 + 1, with_ones=False)
                for blk in upper_blocks:
                    acc_scr[hh, :DA_V_DIM, lanes(blk)] += jnp.dot(
                        vt, p_scr[BOUNDED_GROUP * hh + 1, :, lanes(blk)], preferred_element_type=F32)
            else:
                first = BOUNDED_GROUP * hh
                p_all = p_scr[first:first + n_tiles, :, :width].reshape(n_tiles * kt_size, width)
                acc_scr[hh, :DA_V_DIM, :width] += jnp.dot(
                    values_t(hh, t0, n_tiles, with_ones=False), p_all, preferred_element_type=F32)

        def body(i, carry):
            interleave([tile_group_stages(hh, BOUNDED_GROUP * i, BOUNDED_GROUP)
                        for hh in range(heads)])
            return carry

        lax.fori_loop(0, (2 * qi) // BOUNDED_GROUP, body, 0)

        @pl.when(qi % 2 == 1)
        def _():
            interleave([tile_group_stages(hh, 2 * (qi - 1), 2) for hh in range(heads)])

        interleave([tile_group_stages(hh, 2 * qi, 2, diagonal=True) for hh in range(heads)])
        for hh in range(heads):
            finish(hh, acc_scr[hh, :DA_V_DIM, :width], l_scr[hh])

    @pl.when(jnp.logical_not(bounded_ok))
    def _():
        for hh in range(heads):
            m_scr[hh] = jnp.full(m_scr.shape[1:], NEG_BIG, F32)
            p_scr[2 * hh + 1, :, :width] = jnp.zeros((kt_size, width), BF16)

        def scores(hh, t, slot, blocks=None):
            kt = key_rows(hh, t)
            if blocks is None:
                s_scr[2 * hh + slot, :, :width] = jnp.dot(
                    kt, qq_scr[hh, :, :width], preferred_element_type=F32)
            else:
                for blk in blocks:
                    s_scr[2 * hh + slot, :, lanes(blk)] = jnp.dot(
                        kt, qq_scr[hh, :, lanes(blk)], preferred_element_type=F32)

        def weighted_values(hh, t, slot):
            return jnp.dot(values_t(hh, t), p_scr[2 * hh + slot, :, :width],
                           preferred_element_type=F32)

        def step(hh, t, slot, mask=None, prefetch_blocks=None):
            scores(hh, t + 1, 1 - slot, prefetch_blocks)
            c = slope2[hh] * (t * kt_size - qi * tile).astype(F32)
            s = s_scr[2 * hh + slot, :, :width]
            if mask is not None:
                s = jnp.where(mask, s, NEG_BIG)
            m_old = m_scr[hh]
            m_new = jnp.maximum(m_old, jnp.max(s, axis=0, keepdims=True) + c)
            p = jnp.exp2(s - (m_new - c))
            alpha = jnp.exp2(m_old - m_new)
            p_scr[2 * hh + slot, :, :width] = p.astype(BF16)
            m_scr[hh] = m_new
            acc_scr[hh, :, :width] = alpha * (acc_scr[hh, :, :width]
                                              + weighted_values(hh, t - 1, 1 - slot))

        for hh in range(heads):
            scores(hh, 0, 0)

        def body(i, carry):
            for slot in range(2):
                for hh in range(heads):
                    step(hh, 2 * i + slot, slot)
            return carry

        lax.fori_loop(0, qi, body, 0)

        for hh in range(heads):
            step(hh, 2 * qi, 0, mask=kk <= qpos, prefetch_blocks=upper_blocks)

        def last_tile(hh):
            t = 2 * qi + 1
            c = slope2[hh] * kt_size
            acc = acc_scr[hh, :, :width] + weighted_values(hh, t - 1, 0)
            vt = values_t(hh, t)
            parts = []
            for blk in range(n_blocks):
                part = acc[:, lanes(blk)]
                if blk in upper_blocks:
                    s = jnp.where(tri, s_scr[2 * hh + 1, :, lanes(blk)], NEG_BIG)
                    m_old = m_scr[hh, :, lanes(blk)]
                    m_new = jnp.maximum(m_old, jnp.max(s, axis=0, keepdims=True) + c)
                    p = jnp.exp2(s - (m_new - c)).astype(BF16)
                    part = (jnp.exp2(m_old - m_new) * part
                            + jnp.dot(vt, p, preferred_element_type=F32))
                parts.append(part)
            return jnp.concatenate(parts, axis=1)

        for hh in range(heads):
            acc = last_tile(hh)
            finish(hh, acc[:DA_V_DIM, :], acc[DA_V_DIM:DA_V_DIM + 1, :])


def _attention(slopes, kmax, lam, proj3, vt, g, batch, seq, tile, heads):
    kern = functools.partial(_attn_kernel, tile=tile, heads=heads)
    width = heads * DA_V_DIM
    groups = DA_HEADS // heads
    pitch = 2 * tile + LANES
    return pl.pallas_call(
        kern,
        grid=(batch, groups, seq // tile),
        in_specs=[
            pl.BlockSpec(memory_space=pltpu.SMEM),
            pl.BlockSpec(memory_space=pltpu.SMEM),
            pl.BlockSpec((4, DA_HEAD_DIM), lambda b, h, i: (0, 0)),
            pl.BlockSpec((None, tile, width), lambda b, h, i: (b, i, COL_DA_Q * groups + h)),
            pl.BlockSpec((None, seq, width), lambda b, h, i: (b, 0, COL_DA_K * groups + h)),
            pl.BlockSpec((None, width, seq), lambda b, h, i: (b, h, 0)),
            pl.BlockSpec((1, width), lambda b, h, i: (0, h)),
        ],
        out_specs=pl.BlockSpec((None, tile, width), lambda b, h, i: (b, i, h)),
        out_shape=jax.ShapeDtypeStruct((batch, seq, D_MODEL), BF16),
        scratch_shapes=[
            pltpu.VMEM((heads, 2 * DA_V_DIM, pitch), BF16),
            pltpu.VMEM((tile // 2, DA_V_DIM), BF16),
            pltpu.VMEM((2 * heads, tile // 2, pitch), F32),
            pltpu.VMEM((BOUNDED_GROUP * heads, tile // 2, pitch), BF16),
            pltpu.VMEM((heads, DA_V_DIM + ONES_ROWS, pitch), F32),
            pltpu.VMEM((heads, 1, 2 * tile), F32),
            pltpu.VMEM((heads, 1, 2 * tile), F32),
        ],
        compiler_params=pltpu.CompilerParams(
            dimension_semantics=("parallel", "parallel", "arbitrary"),
            vmem_limit_bytes=VMEM_LIMIT),
    )(slopes, kmax, lam, proj3, proj3, vt, g)


def _rms(x, g):
    return x * lax.rsqrt(jnp.mean(x * x, axis=-1, keepdims=True) + EPS) * g


def _mlstm_chunk(r0, chunk, qk_ref, v_ref, og_ref, grow_ref, gcol_ref, cw_ref, cb_ref, brow_ref,
                 bcol_ref, ng_ref, c_scr, n_scr, m_scr, ext_scr, mout_scr):
    pad = SUBLANES
    rows = slice(r0, r0 + chunk)

    ext_scr[pad:pad + chunk, :] = qk_ref[rows, :].astype(F32)
    conv = cb_ref[...] + cw_ref[CONV_WIDTH - 1:CONV_WIDTH, :] * ext_scr[pad:pad + chunk, :]
    for tap in range(1, CONV_WIDTH):
        conv = conv + (cw_ref[CONV_WIDTH - 1 - tap:CONV_WIDTH - tap, :]
                       * ext_scr[pad - tap:pad - tap + chunk, :])
    ext_scr[0:pad, :] = ext_scr[chunk:chunk + pad, :]
    qk = conv * _sigmoid(conv)

    g_rows = grow_ref[:, rows] + brow_ref[...]
    g_cols = gcol_ref[rows, :] + bcol_ref[...]
    r_idx = lax.broadcasted_iota(jnp.int32, (chunk, chunk), 0)
    c_idx = lax.broadcasted_iota(jnp.int32, (chunk, chunk), 1)
    causal = r_idx >= c_idx
    tril = jnp.where(causal, 1.0, 0.0).astype(BF16)
    triu = jnp.where(r_idx <= c_idx, 1.0, 0.0).astype(BF16)

    def bf16_terms(x):
        hi = x.astype(BF16)
        mid = (x - hi.astype(F32)).astype(BF16)
        lo = (x - hi.astype(F32) - mid.astype(F32)).astype(BF16)
        return hi, mid, lo

    b_rows = sum(jnp.dot(term, triu, preferred_element_type=F32)
                 for term in bf16_terms(_log_sigmoid(g_rows)))
    b_cols = sum(jnp.dot(tril, term, preferred_element_type=F32)
                 for term in bf16_terms(_log_sigmoid(g_cols)))

    q_scale = ML_QK_DIM ** -0.5
    for hd in range(ML_HEADS):
        yield
        qf = qk[:, hd * ML_QK_DIM:(hd + 1) * ML_QK_DIM] * q_scale
        kf = qk[:, ML_QK_WIDTH + hd * ML_QK_DIM:ML_QK_WIDTH + (hd + 1) * ML_QK_DIM]
        qb = qf.astype(BF16)
        sl = slice(hd * ML_V_DIM, (hd + 1) * ML_V_DIM)
        vb = v_ref[rows, sl]
        fcol = ML_HEADS + hd
        bt = b_cols[:, fcol:fcol + 1]
        bs = b_rows[fcol:fcol + 1, :]
        i_row = g_rows[hd:hd + 1, :]
        i_col = g_cols[:, hd:hd + 1]
        m_prev = m_scr[hd]
        c_prev = c_scr[hd]
        n_prev = n_scr[hd]

        logd = jnp.where(causal, bt - bs + i_row, NEG_BIG)
        inter = bt + m_prev
        m_t = jnp.maximum(inter, jnp.max(logd, axis=1, keepdims=True))
        dmat = jnp.exp(logd - m_t)
        sc = lax.dot_general(qb, kf.astype(BF16), NT_DIMS, preferred_element_type=F32) * dmat
        w_inter = jnp.exp(inter - m_t)
        num = (w_inter * jnp.dot(qb, c_prev.astype(BF16), preferred_element_type=F32)
               + jnp.dot(sc.astype(BF16), vb, preferred_element_type=F32))
        den = (w_inter * jnp.sum(qf * n_prev, axis=1, keepdims=True)
               + jnp.sum(sc, axis=1, keepdims=True))
        hh = num / jnp.maximum(jnp.abs(den), jnp.exp(-m_t))

        g_last = bt[chunk - 1:chunk, :]
        log_w = g_last - bt + i_col
        m_new = jnp.maximum(g_last + m_prev, jnp.max(log_w, axis=0, keepdims=True))
        kw = kf * jnp.exp(log_w - m_new)
        decay = jnp.exp(g_last + m_prev - m_new)
        c_scr[hd] = decay * c_prev + lax.dot_general(
            kw.astype(BF16), vb, TN_DIMS, preferred_element_type=F32)
        n_scr[hd] = decay * n_prev + jnp.sum(kw, axis=0, keepdims=True)
        m_scr[hd] = m_new

        hn = hh * lax.rsqrt(jnp.mean(hh * hh, axis=1, keepdims=True) + EPS) * ng_ref[:, sl]
        mout_scr[rows, sl] = (_sigmoid(og_ref[rows, sl].astype(F32)) * hn).astype(mout_scr.dtype)


def _mlstm_tail_kernel(qk_ref, v_ref, og_ref, grow_ref, gcol_ref, cw_ref, cb_ref, brow_ref,
                       bcol_ref, ng_ref,
                       x_ref, a_ref, ga_ref, gm_ref, bm_ref, wa_ref, wm_ref, wo_ref,
                       gmlp_ref, w1_ref, w2_ref, gfin_ref,
                       o_ref, c_scr, n_scr, m_scr, ext_scr, mout_scr, *, chunk, tiles_per_seq):
    i = pl.program_id(0)

    @pl.when(i == 0)
    def _():
        mout_scr[...] = jnp.zeros(mout_scr.shape, mout_scr.dtype)

    @pl.when(i % tiles_per_seq == 0)
    def _():
        c_scr[...] = jnp.zeros(c_scr.shape, F32)
        n_scr[...] = jnp.zeros(n_scr.shape, F32)
        m_scr[...] = jnp.zeros(m_scr.shape, F32)
        ext_scr[0:SUBLANES, :] = jnp.zeros((SUBLANES, ext_scr.shape[1]), F32)

    def tail_stages():
        ya = jnp.dot(a_ref[...], wa_ref[...], preferred_element_type=F32)
        ym = jnp.dot(mout_scr[...], wm_ref[...], preferred_element_type=F32)
        gate_a = _sigmoid(ga_ref[...].astype(F32) + bm_ref[:, :D_MODEL])
        gate_m = _sigmoid(gm_ref[...].astype(F32) + bm_ref[:, D_MODEL:])
        merged = (gate_a * ya + gate_m * ym).astype(BF16)
        yield
        x1 = x_ref[...] + jnp.dot(merged, wo_ref[...], preferred_element_type=F32)
        hm = _rms(x1, gmlp_ref[...]).astype(BF16)
        acc = x1
        for c in range(D_FF // FF_BLOCK):
            yield
            cols = slice(c * FF_BLOCK, (c + 1) * FF_BLOCK)
            u = jnp.maximum(jnp.dot(hm, w1_ref[:, cols], preferred_element_type=F32), 0.0)
            acc = acc + jnp.dot((u * u).astype(BF16), w2_ref[cols, :], preferred_element_type=F32)
        o_ref[...] = _rms(acc, gfin_ref[...])

    def mlstm_stages():
        for r0 in range(0, mout_scr.shape[0], chunk):
            yield from _mlstm_chunk(r0, chunk, qk_ref, v_ref, og_ref, grow_ref, gcol_ref, cw_ref,
                                    cb_ref, brow_ref, bcol_ref, ng_ref, c_scr, n_scr, m_scr,
                                    ext_scr, mout_scr)

    pending = [mlstm_stages(), tail_stages()]
    while pending:
        for stage in list(pending):
            if next(stage, StopIteration) is StopIteration:
                pending.remove(stage)


def _mlstm_tail(proj3, proj, g_rows, g_cols3, conv_w, conv_b, b_row, b_col, norm_g,
                x2, a2, b_merge, wa, wm, wo, g_mlp, w1, w2, g_fin, batch, seq, tm, chunk):
    tokens = batch * seq
    n_tiles = tokens // tm
    tps = seq // tm
    kern = functools.partial(_mlstm_tail_kernel, chunk=chunk, tiles_per_seq=tps)

    def cur(i):
        return jnp.minimum(i, n_tiles - 1)

    def prev(i):
        return jnp.maximum(i - 1, 0)

    def resident(arr):
        return pl.BlockSpec(arr.shape, lambda i: (0,) * arr.ndim, pipeline_mode=pl.Buffered(1))

    def seq_block(col):
        return pl.BlockSpec((None, tm, D_MODEL), lambda i: (cur(i) // tps, cur(i) % tps, col))

    return pl.pallas_call(
        kern,
        grid=(n_tiles + 1,),
        in_specs=[
            seq_block(COL_ML_QK),
            seq_block(COL_ML_V),
            seq_block(COL_ML_O),
            pl.BlockSpec((None, 2 * ML_HEADS, tm), lambda i: (cur(i) // tps, 0, cur(i) % tps)),
            pl.BlockSpec((None, tm, LANES), lambda i: (cur(i) // tps, cur(i) % tps, 0)),
            resident(conv_w), resident(conv_b), resident(b_row), resident(b_col), resident(norm_g),
            pl.BlockSpec((tm, D_MODEL), lambda i: (prev(i), 0)),
            pl.BlockSpec((tm, D_MODEL), lambda i: (prev(i), 0)),
            pl.BlockSpec((tm, D_MODEL), lambda i: (prev(i), COL_MG_A)),
            pl.BlockSpec((tm, D_MODEL), lambda i: (prev(i), COL_MG_M)),
            resident(b_merge), resident(wa), resident(wm), resident(wo), resident(g_mlp),
            resident(w1), resident(w2), resident(g_fin),
        ],
        out_specs=pl.BlockSpec((tm, D_MODEL), lambda i: (prev(i), 0)),
        out_shape=jax.ShapeDtypeStruct((tokens, D_MODEL), F32),
        scratch_shapes=[
            pltpu.VMEM((ML_HEADS, ML_QK_DIM, ML_V_DIM), F32),
            pltpu.VMEM((ML_HEADS, 1, ML_QK_DIM), F32),
            pltpu.VMEM((ML_HEADS, 1, 1), F32),
            pltpu.VMEM((chunk + 2 * SUBLANES, D_MODEL), F32),
            pltpu.VMEM((tm, D_MODEL), BF16),
        ],
        compiler_params=pltpu.CompilerParams(
            dimension_semantics=("arbitrary",), vmem_limit_bytes=VMEM_LIMIT),
    )(proj3, proj3, proj3, g_rows, g_cols3, conv_w, conv_b, b_row, b_col, norm_g,
      x2, a2, proj, proj, b_merge, wa, wm, wo, g_mlp, w1, w2, g_fin)


def kernel(x, norm_mix_g, w_in, b_gates, conv_w, conv_b, lam, da_norm_g, ml_norm_g, b_merge,
           w_branch_a, w_branch_m, w_out, norm_mlp_g, w_ff1, w_ff2, norm_final_g):
    batch, seq, d_model = x.shape
    assert d_model == D_MODEL and w_in.shape == (1, D_MODEL, D_IN)
    assert seq % max(ROW_TILE, ATTN_TILE) == 0 and ROW_TILE % MLSTM_CHUNK == 0
    tokens = batch * seq
    x2 = x.reshape(tokens, D_MODEL)

    w_bf = w_in[0].astype(BF16)
    w_tail = w_bf[:, OFF_ML_O:]

    proj, vt, ifg, ifg_t, k_sq = _in_proj(x2, norm_mix_g, w_bf, w_tail, batch, seq, ROW_TILE)
    kmax = jnp.sqrt(jnp.max(k_sq.reshape(batch, -1, LANES), axis=1))
    proj3 = proj.reshape(batch, seq, N_PROJ_BLOCKS * D_MODEL)

    slopes = 2.0 ** (-8.0 * jnp.arange(1, DA_HEADS + 1, dtype=F32) / DA_HEADS)
    a_out = _attention(slopes, kmax, lam[0], proj3, vt, da_norm_g, batch, seq,
                       ATTN_TILE, ATTN_HEADS_PER_STEP)

    ifg3 = ifg.reshape(batch, seq, LANES)
    b_row = b_gates[0].reshape(2 * ML_HEADS, 1)
    b_col = jnp.pad(b_gates, ((0, 0), (0, LANES - 2 * ML_HEADS)))
    out = _mlstm_tail(proj3, proj, ifg_t, ifg3, conv_w[0], conv_b, b_row, b_col, ml_norm_g,
                      x2, a_out.reshape(tokens, D_MODEL), b_merge,
                      w_branch_a[0].astype(BF16), w_branch_m[0].astype(BF16), w_out[0].astype(BF16),
                      norm_mlp_g, w_ff1[0].astype(BF16), w_ff2[0].astype(BF16),
                      norm_final_g.reshape(1, D_MODEL), batch, seq, ROW_TILE, MLSTM_CHUNK)
    return out.reshape(batch, seq, D_MODEL)
```

```python
import functools
import math

import jax
import jax.numpy as jnp
from jax import lax
from jax.experimental import pallas as pl
from jax.experimental.pallas import tpu as pltpu

F32 = jnp.float32
BF16 = jnp.bfloat16

D_MODEL = 1024
DA_HEADS = 8
DA_HEAD_DIM = 64
DA_V_DIM = 2 * DA_HEAD_DIM
ML_HEADS = 4
ML_V_DIM = D_MODEL // ML_HEADS
ML_QK_DIM = ML_V_DIM // 2
ML_QK_WIDTH = ML_HEADS * ML_QK_DIM
CONV_WIDTH = 4
D_FF = 4 * D_MODEL
EPS = 1e-6
LAM_INIT = 0.8 - 0.6 * math.exp(-0.3 * 0)
NEG_BIG = -1e30
LOG2E = math.log2(math.e)
ONES_ROWS = 16
FF_BLOCK = 512
BOUND_SLACK = 1.02
MAX_SHIFT_SPREAD = 100.0
BOUNDED_GROUP = 4

LANES = 128
SUBLANES = 8
VMEM_LIMIT = 60 * 1024 * 1024

ROW_TILE = 512
ATTN_TILE = 512
ATTN_HEADS_PER_STEP = 4
MLSTM_CHUNK = 256

COL_DA_Q, COL_DA_K, COL_ML_QK, COL_ML_V, COL_ML_O, COL_MG_A, COL_MG_M = range(7)
N_PROJ_BLOCKS = 7

OFF_DA_Q = 0
OFF_DA_K = OFF_DA_Q + D_MODEL
OFF_DA_V = OFF_DA_K + D_MODEL
OFF_ML_Q = OFF_DA_V + D_MODEL
OFF_ML_V = OFF_ML_Q + 2 * ML_QK_WIDTH
OFF_ML_IF = OFF_ML_V + D_MODEL
OFF_ML_O = OFF_ML_IF + 2 * ML_HEADS
D_IN = OFF_ML_O + 3 * D_MODEL

NT_DIMS = (((1,), (1,)), ((), ()))
TN_DIMS = (((0,), (0,)), ((), ()))
TT_DIMS = (((0,), (1,)), ((), ()))


def _sigmoid(x):
    return 1.0 / (1.0 + jnp.exp(-x))


def _log_sigmoid(x):
    return jnp.minimum(x, 0.0) - jnp.log(1.0 + jnp.exp(-jnp.abs(x)))


def _in_proj_kernel(x_ref, g_ref, w_ref, wtail_ref, proj_ref, vt_ref, if_ref, ift_ref, kn_ref):
    x = x_ref[...]
    hb = (x * lax.rsqrt(jnp.mean(x * x, axis=-1, keepdims=True) + EPS) * g_ref[...]).astype(BF16)

    def project(w_cols):
        return jnp.dot(hb, w_cols, preferred_element_type=F32)

    def project_t(w_cols):
        return lax.dot_general(w_cols, hb, TT_DIMS, preferred_element_type=F32)

    proj_ref[:, 0:D_MODEL] = (project(w_ref[:, OFF_DA_Q:OFF_DA_K])
                              * (DA_HEAD_DIM ** -0.5 * LOG2E)).astype(BF16)
    k = project(w_ref[:, OFF_DA_K:OFF_DA_V])
    proj_ref[:, D_MODEL:2 * D_MODEL] = k.astype(BF16)
    r_idx = lax.broadcasted_iota(jnp.int32, (D_MODEL, LANES), 0)
    c_idx = lax.broadcasted_iota(jnp.int32, (D_MODEL, LANES), 1)
    select = jnp.where(r_idx // DA_HEAD_DIM == c_idx, 1.0, 0.0).astype(BF16)
    k_sq = jnp.dot((k * k).astype(BF16), select, preferred_element_type=F32)
    kn_ref[...] = jnp.broadcast_to(jnp.max(k_sq, axis=0, keepdims=True), kn_ref.shape)
    vt_ref[...] = project_t(w_ref[:, OFF_DA_V:OFF_ML_Q]).astype(BF16)
    for blk in range(2):
        src = OFF_ML_Q + blk * D_MODEL
        dst = (COL_ML_QK + blk) * D_MODEL
        proj_ref[:, dst:dst + D_MODEL] = project(w_ref[:, src:src + D_MODEL]).astype(BF16)
    w_if = w_ref[:, OFF_ML_IF:OFF_ML_IF + LANES]
    if_ref[...] = project(w_if)
    ift_ref[...] = project_t(w_if)
    for blk in range(3):
        dst = (COL_ML_O + blk) * D_MODEL
        proj_ref[:, dst:dst + D_MODEL] = project(
            wtail_ref[:, blk * D_MODEL:(blk + 1) * D_MODEL]).astype(BF16)


def _in_proj(x2, g, w_bf, w_tail, batch, seq, tm):
    tokens = batch * seq
    nsb = seq // tm

    def resident(arr):
        return pl.BlockSpec(arr.shape, lambda i: (0, 0), pipeline_mode=pl.Buffered(1))

    return pl.pallas_call(
        _in_proj_kernel,
        grid=(tokens // tm,),
        in_specs=[
            pl.BlockSpec((tm, D_MODEL), lambda i: (i, 0)),
            resident(g),
            resident(w_bf),
            resident(w_tail),
        ],
        out_specs=[
            pl.BlockSpec((tm, N_PROJ_BLOCKS * D_MODEL), lambda i: (i, 0)),
            pl.BlockSpec((None, D_MODEL, tm), lambda i: (i // nsb, 0, i % nsb)),
            pl.BlockSpec((tm, LANES), lambda i: (i, 0)),
            pl.BlockSpec((None, LANES, tm), lambda i: (i // nsb, 0, i % nsb)),
            pl.BlockSpec((SUBLANES, LANES), lambda i: (i, 0)),
        ],
        out_shape=[
            jax.ShapeDtypeStruct((tokens, N_PROJ_BLOCKS * D_MODEL), BF16),
            jax.ShapeDtypeStruct((batch, D_MODEL, seq), BF16),
            jax.ShapeDtypeStruct((tokens, LANES), F32),
            jax.ShapeDtypeStruct((batch, LANES, seq), F32),
            jax.ShapeDtypeStruct((tokens // tm * SUBLANES, LANES), F32),
        ],
        compiler_params=pltpu.CompilerParams(
            dimension_semantics=("parallel",), vmem_limit_bytes=VMEM_LIMIT),
    )(x2, g, w_bf, w_tail)


def _attn_kernel(slopes_ref, kmax_ref, lam_ref, q_ref, k_ref, vt_ref, g_ref, o_ref,
                 qq_scr, pos_scr, s_scr, p_scr, acc_scr, m_scr, l_scr, *, tile, heads):
    batch_idx = pl.program_id(0)
    group = pl.program_id(1)
    qi = pl.program_id(2)
    kt_size = tile // 2
    width = 2 * tile
    slope2 = [slopes_ref[group * heads + hh] * LOG2E for hh in range(heads)]

    def head_cols(hh):
        return slice(hh * DA_V_DIM, (hh + 1) * DA_V_DIM)

    key_off = lax.broadcasted_iota(jnp.int32, pos_scr.shape, 0).astype(F32)
    lane = lax.broadcasted_iota(jnp.int32, pos_scr.shape, 1)
    pos_scr[...] = jnp.where(lane < 2, key_off, 0.0).astype(BF16)
    q_lane = lax.broadcasted_iota(jnp.int32, (1, width), 1)
    q_off = jnp.where(q_lane >= tile, q_lane - tile, q_lane).astype(F32)
    spread = jnp.zeros((1, 1), F32)
    for hh in range(heads):
        qt = q_ref[:, head_cols(hh)].T
        d_idx = lax.broadcasted_iota(jnp.int32, qt.shape, 0)
        zero = jnp.zeros_like(qt)
        q0 = jnp.where(d_idx < DA_HEAD_DIM, qt, zero)
        q1 = jnp.where(d_idx >= DA_HEAD_DIM, qt, zero)
        qq_scr[hh, 0:DA_V_DIM, :width] = jnp.concatenate([q0, q1], axis=1)
        qt = qt.astype(F32)
        slope_vec = jnp.full((ONES_ROWS, width), slope2[hh], F32)
        slope_hi = slope_vec.astype(BF16).astype(F32)
        feat = lax.broadcasted_iota(jnp.int32, slope_vec.shape, 0)
        qq_scr[hh, DA_V_DIM:DA_V_DIM + ONES_ROWS, :width] = jnp.where(
            feat == 0, slope_hi, jnp.where(feat == 1, slope_vec - slope_hi, 0.0)).astype(BF16)
        qq_scr[hh, DA_V_DIM + ONES_ROWS:, :width] = jnp.zeros((DA_V_DIM - ONES_ROWS, width), BF16)
        acc_scr[hh, :, :width] = jnp.zeros((acc_scr.shape[1], width), F32)
        q_sq = qt * qt
        q_norm = jnp.sqrt(jnp.concatenate(
            [jnp.sum(q_sq[:DA_HEAD_DIM], axis=0, keepdims=True),
             jnp.sum(q_sq[DA_HEAD_DIM:], axis=0, keepdims=True)], axis=1))
        head = group * heads + hh
        k_max = jnp.where(q_lane < tile, kmax_ref[batch_idx, 2 * head],
                          kmax_ref[batch_idx, 2 * head + 1])
        qk_bound = q_norm * k_max * BOUND_SLACK + 1.0
        spread = jnp.maximum(spread, jnp.max(qk_bound, axis=1, keepdims=True))
        m_scr[hh] = qk_bound + slope2[hh] * q_off
    bounded_ok = jnp.max(spread) * 2.0 < MAX_SHIFT_SPREAD

    def lanes(blk):
        return slice(blk * kt_size, (blk + 1) * kt_size)

    n_blocks = width // kt_size
    upper_blocks = (1, 3)

    def key_rows(hh, t):
        start = pl.multiple_of(t * kt_size, kt_size)
        return jnp.concatenate([k_ref[pl.ds(start, kt_size), head_cols(hh)], pos_scr[...]], axis=1)

    def values_t(hh, t, n_tiles=1, with_ones=True):
        start = pl.multiple_of(jnp.maximum(t, 0) * kt_size, kt_size)
        vt = vt_ref[head_cols(hh), pl.ds(start, n_tiles * kt_size)]
        if not with_ones:
            return vt
        return jnp.concatenate([vt, jnp.ones((ONES_ROWS, n_tiles * kt_size), BF16)], axis=0)

    kk = lax.broadcasted_iota(jnp.int32, (kt_size, width), 0)
    qpos = lax.broadcasted_iota(jnp.int32, (kt_size, width), 1)
    qpos = jnp.where(qpos >= tile, qpos - tile, qpos)
    tri = (lax.broadcasted_iota(jnp.int32, (kt_size, kt_size), 0)
           <= lax.broadcasted_iota(jnp.int32, (kt_size, kt_size), 1))

    lam = lam_ref[...]
    lam_full = (jnp.exp(jnp.sum(lam[0:1] * lam[1:2], axis=1, keepdims=True))
                - jnp.exp(jnp.sum(lam[2:3] * lam[3:4], axis=1, keepdims=True)) + LAM_INIT)

    def finish(hh, acc, l):
        inv_l = 1.0 / l
        o = (acc[:, :tile] * inv_l[:, :tile]
             - acc[:, tile:] * (lam_full * inv_l[:, tile:]))
        o = o * (lax.rsqrt(jnp.mean(o * o, axis=0, keepdims=True) + EPS) * (1.0 - LAM_INIT))
        o_ref[:, head_cols(hh)] = (o.T * g_ref[:, head_cols(hh)]).astype(o_ref.dtype)

    def interleave(stage_iters):
        pending = list(stage_iters)
        while pending:
            for it in list(pending):
                if next(it, StopIteration) is StopIteration:
                    pending.remove(it)

    @pl.when(bounded_ok)
    def _():
        for hh in range(heads):
            l_scr[hh] = jnp.zeros(l_scr.shape[1:], F32)

        def tile_group_stages(hh, t0, n_tiles, diagonal=False):
            shift = m_scr[hh]
            for r in range(n_tiles):
                t = t0 + r
                c = slope2[hh] * (t * kt_size - qi * tile).astype(F32)
                kt = key_rows(hh, t)
                if diagonal and r == 1:
                    for blk in upper_blocks:
                        s = jnp.dot(kt, qq_scr[hh, :, lanes(blk)], preferred_element_type=F32)
                        s = jnp.where(tri, s, NEG_BIG)
                        p = jnp.exp2(s + (c - shift[:, lanes(blk)]))
                        l_scr[hh, :, lanes(blk)] += jnp.sum(p, axis=0, keepdims=True)
                        p_scr[BOUNDED_GROUP * hh + r, :, lanes(blk)] = p.astype(BF16)
                else:
                    s = jnp.dot(kt, qq_scr[hh, :, :width], preferred_element_type=F32)
                    if diagonal:
                        s = jnp.where(kk <= qpos, s, NEG_BIG)
                    p = jnp.exp2(s + (c - shift))
                    l_scr[hh] += jnp.sum(p, axis=0, keepdims=True)
                    p_scr[BOUNDED_GROUP * hh + r, :, :width] = p.astype(BF16)
                yield
            if diagonal:
                acc_scr[hh, :DA_V_DIM, :width] += jnp.dot(
                    values_t(hh, t0, with_ones=False), p_scr[BOUNDED_GROUP * hh, :, :width],
                    preferred_element_type=F32)
                vt = values_t(hh, t0 + 1, with_ones=False)
                for blk in upper_blocks:
                    acc_scr[hh, :DA_V_DIM, lanes(blk)] += jnp.dot(
                        vt, p_scr[BOUNDED_GROUP * hh + 1, :, lanes(blk)], preferred_element_type=F32)
            else:
                first = BOUNDED_GROUP * hh
                p_all = p_scr[first:first + n_tiles, :, :width].reshape(n_tiles * kt_size, width)
                acc_scr[hh, :DA_V_DIM, :width] += jnp.dot(
                    values_t(hh, t0, n_tiles, with_ones=False), p_all, preferred_element_type=F32)

        def body(i, carry):
            interleave([tile_group_stages(hh, BOUNDED_GROUP * i, BOUNDED_GROUP)
                        for hh in range(heads)])
            return carry

        lax.fori_loop(0, (2 * qi) // BOUNDED_GROUP, body, 0)

        @pl.when(qi % 2 == 1)
        def _():
            interleave([tile_group_stages(hh, 2 * (qi - 1), 2) for hh in range(heads)])

        interleave([tile_group_stages(hh, 2 * qi, 2, diagonal=True) for hh in range(heads)])
        for hh in range(heads):
            finish(hh, acc_scr[hh, :DA_V_DIM, :width], l_scr[hh])

    @pl.when(jnp.logical_not(bounded_ok))
    def _():
        for hh in range(heads):
            m_scr[hh] = jnp.full(m_scr.shape[1:], NEG_BIG, F32)
            p_scr[2 * hh + 1, :, :width] = jnp.zeros((kt_size, width), BF16)

        def scores(hh, t, slot, blocks=None):
            kt = key_rows(hh, t)
            if blocks is None:
                s_scr[2 * hh + slot, :, :width] = jnp.dot(
                    kt, qq_scr[hh, :, :width], preferred_element_type=F32)
            else:
                for blk in blocks:
                    s_scr[2 * hh + slot, :, lanes(blk)] = jnp.dot(
                        kt, qq_scr[hh, :, lanes(blk)], preferred_element_type=F32)

        def weighted_values(hh, t, slot):
            return jnp.dot(values_t(hh, t), p_scr[2 * hh + slot, :, :width],
                           preferred_element_type=F32)

        def step(hh, t, slot, mask=None, prefetch_blocks=None):
            scores(hh, t + 1, 1 - slot, prefetch_blocks)
            c = slope2[hh] * (t * kt_size - qi * tile).astype(F32)
            s = s_scr[2 * hh + slot, :, :width]
            if mask is not None:
                s = jnp.where(mask, s, NEG_BIG)
            m_old = m_scr[hh]
            m_new = jnp.maximum(m_old, jnp.max(s, axis=0, keepdims=True) + c)
            p = jnp.exp2(s - (m_new - c))
            alpha = jnp.exp2(m_old - m_new)
            p_scr[2 * hh + slot, :, :width] = p.astype(BF16)
            m_scr[hh] = m_new
            acc_scr[hh, :, :width] = alpha * (acc_scr[hh, :, :width]
                                              + weighted_values(hh, t - 1, 1 - slot))

        for hh in range(heads):
            scores(hh, 0, 0)

        def body(i, carry):
            for slot in range(2):
                for hh in range(heads):
                    step(hh, 2 * i + slot, slot)
            return carry

        lax.fori_loop(0, qi, body, 0)

        for hh in range(heads):
            step(hh, 2 * qi, 0, mask=kk <= qpos, prefetch_blocks=upper_blocks)

        def last_tile(hh):
            t = 2 * qi + 1
            c = slope2[hh] * kt_size
            acc = acc_scr[hh, :, :width] + weighted_values(hh, t - 1, 0)
            vt = values_t(hh, t)
            parts = []
            for blk in range(n_blocks):
                part = acc[:, lanes(blk)]
                if blk in upper_blocks:
                    s = jnp.where(tri, s_scr[2 * hh + 1, :, lanes(blk)], NEG_BIG)
                    m_old = m_scr[hh, :, lanes(blk)]
                    m_new = jnp.maximum(m_old, jnp.max(s, axis=0, keepdims=True) + c)
                    p = jnp.exp2(s - (m_new - c)).astype(BF16)
                    part = (jnp.exp2(m_old - m_new) * part
                            + jnp.dot(vt, p, preferred_element_type=F32))
                parts.append(part)
            return jnp.concatenate(parts, axis=1)

        for hh in range(heads):
            acc = last_tile(hh)
            finish(hh, acc[:DA_V_DIM, :], acc[DA_V_DIM:DA_V_DIM + 1, :])


def _attention(slopes, kmax, lam, proj3, vt, g, batch, seq, tile, heads):
    kern = functools.partial(_attn_kernel, tile=tile, heads=heads)
    width = heads * DA_V_DIM
    groups = DA_HEADS // heads
    pitch = 2 * tile + LANES
    return pl.pallas_call(
        kern,
        grid=(batch, groups, seq // tile),
        in_specs=[
            pl.BlockSpec(memory_space=pltpu.SMEM),
            pl.BlockSpec(memory_space=pltpu.SMEM),
            pl.BlockSpec((4, DA_HEAD_DIM), lambda b, h, i: (0, 0)),
            pl.BlockSpec((None, tile, width), lambda b, h, i: (b, i, COL_DA_Q * groups + h)),
            pl.BlockSpec((None, seq, width), lambda b, h, i: (b, 0, COL_DA_K * groups + h)),
            pl.BlockSpec((None, width, seq), lambda b, h, i: (b, h, 0)),
            pl.BlockSpec((1, width), lambda b, h, i: (0, h)),
        ],
        out_specs=pl.BlockSpec((None, tile, width), lambda b, h, i: (b, i, h)),
        out_shape=jax.ShapeDtypeStruct((batch, seq, D_MODEL), BF16),
        scratch_shapes=[
            pltpu.VMEM((heads, 2 * DA_V_DIM, pitch), BF16),
            pltpu.VMEM((tile // 2, DA_V_DIM), BF16),
            pltpu.VMEM((2 * heads, tile // 2, pitch), F32),
            pltpu.VMEM((BOUNDED_GROUP * heads, tile // 2, pitch), BF16),
            pltpu.VMEM((heads, DA_V_DIM + ONES_ROWS, pitch), F32),
            pltpu.VMEM((heads, 1, 2 * tile), F32),
            pltpu.VMEM((heads, 1, 2 * tile), F32),
        ],
        compiler_params=pltpu.CompilerParams(
            dimension_semantics=("parallel", "parallel", "arbitrary"),
            vmem_limit_bytes=VMEM_LIMIT),
    )(slopes, kmax, lam, proj3, proj3, vt, g)


def _rms(x, g):
    return x * lax.rsqrt(jnp.mean(x * x, axis=-1, keepdims=True) + EPS) * g


def _mlstm_chunk(r0, chunk, qk_ref, v_ref, og_ref, grow_ref, gcol_ref, cw_ref, cb_ref, brow_ref,
                 bcol_ref, ng_ref, c_scr, n_scr, m_scr, ext_scr, mout_scr):
    pad = SUBLANES
    rows = slice(r0, r0 + chunk)

    ext_scr[pad:pad + chunk, :] = qk_ref[rows, :].astype(F32)
    conv = cb_ref[...] + cw_ref[CONV_WIDTH - 1:CONV_WIDTH, :] * ext_scr[pad:pad + chunk, :]
    for tap in range(1, CONV_WIDTH):
        conv = conv + (cw_ref[CONV_WIDTH - 1 - tap:CONV_WIDTH - tap, :]
                       * ext_scr[pad - tap:pad - tap + chunk, :])
    ext_scr[0:pad, :] = ext_scr[chunk:chunk + pad, :]
    qk = conv * _sigmoid(conv)

    g_rows = grow_ref[:, rows] + brow_ref[...]
    g_cols = gcol_ref[rows, :] + bcol_ref[...]
    r_idx = lax.broadcasted_iota(jnp.int32, (chunk, chunk), 0)
    c_idx = lax.broadcasted_iota(jnp.int32, (chunk, chunk), 1)
    causal = r_idx >= c_idx
    tril = jnp.where(causal, 1.0, 0.0).astype(BF16)
    triu = jnp.where(r_idx <= c_idx, 1.0, 0.0).astype(BF16)

    def bf16_terms(x):
        hi = x.astype(BF16)
        mid = (x - hi.astype(F32)).astype(BF16)
        lo = (x - hi.astype(F32) - mid.astype(F32)).astype(BF16)
        return hi, mid, lo

    b_rows = sum(jnp.dot(term, triu, preferred_element_type=F32)
                 for term in bf16_terms(_log_sigmoid(g_rows)))
    b_cols = sum(jnp.dot(tril, term, preferred_element_type=F32)
                 for term in bf16_terms(_log_sigmoid(g_cols)))

    q_scale = ML_QK_DIM ** -0.5
    for hd in range(ML_HEADS):
        yield
        qf = qk[:, hd * ML_QK_DIM:(hd + 1) * ML_QK_DIM] * q_scale
        kf = qk[:, ML_QK_WIDTH + hd * ML_QK_DIM:ML_QK_WIDTH + (hd + 1) * ML_QK_DIM]
        qb = qf.astype(BF16)
        sl = slice(hd * ML_V_DIM, (hd + 1) * ML_V_DIM)
        vb = v_ref[rows, sl]
        fcol = ML_HEADS + hd
        bt = b_cols[:, fcol:fcol + 1]
        bs = b_rows[fcol:fcol + 1, :]
        i_row = g_rows[hd:hd + 1, :]
        i_col = g_cols[:, hd:hd + 1]
        m_prev = m_scr[hd]
        c_prev = c_scr[hd]
        n_prev = n_scr[hd]

        logd = jnp.where(causal, bt - bs + i_row, NEG_BIG)
        inter = bt + m_prev
        m_t = jnp.maximum(inter, jnp.max(logd, axis=1, keepdims=True))
        dmat = jnp.exp(logd - m_t)
        sc = lax.dot_general(qb, kf.astype(BF16), NT_DIMS, preferred_element_type=F32) * dmat
        w_inter = jnp.exp(inter - m_t)
        num = (w_inter * jnp.dot(qb, c_prev.astype(BF16), preferred_element_type=F32)
               + jnp.dot(sc.astype(BF16), vb, preferred_element_type=F32))
        den = (w_inter * jnp.sum(qf * n_prev, axis=1, keepdims=True)
               + jnp.sum(sc, axis=1, keepdims=True))
        hh = num / jnp.maximum(jnp.abs(den), jnp.exp(-m_t))

        g_last = bt[chunk - 1:chunk, :]
        log_w = g_last - bt + i_col
        m_new = jnp.maximum(g_last + m_prev, jnp.max(log_w, axis=0, keepdims=True))
        kw = kf * jnp.exp(log_w - m_new)
        decay = jnp.exp(g_last + m_prev - m_new)
        c_scr[hd] = decay * c_prev + lax.dot_general(
            kw.astype(BF16), vb, TN_DIMS, preferred_element_type=F32)
        n_scr[hd] = decay * n_prev + jnp.sum(kw, axis=0, keepdims=True)
        m_scr[hd] = m_new

        hn = hh * lax.rsqrt(jnp.mean(hh * hh, axis=1, keepdims=True) + EPS) * ng_ref[:, sl]
        mout_scr[rows, sl] = (_sigmoid(og_ref[rows, sl].astype(F32)) * hn).astype(mout_scr.dtype)


def _mlstm_tail_kernel(qk_ref, v_ref, og_ref, grow_ref, gcol_ref, cw_ref, cb_ref, brow_ref,
                       bcol_ref, ng_ref,
                       x_ref, a_ref, ga_ref, gm_ref, bm_ref, wa_ref, wm_ref, wo_ref,
                       gmlp_ref, w1_ref, w2_ref, gfin_ref,
                       o_ref, c_scr, n_scr, m_scr, ext_scr, mout_scr, *, chunk, tiles_per_seq):
    i = pl.program_id(0)

    @pl.when(i == 0)
    def _():
        mout_scr[...] = jnp.zeros(mout_scr.shape, mout_scr.dtype)

    @pl.when(i % tiles_per_seq == 0)
    def _():
        c_scr[...] = jnp.zeros(c_scr.shape, F32)
        n_scr[...] = jnp.zeros(n_scr.shape, F32)
        m_scr[...] = jnp.zeros(m_scr.shape, F32)
        ext_scr[0:SUBLANES, :] = jnp.zeros((SUBLANES, ext_scr.shape[1]), F32)

    def tail_stages():
        ya = jnp.dot(a_ref[...], wa_ref[...], preferred_element_type=F32)
        ym = jnp.dot(mout_scr[...], wm_ref[...], preferred_element_type=F32)
        gate_a = _sigmoid(ga_ref[...].astype(F32) + bm_ref[:, :D_MODEL])
        gate_m = _sigmoid(gm_ref[...].astype(F32) + bm_ref[:, D_MODEL:])
        merged = (gate_a * ya + gate_m * ym).astype(BF16)
        yield
        x1 = x_ref[...] + jnp.dot(merged, wo_ref[...], preferred_element_type=F32)
        hm = _rms(x1, gmlp_ref[...]).astype(BF16)
        acc = x1
        for c in range(D_FF // FF_BLOCK):
            yield
            cols = slice(c * FF_BLOCK, (c + 1) * FF_BLOCK)
            u = jnp.maximum(jnp.dot(hm, w1_ref[:, cols], preferred_element_type=F32), 0.0)
            acc = acc + jnp.dot((u * u).astype(BF16), w2_ref[cols, :], preferred_element_type=F32)
        o_ref[...] = _rms(acc, gfin_ref[...])

    def mlstm_stages(k):
        for _ in range(2 * k):
            yield
        yield from _mlstm_chunk(k * chunk, chunk, qk_ref, v_ref, og_ref, grow_ref, gcol_ref, cw_ref,
                                cb_ref, brow_ref, bcol_ref, ng_ref, c_scr, n_scr, m_scr,
                                ext_scr, mout_scr)

    pending = [tail_stages()] + [mlstm_stages(k) for k in range(mout_scr.shape[0] // chunk)]
    while pending:
        for stage in list(pending):
            if next(stage, StopIteration) is StopIteration:
                pending.remove(stage)


def _mlstm_tail(proj3, proj, g_rows, g_cols3, conv_w, conv_b, b_row, b_col, norm_g,
                x2, a2, b_merge, wa, wm, wo, g_mlp, w1, w2, g_fin, batch, seq, tm, chunk):
    tokens = batch * seq
    n_tiles = tokens // tm
    tps = seq // tm
    kern = functools.partial(_mlstm_tail_kernel, chunk=chunk, tiles_per_seq=tps)

    def cur(i):
        return jnp.minimum(i, n_tiles - 1)

    def prev(i):
        return jnp.maximum(i - 1, 0)

    def resident(arr):
        return pl.BlockSpec(arr.shape, lambda i: (0,) * arr.ndim, pipeline_mode=pl.Buffered(1))

    def seq_block(col):
        return pl.BlockSpec((None, tm, D_MODEL), lambda i: (cur(i) // tps, cur(i) % tps, col))

    return pl.pallas_call(
        kern,
        grid=(n_tiles + 1,),
        in_specs=[
            seq_block(COL_ML_QK),
            seq_block(COL_ML_V),
            seq_block(COL_ML_O),
            pl.BlockSpec((None, 2 * ML_HEADS, tm), lambda i: (cur(i) // tps, 0, cur(i) % tps)),
            pl.BlockSpec((None, tm, LANES), lambda i: (cur(i) // tps, cur(i) % tps, 0)),
            resident(conv_w), resident(conv_b), resident(b_row), resident(b_col), resident(norm_g),
            pl.BlockSpec((tm, D_MODEL), lambda i: (prev(i), 0)),
            pl.BlockSpec((tm, D_MODEL), lambda i: (prev(i), 0)),
            pl.BlockSpec((tm, D_MODEL), lambda i: (prev(i), COL_MG_A)),
            pl.BlockSpec((tm, D_MODEL), lambda i: (prev(i), COL_MG_M)),
            resident(b_merge), resident(wa), resident(wm), resident(wo), resident(g_mlp),
            resident(w1), resident(w2), resident(g_fin),
        ],
        out_specs=pl.BlockSpec((tm, D_MODEL), lambda i: (prev(i), 0)),
        out_shape=jax.ShapeDtypeStruct((tokens, D_MODEL), F32),
        scratch_shapes=[
            pltpu.VMEM((ML_HEADS, ML_QK_DIM, ML_V_DIM), F32),
            pltpu.VMEM((ML_HEADS, 1, ML_QK_DIM), F32),
            pltpu.VMEM((ML_HEADS, 1, 1), F32),
            pltpu.VMEM((chunk + 2 * SUBLANES, D_MODEL), F32),
            pltpu.VMEM((tm, D_MODEL), BF16),
        ],
        compiler_params=pltpu.CompilerParams(
            dimension_semantics=("arbitrary",), vmem_limit_bytes=VMEM_LIMIT),
    )(proj3, proj3, proj3, g_rows, g_cols3, conv_w, conv_b, b_row, b_col, norm_g,
      x2, a2, proj, proj, b_merge, wa, wm, wo, g_mlp, w1, w2, g_fin)


def kernel(x, norm_mix_g, w_in, b_gates, conv_w, conv_b, lam, da_norm_g, ml_norm_g, b_merge,
           w_branch_a, w_branch_m, w_out, norm_mlp_g, w_ff1, w_ff2, norm_final_g):
    batch, seq, d_model = x.shape
    assert d_model == D_MODEL and w_in.shape == (1, D_MODEL, D_IN)
    assert seq % max(ROW_TILE, ATTN_TILE) == 0 and ROW_TILE % MLSTM_CHUNK == 0
    tokens = batch * seq
    x2 = x.reshape(tokens, D_MODEL)

    w_bf = w_in[0].astype(BF16)
    w_tail = w_bf[:, OFF_ML_O:]

    proj, vt, ifg, ifg_t, k_sq = _in_proj(x2, norm_mix_g, w_bf, w_tail, batch, seq, ROW_TILE)
    kmax = jnp.sqrt(jnp.max(k_sq.reshape(batch, -1, LANES), axis=1))
    proj3 = proj.reshape(batch, seq, N_PROJ_BLOCKS * D_MODEL)

    slopes = 2.0 ** (-8.0 * jnp.arange(1, DA_HEADS + 1, dtype=F32) / DA_HEADS)
    a_out = _attention(slopes, kmax, lam[0], proj3, vt, da_norm_g, batch, seq,
                       ATTN_TILE, ATTN_HEADS_PER_STEP)

    ifg3 = ifg.reshape(batch, seq, LANES)
    b_row = b_gates[0].reshape(2 * ML_HEADS, 1)
    b_col = jnp.pad(b_gates, ((0, 0), (0, LANES - 2 * ML_HEADS)))
    out = _mlstm_tail(proj3, proj, ifg_t, ifg3, conv_w[0], conv_b, b_row, b_col, ml_norm_g,
                      x2, a_out.reshape(tokens, D_MODEL), b_merge,
                      w_branch_a[0].astype(BF16), w_branch_m[0].astype(BF16), w_out[0].astype(BF16),
                      norm_mlp_g, w_ff1[0].astype(BF16), w_ff2[0].astype(BF16),
                      norm_final_g.reshape(1, D_MODEL), batch, seq, ROW_TILE, MLSTM_CHUNK)
    return out.reshape(batch, seq, D_MODEL)
```

```python
import functools
import math

import jax
import jax.numpy as jnp
from jax import lax
from jax.experimental import pallas as pl
from jax.experimental.pallas import tpu as pltpu

F32 = jnp.float32
BF16 = jnp.bfloat16

D_MODEL = 1024
DA_HEADS = 8
DA_HEAD_DIM = 64
DA_V_DIM = 2 * DA_HEAD_DIM
ML_HEADS = 4
ML_V_DIM = D_MODEL // ML_HEADS
ML_QK_DIM = ML_V_DIM // 2
ML_QK_WIDTH = ML_HEADS * ML_QK_DIM
CONV_WIDTH = 4
D_FF = 4 * D_MODEL
EPS = 1e-6
LAM_INIT = 0.8 - 0.6 * math.exp(-0.3 * 0)
NEG_BIG = -1e30
LOG2E = math.log2(math.e)
ONES_ROWS = 16
FF_BLOCK = 512
BOUND_SLACK = 1.02
MAX_SHIFT_SPREAD = 100.0
BOUNDED_GROUP = 4

LANES = 128
SUBLANES = 8
VMEM_LIMIT = 60 * 1024 * 1024

ROW_TILE = 512
ATTN_TILE = 512
ATTN_HEADS_PER_STEP = 4
MLSTM_CHUNK = 256

COL_DA_Q, COL_DA_K, COL_ML_QK, COL_ML_V, COL_ML_O, COL_MG_A, COL_MG_M = range(7)
N_PROJ_BLOCKS = 7

OFF_DA_Q = 0
OFF_DA_K = OFF_DA_Q + D_MODEL
OFF_DA_V = OFF_DA_K + D_MODEL
OFF_ML_Q = OFF_DA_V + D_MODEL
OFF_ML_V = OFF_ML_Q + 2 * ML_QK_WIDTH
OFF_ML_IF = OFF_ML_V + D_MODEL
OFF_ML_O = OFF_ML_IF + 2 * ML_HEADS
D_IN = OFF_ML_O + 3 * D_MODEL

NT_DIMS = (((1,), (1,)), ((), ()))
TN_DIMS = (((0,), (0,)), ((), ()))
TT_DIMS = (((0,), (1,)), ((), ()))


def _sigmoid(x):
    return 1.0 / (1.0 + jnp.exp(-x))


def _log_sigmoid(x):
    return jnp.minimum(x, 0.0) - jnp.log(1.0 + jnp.exp(-jnp.abs(x)))


def _in_proj_kernel(x_ref, g_ref, w_ref, wtail_ref, proj_ref, vt_ref, if_ref, ift_ref, kn_ref):
    x = x_ref[...]
    hb = (x * lax.rsqrt(jnp.mean(x * x, axis=-1, keepdims=True) + EPS) * g_ref[...]).astype(BF16)

    def project(w_cols):
        return jnp.dot(hb, w_cols, preferred_element_type=F32)

    def project_t(w_cols):
        return lax.dot_general(w_cols, hb, TT_DIMS, preferred_element_type=F32)

    proj_ref[:, 0:D_MODEL] = (project(w_ref[:, OFF_DA_Q:OFF_DA_K])
                              * (DA_HEAD_DIM ** -0.5 * LOG2E)).astype(BF16)
    k = project(w_ref[:, OFF_DA_K:OFF_DA_V])
    proj_ref[:, D_MODEL:2 * D_MODEL] = k.astype(BF16)
    r_idx = lax.broadcasted_iota(jnp.int32, (D_MODEL, LANES), 0)
    c_idx = lax.broadcasted_iota(jnp.int32, (D_MODEL, LANES), 1)
    select = jnp.where(r_idx // DA_HEAD_DIM == c_idx, 1.0, 0.0).astype(BF16)
    k_sq = jnp.dot((k * k).astype(BF16), select, preferred_element_type=F32)
    kn_ref[...] = jnp.broadcast_to(jnp.max(k_sq, axis=0, keepdims=True), kn_ref.shape)
    vt_ref[...] = project_t(w_ref[:, OFF_DA_V:OFF_ML_Q]).astype(BF16)
    for blk in range(2):
        src = OFF_ML_Q + blk * D_MODEL
        dst = (COL_ML_QK + blk) * D_MODEL
        proj_ref[:, dst:dst + D_MODEL] = project(w_ref[:, src:src + D_MODEL]).astype(BF16)
    w_if = w_ref[:, OFF_ML_IF:OFF_ML_IF + LANES]
    if_ref[...] = project(w_if)
    ift_ref[...] = project_t(w_if)
    for blk in range(3):
        dst = (COL_ML_O + blk) * D_MODEL
        proj_ref[:, dst:dst + D_MODEL] = project(
            wtail_ref[:, blk * D_MODEL:(blk + 1) * D_MODEL]).astype(BF16)


def _in_proj(x2, g, w_bf, w_tail, batch, seq, tm):
    tokens = batch * seq
    nsb = seq // tm

    def resident(arr):
        return pl.BlockSpec(arr.shape, lambda i: (0, 0), pipeline_mode=pl.Buffered(1))

    return pl.pallas_call(
        _in_proj_kernel,
        grid=(tokens // tm,),
        in_specs=[
            pl.BlockSpec((tm, D_MODEL), lambda i: (i, 0)),
            resident(g),
            resident(w_bf),
            resident(w_tail),
        ],
        out_specs=[
            pl.BlockSpec((tm, N_PROJ_BLOCKS * D_MODEL), lambda i: (i, 0)),
            pl.BlockSpec((None, D_MODEL, tm), lambda i: (i // nsb, 0, i % nsb)),
            pl.BlockSpec((tm, LANES), lambda i: (i, 0)),
            pl.BlockSpec((None, LANES, tm), lambda i: (i // nsb, 0, i % nsb)),
            pl.BlockSpec((SUBLANES, LANES), lambda i: (i, 0)),
        ],
        out_shape=[
            jax.ShapeDtypeStruct((tokens, N_PROJ_BLOCKS * D_MODEL), BF16),
            jax.ShapeDtypeStruct((batch, D_MODEL, seq), BF16),
            jax.ShapeDtypeStruct((tokens, LANES), F32),
            jax.ShapeDtypeStruct((batch, LANES, seq), F32),
            jax.ShapeDtypeStruct((tokens // tm * SUBLANES, LANES), F32),
        ],
        compiler_params=pltpu.CompilerParams(
            dimension_semantics=("parallel",), vmem_limit_bytes=VMEM_LIMIT),
    )(x2, g, w_bf, w_tail)


def _attn_kernel(slopes_ref, kmax_ref, lam_ref, q_ref, k_ref, vt_ref, g_ref, o_ref,
                 qq_scr, pos_scr, s_scr, p_scr, acc_scr, m_scr, l_scr, *, tile, heads):
    batch_idx = pl.program_id(0)
    group = pl.program_id(1)
    qi = pl.program_id(2)
    kt_size = tile // 2
    width = 2 * tile
    slope2 = [slopes_ref[group * heads + hh] * LOG2E for hh in range(heads)]

    def head_cols(hh):
        return slice(hh * DA_V_DIM, (hh + 1) * DA_V_DIM)

    @pl.when(qi == 0)
    def _():
        key_off = lax.broadcasted_iota(jnp.int32, pos_scr.shape, 0).astype(F32)
        lane = lax.broadcasted_iota(jnp.int32, pos_scr.shape, 1)
        pos_scr[...] = jnp.where(lane < 2, key_off, 0.0).astype(BF16)
        for hh in range(heads):
            slope_vec = jnp.full((ONES_ROWS, width), slope2[hh], F32)
            slope_hi = slope_vec.astype(BF16).astype(F32)
            feat = lax.broadcasted_iota(jnp.int32, slope_vec.shape, 0)
            qq_scr[hh, DA_V_DIM:DA_V_DIM + ONES_ROWS, :width] = jnp.where(
                feat == 0, slope_hi, jnp.where(feat == 1, slope_vec - slope_hi, 0.0)).astype(BF16)
            qq_scr[hh, DA_V_DIM + ONES_ROWS:, :width] = jnp.zeros(
                (DA_V_DIM - ONES_ROWS, width), BF16)

    q_lane = lax.broadcasted_iota(jnp.int32, (1, width), 1)
    q_off = jnp.where(q_lane >= tile, q_lane - tile, q_lane).astype(F32)
    spread = jnp.zeros((1, 1), F32)
    for hh in range(heads):
        qt = q_ref[:, head_cols(hh)].T
        d_idx = lax.broadcasted_iota(jnp.int32, qt.shape, 0)
        zero = jnp.zeros_like(qt)
        q0 = jnp.where(d_idx < DA_HEAD_DIM, qt, zero)
        q1 = jnp.where(d_idx >= DA_HEAD_DIM, qt, zero)
        qq_scr[hh, 0:DA_V_DIM, :width] = jnp.concatenate([q0, q1], axis=1)
        qt = qt.astype(F32)
        acc_scr[hh, :, :width] = jnp.zeros((acc_scr.shape[1], width), F32)
        q_sq = qt * qt
        q_norm = jnp.sqrt(jnp.concatenate(
            [jnp.sum(q_sq[:DA_HEAD_DIM], axis=0, keepdims=True),
             jnp.sum(q_sq[DA_HEAD_DIM:], axis=0, keepdims=True)], axis=1))
        head = group * heads + hh
        k_max = jnp.where(q_lane < tile, kmax_ref[batch_idx, 2 * head],
                          kmax_ref[batch_idx, 2 * head + 1])
        qk_bound = q_norm * k_max * BOUND_SLACK + 1.0
        spread = jnp.maximum(spread, jnp.max(qk_bound, axis=1, keepdims=True))
        m_scr[hh] = qk_bound + slope2[hh] * q_off
    bounded_ok = jnp.max(spread) * 2.0 < MAX_SHIFT_SPREAD

    def lanes(blk):
        return slice(blk * kt_size, (blk + 1) * kt_size)

    n_blocks = width // kt_size
    upper_blocks = (1, 3)

    def key_rows(hh, t):
        start = pl.multiple_of(t * kt_size, kt_size)
        return jnp.concatenate([k_ref[pl.ds(start, kt_size), head_cols(hh)], pos_scr[...]], axis=1)

    def values_t(hh, t, n_tiles=1, with_ones=True):
        start = pl.multiple_of(jnp.maximum(t, 0) * kt_size, kt_size)
        vt = vt_ref[head_cols(hh), pl.ds(start, n_tiles * kt_size)]
        if not with_ones:
            return vt
        return jnp.concatenate([vt, jnp.ones((ONES_ROWS, n_tiles * kt_size), BF16)], axis=0)

    kk = lax.broadcasted_iota(jnp.int32, (kt_size, width), 0)
    qpos = lax.broadcasted_iota(jnp.int32, (kt_size, width), 1)
    qpos = jnp.where(qpos >= tile, qpos - tile, qpos)
    tri = (lax.broadcasted_iota(jnp.int32, (kt_size, kt_size), 0)
           <= lax.broadcasted_iota(jnp.int32, (kt_size, kt_size), 1))

    lam = lam_ref[...]
    lam_full = (jnp.exp(jnp.sum(lam[0:1] * lam[1:2], axis=1, keepdims=True))
                - jnp.exp(jnp.sum(lam[2:3] * lam[3:4], axis=1, keepdims=True)) + LAM_INIT)

    def finish(hh, acc, l):
        inv_l = 1.0 / l
        o = (acc[:, :tile] * inv_l[:, :tile]
             - acc[:, tile:] * (lam_full * inv_l[:, tile:]))
        o = o * lax.rsqrt(jnp.mean(o * o, axis=0, keepdims=True) + EPS) * (1.0 - LAM_INIT)
        o_ref[:, head_cols(hh)] = (o.T * g_ref[:, head_cols(hh)]).astype(o_ref.dtype)

    def interleave(stage_iters):
        pending = list(stage_iters)
        while pending:
            for it in list(pending):
                if next(it, StopIteration) is StopIteration:
                    pending.remove(it)

    @pl.when(bounded_ok)
    def _():
        for hh in range(heads):
            l_scr[hh] = jnp.zeros(l_scr.shape[1:], F32)

        def tile_group_stages(hh, t0, n_tiles, diagonal=False):
            shift = m_scr[hh]
            for r in range(n_tiles):
                t = t0 + r
                c = slope2[hh] * (t * kt_size - qi * tile).astype(F32)
                kt = key_rows(hh, t)
                if diagonal and r == 1:
                    for blk in upper_blocks:
                        s = jnp.dot(kt, qq_scr[hh, :, lanes(blk)], preferred_element_type=F32)
                        s = jnp.where(tri, s, NEG_BIG)
                        p = jnp.exp2(s + (c - shift[:, lanes(blk)]))
                        l_scr[hh, :, lanes(blk)] += jnp.sum(p, axis=0, keepdims=True)
                        p_scr[BOUNDED_GROUP * hh + r, :, lanes(blk)] = p.astype(BF16)
                else:
                    s = jnp.dot(kt, qq_scr[hh, :, :width], preferred_element_type=F32)
                    if diagonal:
                        s = jnp.where(kk <= qpos, s, NEG_BIG)
                    p = jnp.exp2(s + (c - shift))
                    l_scr[hh] += jnp.sum(p, axis=0, keepdims=True)
                    p_scr[BOUNDED_GROUP * hh + r, :, :width] = p.astype(BF16)
                yield
            if diagonal:
                acc_scr[hh, :DA_V_DIM, :width] += jnp.dot(
                    values_t(hh, t0, with_ones=False), p_scr[BOUNDED_GROUP * hh, :, :width],
                    preferred_element_type=F32)
                vt = values_t(hh, t0 + 1, with_ones=False)
                for blk in upper_blocks:
                    acc_scr[hh, :DA_V_DIM, lanes(blk)] += jnp.dot(
                        vt, p_scr[BOUNDED_GROUP * hh + 1, :, lanes(blk)], preferred_element_type=F32)
            else:
                first = BOUNDED_GROUP * hh
                p_all = p_scr[first:first + n_tiles, :, :width].reshape(n_tiles * kt_size, width)
                acc_scr[hh, :DA_V_DIM, :width] += jnp.dot(
                    values_t(hh, t0, n_tiles, with_ones=False), p_all, preferred_element_type=F32)

        def body(i, carry):
            interleave([tile_group_stages(hh, BOUNDED_GROUP * i, BOUNDED_GROUP)
                        for hh in range(heads)])
            return carry

        lax.fori_loop(0, (2 * qi) // BOUNDED_GROUP, body, 0)

        @pl.when(qi % 2 == 1)
        def _():
            interleave([tile_group_stages(hh, 2 * (qi - 1), 2) for hh in range(heads)])

        def diagonal_stages(hh):
            yield from tile_group_stages(hh, 2 * qi, 2, diagonal=True)
            finish(hh, acc_scr[hh, :DA_V_DIM, :width], l_scr[hh])

        interleave([diagonal_stages(hh) for hh in range(heads)])

    @pl.when(jnp.logical_not(bounded_ok))
    def _():
        for hh in range(heads):
            m_scr[hh] = jnp.full(m_scr.shape[1:], NEG_BIG, F32)
            p_scr[2 * hh + 1, :, :width] = jnp.zeros((kt_size, width), BF16)

        def scores(hh, t, slot, blocks=None):
            kt = key_rows(hh, t)
            if blocks is None:
                s_scr[2 * hh + slot, :, :width] = jnp.dot(
                    kt, qq_scr[hh, :, :width], preferred_element_type=F32)
            else:
                for blk in blocks:
                    s_scr[2 * hh + slot, :, lanes(blk)] = jnp.dot(
                        kt, qq_scr[hh, :, lanes(blk)], preferred_element_type=F32)

        def weighted_values(hh, t, slot):
            return jnp.dot(values_t(hh, t), p_scr[2 * hh + slot, :, :width],
                           preferred_element_type=F32)

        def step(hh, t, slot, mask=None, prefetch_blocks=None):
            scores(hh, t + 1, 1 - slot, prefetch_blocks)
            c = slope2[hh] * (t * kt_size - qi * tile).astype(F32)
            s = s_scr[2 * hh + slot, :, :width]
            if mask is not None:
                s = jnp.where(mask, s, NEG_BIG)
            m_old = m_scr[hh]
            m_new = jnp.maximum(m_old, jnp.max(s, axis=0, keepdims=True) + c)
            p = jnp.exp2(s - (m_new - c))
            alpha = jnp.exp2(m_old - m_new)
            p_scr[2 * hh + slot, :, :width] = p.astype(BF16)
            m_scr[hh] = m_new
            acc_scr[hh, :, :width] = alpha * (acc_scr[hh, :, :width]
                                              + weighted_values(hh, t - 1, 1 - slot))

        for hh in range(heads):
            scores(hh, 0, 0)

        def body(i, carry):
            for slot in range(2):
                for hh in range(heads):
                    step(hh, 2 * i + slot, slot)
            return carry

        lax.fori_loop(0, qi, body, 0)

        for hh in range(heads):
            step(hh, 2 * qi, 0, mask=kk <= qpos, prefetch_blocks=upper_blocks)

        def last_tile(hh):
            t = 2 * qi + 1
            c = slope2[hh] * kt_size
            acc = acc_scr[hh, :, :width] + weighted_values(hh, t - 1, 0)
            vt = values_t(hh, t)
            parts = []
            for blk in range(n_blocks):
                part = acc[:, lanes(blk)]
                if blk in upper_blocks:
                    s = jnp.where(tri, s_scr[2 * hh + 1, :, lanes(blk)], NEG_BIG)
                    m_old = m_scr[hh, :, lanes(blk)]
                    m_new = jnp.maximum(m_old, jnp.max(s, axis=0, keepdims=True) + c)
                    p = jnp.exp2(s - (m_new - c)).astype(BF16)
                    part = (jnp.exp2(m_old - m_new) * part
                            + jnp.dot(vt, p, preferred_element_type=F32))
                parts.append(part)
            return jnp.concatenate(parts, axis=1)

        for hh in range(heads):
            acc = last_tile(hh)
            finish(hh, acc[:DA_V_DIM, :], acc[DA_V_DIM:DA_V_DIM + 1, :])


def _attention(slopes, kmax, lam, proj3, vt, g, batch, seq, tile, heads):
    kern = functools.partial(_attn_kernel, tile=tile, heads=heads)
    width = heads * DA_V_DIM
    groups = DA_HEADS // heads
    pitch = 2 * tile + LANES
    return pl.pallas_call(
        kern,
        grid=(batch, groups, seq // tile),
        in_specs=[
            pl.BlockSpec(memory_space=pltpu.SMEM),
            pl.BlockSpec(memory_space=pltpu.SMEM),
            pl.BlockSpec((4, DA_HEAD_DIM), lambda b, h, i: (0, 0)),
            pl.BlockSpec((None, tile, width), lambda b, h, i: (b, i, COL_DA_Q * groups + h)),
            pl.BlockSpec((None, seq, width), lambda b, h, i: (b, 0, COL_DA_K * groups + h)),
            pl.BlockSpec((None, width, seq), lambda b, h, i: (b, h, 0)),
            pl.BlockSpec((1, width), lambda b, h, i: (0, h)),
        ],
        out_specs=pl.BlockSpec((None, tile, width), lambda b, h, i: (b, i, h)),
        out_shape=jax.ShapeDtypeStruct((batch, seq, D_MODEL), BF16),
        scratch_shapes=[
            pltpu.VMEM((heads, 2 * DA_V_DIM, pitch), BF16),
            pltpu.VMEM((tile // 2, DA_V_DIM), BF16),
            pltpu.VMEM((2 * heads, tile // 2, pitch), F32),
            pltpu.VMEM((BOUNDED_GROUP * heads, tile // 2, pitch), BF16),
            pltpu.VMEM((heads, DA_V_DIM + ONES_ROWS, pitch), F32),
            pltpu.VMEM((heads, 1, 2 * tile), F32),
            pltpu.VMEM((heads, 1, 2 * tile), F32),
        ],
        compiler_params=pltpu.CompilerParams(
            dimension_semantics=("parallel", "parallel", "arbitrary"),
            vmem_limit_bytes=VMEM_LIMIT),
    )(slopes, kmax, lam, proj3, proj3, vt, g)


def _rms(x, g):
    return x * lax.rsqrt(jnp.mean(x * x, axis=-1, keepdims=True) + EPS) * g


def _mlstm_chunk(r0, chunk, qk_ref, v_ref, og_ref, grow_ref, gcol_ref, cw_ref, cb_ref, brow_ref,
                 bcol_ref, ng_ref, c_scr, n_scr, m_scr, ext_scr, mout_scr):
    pad = SUBLANES
    rows = slice(r0, r0 + chunk)

    ext_scr[pad:pad + chunk, :] = qk_ref[rows, :].astype(F32)
    conv = cb_ref[...] + cw_ref[CONV_WIDTH - 1:CONV_WIDTH, :] * ext_scr[pad:pad + chunk, :]
    for tap in range(1, CONV_WIDTH):
        conv = conv + (cw_ref[CONV_WIDTH - 1 - tap:CONV_WIDTH - tap, :]
                       * ext_scr[pad - tap:pad - tap + chunk, :])
    ext_scr[0:pad, :] = ext_scr[chunk:chunk + pad, :]
    qk = conv * _sigmoid(conv)

    g_rows = grow_ref[:, rows] + brow_ref[...]
    g_cols = gcol_ref[rows, :] + bcol_ref[...]
    r_idx = lax.broadcasted_iota(jnp.int32, (chunk, chunk), 0)
    c_idx = lax.broadcasted_iota(jnp.int32, (chunk, chunk), 1)
    causal = r_idx >= c_idx
    tril = jnp.where(causal, 1.0, 0.0).astype(BF16)
    triu = jnp.where(r_idx <= c_idx, 1.0, 0.0).astype(BF16)

    def bf16_terms(x):
        hi = x.astype(BF16)
        mid = (x - hi.astype(F32)).astype(BF16)
        lo = (x - hi.astype(F32) - mid.astype(F32)).astype(BF16)
        return hi, mid, lo

    b_rows = sum(jnp.dot(term, triu, preferred_element_type=F32)
                 for term in bf16_terms(_log_sigmoid(g_rows)))
    b_cols = sum(jnp.dot(tril, term, preferred_element_type=F32)
                 for term in bf16_terms(_log_sigmoid(g_cols)))

    q_scale = ML_QK_DIM ** -0.5
    for hd in range(ML_HEADS):
        yield
        qf = qk[:, hd * ML_QK_DIM:(hd + 1) * ML_QK_DIM] * q_scale
        kf = qk[:, ML_QK_WIDTH + hd * ML_QK_DIM:ML_QK_WIDTH + (hd + 1) * ML_QK_DIM]
        qb = qf.astype(BF16)
        sl = slice(hd * ML_V_DIM, (hd + 1) * ML_V_DIM)
        vb = v_ref[rows, sl]
        fcol = ML_HEADS + hd
        bt = b_cols[:, fcol:fcol + 1]
        bs = b_rows[fcol:fcol + 1, :]
        i_row = g_rows[hd:hd + 1, :]
        i_col = g_cols[:, hd:hd + 1]
        m_prev = m_scr[hd]
        c_prev = c_scr[hd]
        n_prev = n_scr[hd]

        logd = jnp.where(causal, bt - bs + i_row, NEG_BIG)
        inter = bt + m_prev
        m_t = jnp.maximum(inter, jnp.max(logd, axis=1, keepdims=True))
        dmat = jnp.exp(logd - m_t)
        sc = lax.dot_general(qb, kf.astype(BF16), NT_DIMS, preferred_element_type=F32) * dmat
        w_inter = jnp.exp(inter - m_t)
        num = (w_inter * jnp.dot(qb, c_prev.astype(BF16), preferred_element_type=F32)
               + jnp.dot(sc.astype(BF16), vb, preferred_element_type=F32))
        den = (w_inter * jnp.sum(qf * n_prev, axis=1, keepdims=True)
               + jnp.sum(sc, axis=1, keepdims=True))
        hh = num / jnp.maximum(jnp.abs(den), jnp.exp(-m_t))

        g_last = bt[chunk - 1:chunk, :]
        log_w = g_last - bt + i_col
        m_new = jnp.maximum(g_last + m_prev, jnp.max(log_w, axis=0, keepdims=True))
        kw = kf * jnp.exp(log_w - m_new)
        decay = jnp.exp(g_last + m_prev - m_new)
        c_scr[hd] = decay * c_prev + lax.dot_general(
            kw.astype(BF16), vb, TN_DIMS, preferred_element_type=F32)
        n_scr[hd] = decay * n_prev + jnp.sum(kw, axis=0, keepdims=True)
        m_scr[hd] = m_new

        hn = hh * lax.rsqrt(jnp.mean(hh * hh, axis=1, keepdims=True) + EPS) * ng_ref[:, sl]
        mout_scr[rows, sl] = (_sigmoid(og_ref[rows, sl].astype(F32)) * hn).astype(mout_scr.dtype)


def _mlstm_tail_kernel(qk_ref, v_ref, og_ref, grow_ref, gcol_ref, cw_ref, cb_ref, brow_ref,
                       bcol_ref, ng_ref,
                       x_ref, a_ref, ga_ref, gm_ref, bm_ref, wa_ref, wm_ref, wo_ref,
                       gmlp_ref, w1_ref, w2_ref, gfin_ref,
                       o_ref, c_scr, n_scr, m_scr, ext_scr, mout_scr, *, chunk, tiles_per_seq):
    i = pl.program_id(0)

    @pl.when(i == 0)
    def _():
        mout_scr[...] = jnp.zeros(mout_scr.shape, mout_scr.dtype)

    @pl.when(i % tiles_per_seq == 0)
    def _():
        c_scr[...] = jnp.zeros(c_scr.shape, F32)
        n_scr[...] = jnp.zeros(n_scr.shape, F32)
        m_scr[...] = jnp.zeros(m_scr.shape, F32)
        ext_scr[0:SUBLANES, :] = jnp.zeros((SUBLANES, ext_scr.shape[1]), F32)

    def tail_stages():
        ya = jnp.dot(a_ref[...], wa_ref[...], preferred_element_type=F32)
        ym = jnp.dot(mout_scr[...], wm_ref[...], preferred_element_type=F32)
        gate_a = _sigmoid(ga_ref[...].astype(F32) + bm_ref[:, :D_MODEL])
        gate_m = _sigmoid(gm_ref[...].astype(F32) + bm_ref[:, D_MODEL:])
        merged = (gate_a * ya + gate_m * ym).astype(BF16)
        yield
        x1 = x_ref[...] + jnp.dot(merged, wo_ref[...], preferred_element_type=F32)
        hm = _rms(x1, gmlp_ref[...]).astype(BF16)
        acc = x1
        for c in range(D_FF // FF_BLOCK):
            yield
            cols = slice(c * FF_BLOCK, (c + 1) * FF_BLOCK)
            u = jnp.maximum(jnp.dot(hm, w1_ref[:, cols], preferred_element_type=F32), 0.0)
            acc = acc + jnp.dot((u * u).astype(BF16), w2_ref[cols, :], preferred_element_type=F32)
        o_ref[...] = _rms(acc, gfin_ref[...])

    def mlstm_stages():
        for r0 in range(0, mout_scr.shape[0], chunk):
            yield from _mlstm_chunk(r0, chunk, qk_ref, v_ref, og_ref, grow_ref, gcol_ref, cw_ref,
                                    cb_ref, brow_ref, bcol_ref, ng_ref, c_scr, n_scr, m_scr,
                                    ext_scr, mout_scr)

    pending = [mlstm_stages(), tail_stages()]
    while pending:
        for stage in list(pending):
            if next(stage, StopIteration) is StopIteration:
                pending.remove(stage)


def _mlstm_tail(proj3, proj, g_rows, g_cols3, conv_w, conv_b, b_row, b_col, norm_g,
                x2, a2, b_merge, wa, wm, wo, g_mlp, w1, w2, g_fin, batch, seq, tm, chunk):
    tokens = batch * seq
    n_tiles = tokens // tm
    tps = seq // tm
    kern = functools.partial(_mlstm_tail_kernel, chunk=chunk, tiles_per_seq=tps)

    def cur(i):
        return jnp.minimum(i, n_tiles - 1)

    def prev(i):
        return jnp.maximum(i - 1, 0)

    def resident(arr):
        return pl.BlockSpec(arr.shape, lambda i: (0,) * arr.ndim, pipeline_mode=pl.Buffered(1))

    def seq_block(col):
        return pl.BlockSpec((None, tm, D_MODEL), lambda i: (cur(i) // tps, cur(i) % tps, col))

    return pl.pallas_call(
        kern,
        grid=(n_tiles + 1,),
        in_specs=[
            seq_block(COL_ML_QK),
            seq_block(COL_ML_V),
            seq_block(COL_ML_O),
            pl.BlockSpec((None, 2 * ML_HEADS, tm), lambda i: (cur(i) // tps, 0, cur(i) % tps)),
            pl.BlockSpec((None, tm, LANES), lambda i: (cur(i) // tps, cur(i) % tps, 0)),
            resident(conv_w), resident(conv_b), resident(b_row), resident(b_col), resident(norm_g),
            pl.BlockSpec((tm, D_MODEL), lambda i: (prev(i), 0)),
            pl.BlockSpec((tm, D_MODEL), lambda i: (prev(i), 0)),
            pl.BlockSpec((tm, D_MODEL), lambda i: (prev(i), COL_MG_A)),
            pl.BlockSpec((tm, D_MODEL), lambda i: (prev(i), COL_MG_M)),
            resident(b_merge), resident(wa), resident(wm), resident(wo), resident(g_mlp),
            resident(w1), resident(w2), resident(g_fin),
        ],
        out_specs=pl.BlockSpec((tm, D_MODEL), lambda i: (prev(i), 0)),
        out_shape=jax.ShapeDtypeStruct((tokens, D_MODEL), F32),
        scratch_shapes=[
            pltpu.VMEM((ML_HEADS, ML_QK_DIM, ML_V_DIM), F32),
            pltpu.VMEM((ML_HEADS, 1, ML_QK_DIM), F32),
            pltpu.VMEM((ML_HEADS, 1, 1), F32),
            pltpu.VMEM((chunk + 2 * SUBLANES, D_MODEL), F32),
            pltpu.VMEM((tm, D_MODEL), BF16),
        ],
        compiler_params=pltpu.CompilerParams(
            dimension_semantics=("arbitrary",), vmem_limit_bytes=VMEM_LIMIT),
    )(proj3, proj3, proj3, g_rows, g_cols3, conv_w, conv_b, b_row, b_col, norm_g,
      x2, a2, proj, proj, b_merge, wa, wm, wo, g_mlp, w1, w2, g_fin)


def kernel(x, norm_mix_g, w_in, b_gates, conv_w, conv_b, lam, da_norm_g, ml_norm_g, b_merge,
           w_branch_a, w_branch_m, w_out, norm_mlp_g, w_ff1, w_ff2, norm_final_g):
    batch, seq, d_model = x.shape
    assert d_model == D_MODEL and w_in.shape == (1, D_MODEL, D_IN)
    assert seq % max(ROW_TILE, ATTN_TILE) == 0 and ROW_TILE % MLSTM_CHUNK == 0
    tokens = batch * seq
    x2 = x.reshape(tokens, D_MODEL)

    w_bf = w_in[0].astype(BF16)
    w_tail = w_bf[:, OFF_ML_O:]

    proj, vt, ifg, ifg_t, k_sq = _in_proj(x2, norm_mix_g, w_bf, w_tail, batch, seq, ROW_TILE)
    kmax = jnp.sqrt(jnp.max(k_sq.reshape(batch, -1, LANES), axis=1))
    proj3 = proj.reshape(batch, seq, N_PROJ_BLOCKS * D_MODEL)

    slopes = 2.0 ** (-8.0 * jnp.arange(1, DA_HEADS + 1, dtype=F32) / DA_HEADS)
    a_out = _attention(slopes, kmax, lam[0], proj3, vt, da_norm_g, batch, seq,
                       ATTN_TILE, ATTN_HEADS_PER_STEP)

    ifg3 = ifg.reshape(batch, seq, LANES)
    b_row = b_gates[0].reshape(2 * ML_HEADS, 1)
    b_col = jnp.pad(b_gates, ((0, 0), (0, LANES - 2 * ML_HEADS)))
    out = _mlstm_tail(proj3, proj, ifg_t, ifg3, conv_w[0], conv_b, b_row, b_col, ml_norm_g,
                      x2, a_out.reshape(tokens, D_MODEL), b_merge,
                      w_branch_a[0].astype(BF16), w_branch_m[0].astype(BF16), w_out[0].astype(BF16),
                      norm_mlp_g, w_ff1[0].astype(BF16), w_ff2[0].astype(BF16),
                      norm_final_g.reshape(1, D_MODEL), batch, seq, ROW_TILE, MLSTM_CHUNK)
    return out.reshape(batch, seq, D_MODEL)
```

```python
import functools
import math

import jax
import jax.numpy as jnp
from jax import lax
from jax.experimental import pallas as pl
from jax.experimental.pallas import tpu as pltpu

F32 = jnp.float32
BF16 = jnp.bfloat16

D_MODEL = 1024
DA_HEADS = 8
DA_HEAD_DIM = 64
DA_V_DIM = 2 * DA_HEAD_DIM
ML_HEADS = 4
ML_V_DIM = D_MODEL // ML_HEADS
ML_QK_DIM = ML_V_DIM // 2
ML_QK_WIDTH = ML_HEADS * ML_QK_DIM
CONV_WIDTH = 4
D_FF = 4 * D_MODEL
EPS = 1e-6
LAM_INIT = 0.8 - 0.6 * math.exp(-0.3 * 0)
NEG_BIG = -1e30
LOG2E = math.log2(math.e)
ONES_ROWS = 16
FF_BLOCK = 512
BOUND_SLACK = 1.02
MAX_SHIFT_SPREAD = 100.0
BOUNDED_GROUP = 4

LANES = 128
SUBLANES = 8
VMEM_LIMIT = 60 * 1024 * 1024

ROW_TILE = 512
ATTN_TILE = 512
ATTN_HEADS_PER_STEP = 4
MLSTM_CHUNK = 256

COL_DA_Q, COL_DA_K, COL_ML_QK, COL_ML_V, COL_ML_O, COL_MG_A, COL_MG_M = range(7)
N_PROJ_BLOCKS = 7

OFF_DA_Q = 0
OFF_DA_K = OFF_DA_Q + D_MODEL
OFF_DA_V = OFF_DA_K + D_MODEL
OFF_ML_Q = OFF_DA_V + D_MODEL
OFF_ML_V = OFF_ML_Q + 2 * ML_QK_WIDTH
OFF_ML_IF = OFF_ML_V + D_MODEL
OFF_ML_O = OFF_ML_IF + 2 * ML_HEADS
D_IN = OFF_ML_O + 3 * D_MODEL

NT_DIMS = (((1,), (1,)), ((), ()))
TN_DIMS = (((0,), (0,)), ((), ()))
TT_DIMS = (((0,), (1,)), ((), ()))


def _sigmoid(x):
    return 1.0 / (1.0 + jnp.exp(-x))


def _log_sigmoid(x):
    return jnp.minimum(x, 0.0) - jnp.log(1.0 + jnp.exp(-jnp.abs(x)))


def _in_proj_kernel(x_ref, g_ref, w_ref, wtail_ref, proj_ref, vt_ref, if_ref, ift_ref, kn_ref):
    x = x_ref[...]
    hb = (x * lax.rsqrt(jnp.mean(x * x, axis=-1, keepdims=True) + EPS) * g_ref[...]).astype(BF16)

    def project(w_cols):
        return jnp.dot(hb, w_cols, preferred_element_type=F32)

    def project_t(w_cols):
        return lax.dot_general(w_cols, hb, TT_DIMS, preferred_element_type=F32)

    proj_ref[:, 0:D_MODEL] = (project(w_ref[:, OFF_DA_Q:OFF_DA_K])
                              * (DA_HEAD_DIM ** -0.5 * LOG2E)).astype(BF16)
    k = project(w_ref[:, OFF_DA_K:OFF_DA_V])
    proj_ref[:, D_MODEL:2 * D_MODEL] = k.astype(BF16)
    r_idx = lax.broadcasted_iota(jnp.int32, (D_MODEL, LANES), 0)
    c_idx = lax.broadcasted_iota(jnp.int32, (D_MODEL, LANES), 1)
    select = jnp.where(r_idx // DA_HEAD_DIM == c_idx, 1.0, 0.0).astype(BF16)
    k_sq = jnp.dot((k * k).astype(BF16), select, preferred_element_type=F32)
    kn_ref[...] = jnp.broadcast_to(jnp.max(k_sq, axis=0, keepdims=True), kn_ref.shape)
    vt_ref[...] = project_t(w_ref[:, OFF_DA_V:OFF_ML_Q]).astype(BF16)
    for blk in range(2):
        src = OFF_ML_Q + blk * D_MODEL
        dst = (COL_ML_QK + blk) * D_MODEL
        proj_ref[:, dst:dst + D_MODEL] = project(w_ref[:, src:src + D_MODEL]).astype(BF16)
    w_if = w_ref[:, OFF_ML_IF:OFF_ML_IF + LANES]
    if_ref[...] = project(w_if)
    ift_ref[...] = project_t(w_if)
    for blk in range(3):
        dst = (COL_ML_O + blk) * D_MODEL
        proj_ref[:, dst:dst + D_MODEL] = project(
            wtail_ref[:, blk * D_MODEL:(blk + 1) * D_MODEL]).astype(BF16)


def _in_proj(x2, g, w_bf, w_tail, batch, seq, tm):
    tokens = batch * seq
    nsb = seq // tm

    def resident(arr):
        return pl.BlockSpec(arr.shape, lambda i: (0, 0), pipeline_mode=pl.Buffered(1))

    return pl.pallas_call(
        _in_proj_kernel,
        grid=(tokens // tm,),
        in_specs=[
            pl.BlockSpec((tm, D_MODEL), lambda i: (i, 0)),
            resident(g),
            resident(w_bf),
            resident(w_tail),
        ],
        out_specs=[
            pl.BlockSpec((tm, N_PROJ_BLOCKS * D_MODEL), lambda i: (i, 0)),
            pl.BlockSpec((None, D_MODEL, tm), lambda i: (i // nsb, 0, i % nsb)),
            pl.BlockSpec((tm, LANES), lambda i: (i, 0)),
            pl.BlockSpec((None, LANES, tm), lambda i: (i // nsb, 0, i % nsb)),
            pl.BlockSpec((SUBLANES, LANES), lambda i: (i, 0)),
        ],
        out_shape=[
            jax.ShapeDtypeStruct((tokens, N_PROJ_BLOCKS * D_MODEL), BF16),
            jax.ShapeDtypeStruct((batch, D_MODEL, seq), BF16),
            jax.ShapeDtypeStruct((tokens, LANES), F32),
            jax.ShapeDtypeStruct((batch, LANES, seq), F32),
            jax.ShapeDtypeStruct((tokens // tm * SUBLANES, LANES), F32),
        ],
        compiler_params=pltpu.CompilerParams(
            dimension_semantics=("parallel",), vmem_limit_bytes=VMEM_LIMIT),
    )(x2, g, w_bf, w_tail)


def _attn_kernel(slopes_ref, kmax_ref, lam_ref, q_ref, k_ref, vt_ref, g_ref, o_ref,
                 qq_scr, pos_scr, s_scr, p_scr, acc_scr, m_scr, l_scr, *, tile, heads):
    batch_idx = pl.program_id(0)
    group = pl.program_id(1)
    qi = pl.program_id(2)
    kt_size = tile // 2
    width = 2 * tile
    slope2 = [slopes_ref[group * heads + hh] * LOG2E for hh in range(heads)]

    def head_cols(hh):
        return slice(hh * DA_V_DIM, (hh + 1) * DA_V_DIM)

    @pl.when(qi == 0)
    def _():
        key_off = lax.broadcasted_iota(jnp.int32, pos_scr.shape, 0).astype(F32)
        lane = lax.broadcasted_iota(jnp.int32, pos_scr.shape, 1)
        pos_scr[...] = jnp.where(lane < 2, key_off, 0.0).astype(BF16)
        for hh in range(heads):
            slope_vec = jnp.full((ONES_ROWS, width), slope2[hh], F32)
            slope_hi = slope_vec.astype(BF16).astype(F32)
            feat = lax.broadcasted_iota(jnp.int32, slope_vec.shape, 0)
            qq_scr[hh, DA_V_DIM:DA_V_DIM + ONES_ROWS, :width] = jnp.where(
                feat == 0, slope_hi, jnp.where(feat == 1, slope_vec - slope_hi, 0.0)).astype(BF16)
            qq_scr[hh, DA_V_DIM + ONES_ROWS:, :width] = jnp.zeros(
                (DA_V_DIM - ONES_ROWS, width), BF16)

    q_lane = lax.broadcasted_iota(jnp.int32, (1, width), 1)
    q_off = jnp.where(q_lane >= tile, q_lane - tile, q_lane).astype(F32)
    spread = jnp.zeros((1, 1), F32)
    for hh in range(heads):
        qt = q_ref[:, head_cols(hh)].T
        d_idx = lax.broadcasted_iota(jnp.int32, qt.shape, 0)
        zero = jnp.zeros_like(qt)
        q0 = jnp.where(d_idx < DA_HEAD_DIM, qt, zero)
        q1 = jnp.where(d_idx >= DA_HEAD_DIM, qt, zero)
        qq_scr[hh, 0:DA_V_DIM, :width] = jnp.concatenate([q0, q1], axis=1)
        qt = qt.astype(F32)
        acc_scr[hh, :, :width] = jnp.zeros((acc_scr.shape[1], width), F32)
        q_sq = qt * qt
        q_norm = jnp.sqrt(jnp.concatenate(
            [jnp.sum(q_sq[:DA_HEAD_DIM], axis=0, keepdims=True),
             jnp.sum(q_sq[DA_HEAD_DIM:], axis=0, keepdims=True)], axis=1))
        head = group * heads + hh
        k_max = jnp.where(q_lane < tile, kmax_ref[batch_idx, 2 * head],
                          kmax_ref[batch_idx, 2 * head + 1])
        qk_bound = q_norm * k_max * BOUND_SLACK + 1.0
        spread = jnp.maximum(spread, jnp.max(qk_bound, axis=1, keepdims=True))
        m_scr[hh] = qk_bound + slope2[hh] * q_off
    bounded_ok = jnp.max(spread) * 2.0 < MAX_SHIFT_SPREAD

    def lanes(blk):
        return slice(blk * kt_size, (blk + 1) * kt_size)

    n_blocks = width // kt_size
    upper_blocks = (1, 3)

    def key_rows(hh, t):
        start = pl.multiple_of(t * kt_size, kt_size)
        return jnp.concatenate([k_ref[pl.ds(start, kt_size), head_cols(hh)], pos_scr[...]], axis=1)

    def values_t(hh, t, n_tiles=1, with_ones=True):
        start = pl.multiple_of(jnp.maximum(t, 0) * kt_size, kt_size)
        vt = vt_ref[head_cols(hh), pl.ds(start, n_tiles * kt_size)]
        if not with_ones:
            return vt
        return jnp.concatenate([vt, jnp.ones((ONES_ROWS, n_tiles * kt_size), BF16)], axis=0)

    kk = lax.broadcasted_iota(jnp.int32, (kt_size, width), 0)
    qpos = lax.broadcasted_iota(jnp.int32, (kt_size, width), 1)
    qpos = jnp.where(qpos >= tile, qpos - tile, qpos)
    tri = (lax.broadcasted_iota(jnp.int32, (kt_size, kt_size), 0)
           <= lax.broadcasted_iota(jnp.int32, (kt_size, kt_size), 1))

    lam = lam_ref[...]
    lam_full = (jnp.exp(jnp.sum(lam[0:1] * lam[1:2], axis=1, keepdims=True))
                - jnp.exp(jnp.sum(lam[2:3] * lam[3:4], axis=1, keepdims=True)) + LAM_INIT)

    def finish(hh, acc, l):
        inv_l = 1.0 / l
        o = (acc[:, :tile] * inv_l[:, :tile]
             - acc[:, tile:] * (lam_full * inv_l[:, tile:]))
        o = o * lax.rsqrt(jnp.mean(o * o, axis=0, keepdims=True) + EPS) * (1.0 - LAM_INIT)
        o_ref[:, head_cols(hh)] = (o.T * g_ref[:, head_cols(hh)]).astype(o_ref.dtype)

    def interleave(stage_iters):
        pending = list(stage_iters)
        while pending:
            for it in list(pending):
                if next(it, StopIteration) is StopIteration:
                    pending.remove(it)

    @pl.when(bounded_ok)
    def _():
        for hh in range(heads):
            l_scr[hh] = jnp.zeros(l_scr.shape[1:], F32)

        def tile_group_stages(hh, t0, n_tiles, diagonal=False):
            shift = m_scr[hh]
            for r in range(n_tiles):
                t = t0 + r
                c = slope2[hh] * (t * kt_size - qi * tile).astype(F32)
                kt = key_rows(hh, t)
                if diagonal and r == 1:
                    for blk in upper_blocks:
                        s = jnp.dot(kt, qq_scr[hh, :, lanes(blk)], preferred_element_type=F32)
                        s = jnp.where(tri, s, NEG_BIG)
                        p = jnp.exp2(s + (c - shift[:, lanes(blk)]))
                        l_scr[hh, :, lanes(blk)] += jnp.sum(p, axis=0, keepdims=True)
                        p_scr[BOUNDED_GROUP * hh + r, :, lanes(blk)] = p.astype(BF16)
                else:
                    s = jnp.dot(kt, qq_scr[hh, :, :width], preferred_element_type=F32)
                    if diagonal:
                        s = jnp.where(kk <= qpos, s, NEG_BIG)
                    p = jnp.exp2(s + (c - shift))
                    l_scr[hh] += jnp.sum(p, axis=0, keepdims=True)
                    p_scr[BOUNDED_GROUP * hh + r, :, :width] = p.astype(BF16)
                yield
            if diagonal:
                acc_scr[hh, :DA_V_DIM, :width] += jnp.dot(
                    values_t(hh, t0, with_ones=False), p_scr[BOUNDED_GROUP * hh, :, :width],
                    preferred_element_type=F32)
                vt = values_t(hh, t0 + 1, with_ones=False)
                for blk in upper_blocks:
                    acc_scr[hh, :DA_V_DIM, lanes(blk)] += jnp.dot(
                        vt, p_scr[BOUNDED_GROUP * hh + 1, :, lanes(blk)], preferred_element_type=F32)
            else:
                first = BOUNDED_GROUP * hh
                p_all = p_scr[first:first + n_tiles, :, :width].reshape(n_tiles * kt_size, width)
                acc_scr[hh, :DA_V_DIM, :width] += jnp.dot(
                    values_t(hh, t0, n_tiles, with_ones=False), p_all, preferred_element_type=F32)

        def body(i, carry):
            interleave([tile_group_stages(hh, BOUNDED_GROUP * i, BOUNDED_GROUP)
                        for hh in range(heads)])
            return carry

        lax.fori_loop(0, (2 * qi) // BOUNDED_GROUP, body, 0)

        @pl.when(qi % 2 == 1)
        def _():
            interleave([tile_group_stages(hh, 2 * (qi - 1), 2) for hh in range(heads)])

        def diagonal_stages(hh):
            yield from tile_group_stages(hh, 2 * qi, 2, diagonal=True)
            finish(hh, acc_scr[hh, :DA_V_DIM, :width], l_scr[hh])

        interleave([diagonal_stages(hh) for hh in range(heads)])

    @pl.when(jnp.logical_not(bounded_ok))
    def _():
        for hh in range(heads):
            m_scr[hh] = jnp.full(m_scr.shape[1:], NEG_BIG, F32)
            p_scr[2 * hh + 1, :, :width] = jnp.zeros((kt_size, width), BF16)

        def scores(hh, t, slot, blocks=None):
            kt = key_rows(hh, t)
            if blocks is None:
                s_scr[2 * hh + slot, :, :width] = jnp.dot(
                    kt, qq_scr[hh, :, :width], preferred_element_type=F32)
            else:
                for blk in blocks:
                    s_scr[2 * hh + slot, :, lanes(blk)] = jnp.dot(
                        kt, qq_scr[hh, :, lanes(blk)], preferred_element_type=F32)

        def weighted_values(hh, t, slot):
            return jnp.dot(values_t(hh, t), p_scr[2 * hh + slot, :, :width],
                           preferred_element_type=F32)

        def step(hh, t, slot, mask=None, prefetch_blocks=None):
            scores(hh, t + 1, 1 - slot, prefetch_blocks)
            c = slope2[hh] * (t * kt_size - qi * tile).astype(F32)
            s = s_scr[2 * hh + slot, :, :width]
            if mask is not None:
                s = jnp.where(mask, s, NEG_BIG)
            m_old = m_scr[hh]
            m_new = jnp.maximum(m_old, jnp.max(s, axis=0, keepdims=True) + c)
            p = jnp.exp2(s - (m_new - c))
            alpha = jnp.exp2(m_old - m_new)
            p_scr[2 * hh + slot, :, :width] = p.astype(BF16)
            m_scr[hh] = m_new
            acc_scr[hh, :, :width] = alpha * (acc_scr[hh, :, :width]
                                              + weighted_values(hh, t - 1, 1 - slot))

        for hh in range(heads):
            scores(hh, 0, 0)

        def body(i, carry):
            for slot in range(2):
                for hh in range(heads):
                    step(hh, 2 * i + slot, slot)
            return carry

        lax.fori_loop(0, qi, body, 0)

        for hh in range(heads):
            step(hh, 2 * qi, 0, mask=kk <= qpos, prefetch_blocks=upper_blocks)

        def last_tile(hh):
            t = 2 * qi + 1
            c = slope2[hh] * kt_size
            acc = acc_scr[hh, :, :width] + weighted_values(hh, t - 1, 0)
            vt = values_t(hh, t)
            parts = []
            for blk in range(n_blocks):
                part = acc[:, lanes(blk)]
                if blk in upper_blocks:
                    s = jnp.where(tri, s_scr[2 * hh + 1, :, lanes(blk)], NEG_BIG)
                    m_old = m_scr[hh, :, lanes(blk)]
                    m_new = jnp.maximum(m_old, jnp.max(s, axis=0, keepdims=True) + c)
                    p = jnp.exp2(s - (m_new - c)).astype(BF16)
                    part = (jnp.exp2(m_old - m_new) * part
                            + jnp.dot(vt, p, preferred_element_type=F32))
                parts.append(part)
            return jnp.concatenate(parts, axis=1)

        for hh in range(heads):
            acc = last_tile(hh)
            finish(hh, acc[:DA_V_DIM, :], acc[DA_V_DIM:DA_V_DIM + 1, :])


def _attention(slopes, kmax, lam, proj3, vt, g, batch, seq, tile, heads):
    kern = functools.partial(_attn_kernel, tile=tile, heads=heads)
    width = heads * DA_V_DIM
    groups = DA_HEADS // heads
    pitch = 2 * tile + LANES
    return pl.pallas_call(
        kern,
        grid=(batch, groups, seq // tile),
        in_specs=[
            pl.BlockSpec(memory_space=pltpu.SMEM),
            pl.BlockSpec(memory_space=pltpu.SMEM),
            pl.BlockSpec((4, DA_HEAD_DIM), lambda b, h, i: (0, 0)),
            pl.BlockSpec((None, tile, width), lambda b, h, i: (b, i, COL_DA_Q * groups + h)),
            pl.BlockSpec((None, seq, width), lambda b, h, i: (b, 0, COL_DA_K * groups + h)),
            pl.BlockSpec((None, width, seq), lambda b, h, i: (b, h, 0)),
            pl.BlockSpec((1, width), lambda b, h, i: (0, h)),
        ],
        out_specs=pl.BlockSpec((None, tile, width), lambda b, h, i: (b, i, h)),
        out_shape=jax.ShapeDtypeStruct((batch, seq, D_MODEL), BF16),
        scratch_shapes=[
            pltpu.VMEM((heads, 2 * DA_V_DIM, pitch), BF16),
            pltpu.VMEM((tile // 2, DA_V_DIM), BF16),
            pltpu.VMEM((2 * heads, tile // 2, pitch), F32),
            pltpu.VMEM((BOUNDED_GROUP * heads, tile // 2, pitch), BF16),
            pltpu.VMEM((heads, DA_V_DIM + ONES_ROWS, pitch), F32),
            pltpu.VMEM((heads, 1, 2 * tile), F32),
            pltpu.VMEM((heads, 1, 2 * tile), F32),
        ],
        compiler_params=pltpu.CompilerParams(
            dimension_semantics=("parallel", "parallel", "arbitrary"),
            vmem_limit_bytes=VMEM_LIMIT),
    )(slopes, kmax, lam, proj3, proj3, vt, g)


def _rms(x, g):
    return x * lax.rsqrt(jnp.mean(x * x, axis=-1, keepdims=True) + EPS) * g


def _mlstm_chunk(r0, chunk, qk_ref, v_ref, og_ref, grow_ref, gcol_ref, cw_ref, cb_ref, brow_ref,
                 bcol_ref, ng_ref, c_scr, n_scr, m_scr, ext_scr, mout_scr):
    pad = SUBLANES
    rows = slice(r0, r0 + chunk)

    ext_scr[pad:pad + chunk, :] = qk_ref[rows, :].astype(F32)
    conv = cb_ref[...] + cw_ref[CONV_WIDTH - 1:CONV_WIDTH, :] * ext_scr[pad:pad + chunk, :]
    for tap in range(1, CONV_WIDTH):
        conv = conv + (cw_ref[CONV_WIDTH - 1 - tap:CONV_WIDTH - tap, :]
                       * ext_scr[pad - tap:pad - tap + chunk, :])
    ext_scr[0:pad, :] = ext_scr[chunk:chunk + pad, :]
    qk = conv * _sigmoid(conv)

    g_rows = grow_ref[:, rows] + brow_ref[...]
    g_cols = gcol_ref[rows, :] + bcol_ref[...]
    r_idx = lax.broadcasted_iota(jnp.int32, (chunk, chunk), 0)
    c_idx = lax.broadcasted_iota(jnp.int32, (chunk, chunk), 1)
    causal = r_idx >= c_idx
    tril = jnp.where(causal, 1.0, 0.0).astype(BF16)
    triu = jnp.where(r_idx <= c_idx, 1.0, 0.0).astype(BF16)

    def bf16_terms(x):
        hi = x.astype(BF16)
        mid = (x - hi.astype(F32)).astype(BF16)
        lo = (x - hi.astype(F32) - mid.astype(F32)).astype(BF16)
        return hi, mid, lo

    b_rows = sum(jnp.dot(term, triu, preferred_element_type=F32)
                 for term in bf16_terms(_log_sigmoid(g_rows)))
    b_cols = sum(jnp.dot(tril, term, preferred_element_type=F32)
                 for term in bf16_terms(_log_sigmoid(g_cols)))

    q_scale = ML_QK_DIM ** -0.5
    for hd in range(ML_HEADS):
        yield
        qf = qk[:, hd * ML_QK_DIM:(hd + 1) * ML_QK_DIM] * q_scale
        kf = qk[:, ML_QK_WIDTH + hd * ML_QK_DIM:ML_QK_WIDTH + (hd + 1) * ML_QK_DIM]
        qb = qf.astype(BF16)
        sl = slice(hd * ML_V_DIM, (hd + 1) * ML_V_DIM)
        vb = v_ref[rows, sl]
        fcol = ML_HEADS + hd
        bt = b_cols[:, fcol:fcol + 1]
        bs = b_rows[fcol:fcol + 1, :]
        i_row = g_rows[hd:hd + 1, :]
        i_col = g_cols[:, hd:hd + 1]
        m_prev = m_scr[hd]
        c_prev = c_scr[hd]
        n_prev = n_scr[hd]

        logd = jnp.where(causal, bt - bs + i_row, NEG_BIG)
        inter = bt + m_prev
        m_t = jnp.maximum(inter, jnp.max(logd, axis=1, keepdims=True))
        dmat = jnp.exp(logd - m_t)
        sc = lax.dot_general(qb, kf.astype(BF16), NT_DIMS, preferred_element_type=F32) * dmat
        w_inter = jnp.exp(inter - m_t)
        num = (w_inter * jnp.dot(qb, c_prev.astype(BF16), preferred_element_type=F32)
               + jnp.dot(sc.astype(BF16), vb, preferred_element_type=F32))
        den = (w_inter * jnp.sum(qf * n_prev, axis=1, keepdims=True)
               + jnp.sum(sc, axis=1, keepdims=True))
        hh = num / jnp.maximum(jnp.abs(den), jnp.exp(-m_t))

        g_last = bt[chunk - 1:chunk, :]
        log_w = g_last - bt + i_col
        m_new = jnp.maximum(g_last + m_prev, jnp.max(log_w, axis=0, keepdims=True))
        kw = kf * jnp.exp(log_w - m_new)
        decay = jnp.exp(g_last + m_prev - m_new)
        c_scr[hd] = decay * c_prev + lax.dot_general(
            kw.astype(BF16), vb, TN_DIMS, preferred_element_type=F32)
        n_scr[hd] = decay * n_prev + jnp.sum(kw, axis=0, keepdims=True)
        m_scr[hd] = m_new

        hn = hh * lax.rsqrt(jnp.mean(hh * hh, axis=1, keepdims=True) + EPS) * ng_ref[:, sl]
        mout_scr[rows, sl] = (_sigmoid(og_ref[rows, sl].astype(F32)) * hn).astype(mout_scr.dtype)


def _mlstm_tail_kernel(qk_ref, v_ref, og_ref, grow_ref, gcol_ref, cw_ref, cb_ref, brow_ref,
                       bcol_ref, ng_ref,
                       x_ref, a_ref, ga_ref, gm_ref, bm_ref, wa_ref, wm_ref, wo_ref,
                       gmlp_ref, w1_ref, w2_ref, gfin_ref,
                       o_ref, c_scr, n_scr, m_scr, ext_scr, mout_scr, *, chunk, tiles_per_seq):
    i = pl.program_id(0)

    @pl.when(i == 0)
    def _():
        mout_scr[...] = jnp.zeros(mout_scr.shape, mout_scr.dtype)

    @pl.when(i % tiles_per_seq == 0)
    def _():
        c_scr[...] = jnp.zeros(c_scr.shape, F32)
        n_scr[...] = jnp.zeros(n_scr.shape, F32)
        m_scr[...] = jnp.zeros(m_scr.shape, F32)
        ext_scr[0:SUBLANES, :] = jnp.zeros((SUBLANES, ext_scr.shape[1]), F32)

    def tail_stages():
        ya = jnp.dot(a_ref[...], wa_ref[...], preferred_element_type=F32)
        ym = jnp.dot(mout_scr[...], wm_ref[...], preferred_element_type=F32)
        gate_a = _sigmoid(ga_ref[...].astype(F32) + bm_ref[:, :D_MODEL])
        gate_m = _sigmoid(gm_ref[...].astype(F32) + bm_ref[:, D_MODEL:])
        merged = (gate_a * ya + gate_m * ym).astype(BF16)
        yield
        x1 = x_ref[...] + jnp.dot(merged, wo_ref[...], preferred_element_type=F32)
        hm = _rms(x1, gmlp_ref[...]).astype(BF16)
        acc = x1
        for c in range(D_FF // FF_BLOCK):
            yield
            cols = slice(c * FF_BLOCK, (c + 1) * FF_BLOCK)
            u = jnp.maximum(jnp.dot(hm, w1_ref[:, cols], preferred_element_type=F32), 0.0)
            acc = acc + jnp.dot((u * u).astype(BF16), w2_ref[cols, :], preferred_element_type=F32)
        o_ref[...] = _rms(acc, gfin_ref[...])

    def mlstm_stages():
        for r0 in range(0, mout_scr.shape[0], chunk):
            yield from _mlstm_chunk(r0, chunk, qk_ref, v_ref, og_ref, grow_ref, gcol_ref, cw_ref,
                                    cb_ref, brow_ref, bcol_ref, ng_ref, c_scr, n_scr, m_scr,
                                    ext_scr, mout_scr)

    pending = [mlstm_stages(), tail_stages()]
    while pending:
        for stage in list(pending):
            if next(stage, StopIteration) is StopIteration:
                pending.remove(stage)


def _mlstm_tail(proj3, proj, g_rows, g_cols3, conv_w, conv_b, b_row, b_col, norm_g,
                x2, a2, b_merge, wa, wm, wo, g_mlp, w1, w2, g_fin, batch, seq, tm, chunk):
    tokens = batch * seq
    n_tiles = tokens // tm
    tps = seq // tm
    kern = functools.partial(_mlstm_tail_kernel, chunk=chunk, tiles_per_seq=tps)

    def cur(i):
        return jnp.minimum(i, n_tiles - 1)

    def prev(i):
        return jnp.maximum(i - 1, 0)

    def resident(arr):
        return pl.BlockSpec(arr.shape, lambda i: (0,) * arr.ndim, pipeline_mode=pl.Buffered(1))

    def seq_block(col):
        return pl.BlockSpec((None, tm, D_MODEL), lambda i: (cur(i) // tps, cur(i) % tps, col))

    return pl.pallas_call(
        kern,
        grid=(n_tiles + 1,),
        in_specs=[
            seq_block(COL_ML_QK),
            seq_block(COL_ML_V),
            seq_block(COL_ML_O),
            pl.BlockSpec((None, 2 * ML_HEADS, tm), lambda i: (cur(i) // tps, 0, cur(i) % tps)),
            pl.BlockSpec((None, tm, LANES), lambda i: (cur(i) // tps, cur(i) % tps, 0)),
            resident(conv_w), resident(conv_b), resident(b_row), resident(b_col), resident(norm_g),
            pl.BlockSpec((tm, D_MODEL), lambda i: (prev(i), 0)),
            pl.BlockSpec((tm, D_MODEL), lambda i: (prev(i), 0)),
            pl.BlockSpec((tm, D_MODEL), lambda i: (prev(i), COL_MG_A)),
            pl.BlockSpec((tm, D_MODEL), lambda i: (prev(i), COL_MG_M)),
            resident(b_merge), resident(wa), resident(wm), resident(wo), resident(g_mlp),
            resident(w1), resident(w2), resident(g_fin),
        ],
        out_specs=pl.BlockSpec((tm, D_MODEL), lambda i: (prev(i), 0)),
        out_shape=jax.ShapeDtypeStruct((tokens, D_MODEL), F32),
        scratch_shapes=[
            pltpu.VMEM((ML_HEADS, ML_QK_DIM, ML_V_DIM), F32),
            pltpu.VMEM((ML_HEADS, 1, ML_QK_DIM), F32),
            pltpu.VMEM((ML_HEADS, 1, 1), F32),
            pltpu.VMEM((chunk + 2 * SUBLANES, D_MODEL), F32),
            pltpu.VMEM((tm, D_MODEL), BF16),
        ],
        compiler_params=pltpu.CompilerParams(
            dimension_semantics=("arbitrary",), vmem_limit_bytes=VMEM_LIMIT),
    )(proj3, proj3, proj3, g_rows, g_cols3, conv_w, conv_b, b_row, b_col, norm_g,
      x2, a2, proj, proj, b_merge, wa, wm, wo, g_mlp, w1, w2, g_fin)


def kernel(x, norm_mix_g, w_in, b_gates, conv_w, conv_b, lam, da_norm_g, ml_norm_g, b_merge,
           w_branch_a, w_branch_m, w_out, norm_mlp_g, w_ff1, w_ff2, norm_final_g):
    batch, seq, d_model = x.shape
    assert d_model == D_MODEL and w_in.shape == (1, D_MODEL, D_IN)
    assert seq % max(ROW_TILE, ATTN_TILE) == 0 and ROW_TILE % MLSTM_CHUNK == 0
    tokens = batch * seq
    x2 = x.reshape(tokens, D_MODEL)

    w_bf = w_in[0][:, :OFF_ML_IF + LANES].astype(BF16)
    w_tail = w_in[0][:, OFF_ML_O:].astype(BF16)

    proj, vt, ifg, ifg_t, k_sq = _in_proj(x2, norm_mix_g, w_bf, w_tail, batch, seq, ROW_TILE)
    kmax = jnp.sqrt(jnp.max(k_sq.reshape(batch, -1, LANES), axis=1))
    proj3 = proj.reshape(batch, seq, N_PROJ_BLOCKS * D_MODEL)

    slopes = 2.0 ** (-8.0 * jnp.arange(1, DA_HEADS + 1, dtype=F32) / DA_HEADS)
    a_out = _attention(slopes, kmax, lam[0], proj3, vt, da_norm_g, batch, seq,
                       ATTN_TILE, ATTN_HEADS_PER_STEP)

    ifg3 = ifg.reshape(batch, seq, LANES)
    b_row = b_gates[0].reshape(2 * ML_HEADS, 1)
    b_col = jnp.pad(b_gates, ((0, 0), (0, LANES - 2 * ML_HEADS)))
    out = _mlstm_tail(proj3, proj, ifg_t, ifg3, conv_w[0], conv_b, b_row, b_col, ml_norm_g,
                      x2, a_out.reshape(tokens, D_MODEL), b_merge,
                      w_branch_a[0].astype(BF16), w_branch_m[0].astype(BF16), w_out[0].astype(BF16),
                      norm_mlp_g, w_ff1[0].astype(BF16), w_ff2[0].astype(BF16),
                      norm_final_g.reshape(1, D_MODEL), batch, seq, ROW_TILE, MLSTM_CHUNK)
    return out.reshape(batch, seq, D_MODEL)
```

```python
import functools
import math

import jax
import jax.numpy as jnp
from jax import lax
from jax.experimental import pallas as pl
from jax.experimental.pallas import tpu as pltpu

F32 = jnp.float32
BF16 = jnp.bfloat16

D_MODEL = 1024
DA_HEADS = 8
DA_HEAD_DIM = 64
DA_V_DIM = 2 * DA_HEAD_DIM
ML_HEADS = 4
ML_V_DIM = D_MODEL // ML_HEADS
ML_QK_DIM = ML_V_DIM // 2
ML_QK_WIDTH = ML_HEADS * ML_QK_DIM
CONV_WIDTH = 4
D_FF = 4 * D_MODEL
EPS = 1e-6
LAM_INIT = 0.8 - 0.6 * math.exp(-0.3 * 0)
NEG_BIG = -1e30
LOG2E = math.log2(math.e)
ONES_ROWS = 16
FF_BLOCK = 512
BOUND_SLACK = 1.02
MAX_SHIFT_SPREAD = 100.0
BOUNDED_GROUP = 4

LANES = 128
SUBLANES = 8
VMEM_LIMIT = 60 * 1024 * 1024

ROW_TILE = 512
ATTN_TILE = 512
ATTN_HEADS_PER_STEP = 4
MLSTM_CHUNK = 256

COL_DA_Q, COL_DA_K, COL_ML_QK, COL_ML_V, COL_ML_O, COL_MG_A, COL_MG_M = range(7)
N_PROJ_BLOCKS = 7

OFF_DA_Q = 0
OFF_DA_K = OFF_DA_Q + D_MODEL
OFF_DA_V = OFF_DA_K + D_MODEL
OFF_ML_Q = OFF_DA_V + D_MODEL
OFF_ML_V = OFF_ML_Q + 2 * ML_QK_WIDTH
OFF_ML_IF = OFF_ML_V + D_MODEL
OFF_ML_O = OFF_ML_IF + 2 * ML_HEADS
D_IN = OFF_ML_O + 3 * D_MODEL

NT_DIMS = (((1,), (1,)), ((), ()))
TN_DIMS = (((0,), (0,)), ((), ()))
TT_DIMS = (((0,), (1,)), ((), ()))


def _sigmoid(x):
    return 1.0 / (1.0 + jnp.exp(-x))


def _log_sigmoid(x):
    return jnp.minimum(x, 0.0) - jnp.log(1.0 + jnp.exp(-jnp.abs(x)))


def _in_proj_kernel(x_ref, g_ref, w_ref, wtail_ref, proj_ref, vt_ref, if_ref, ift_ref, kn_ref):
    x = x_ref[...]
    hb = (x * lax.rsqrt(jnp.mean(x * x, axis=-1, keepdims=True) + EPS) * g_ref[...]).astype(BF16)

    def project(w_cols):
        return jnp.dot(hb, w_cols, preferred_element_type=F32)

    def project_t(w_cols):
        return lax.dot_general(w_cols, hb, TT_DIMS, preferred_element_type=F32)

    proj_ref[:, 0:D_MODEL] = (project(w_ref[:, OFF_DA_Q:OFF_DA_K])
                              * (DA_HEAD_DIM ** -0.5 * LOG2E)).astype(BF16)
    k = project(w_ref[:, OFF_DA_K:OFF_DA_V])
    proj_ref[:, D_MODEL:2 * D_MODEL] = k.astype(BF16)
    r_idx = lax.broadcasted_iota(jnp.int32, (D_MODEL, LANES), 0)
    c_idx = lax.broadcasted_iota(jnp.int32, (D_MODEL, LANES), 1)
    select = jnp.where(r_idx // DA_HEAD_DIM == c_idx, 1.0, 0.0).astype(BF16)
    k_sq = jnp.dot((k * k).astype(BF16), select, preferred_element_type=F32)
    kn_ref[...] = jnp.broadcast_to(jnp.max(k_sq, axis=0, keepdims=True), kn_ref.shape)
    vt_ref[...] = project_t(w_ref[:, OFF_DA_V:OFF_ML_Q]).astype(BF16)
    for blk in range(2):
        src = OFF_ML_Q + blk * D_MODEL
        dst = (COL_ML_QK + blk) * D_MODEL
        proj_ref[:, dst:dst + D_MODEL] = project(w_ref[:, src:src + D_MODEL]).astype(BF16)
    w_if = w_ref[:, OFF_ML_IF:OFF_ML_IF + LANES]
    if_ref[...] = project(w_if)
    ift_ref[...] = project_t(w_if)
    for blk in range(3):
        dst = (COL_ML_O + blk) * D_MODEL
        proj_ref[:, dst:dst + D_MODEL] = project(
            wtail_ref[:, blk * D_MODEL:(blk + 1) * D_MODEL]).astype(BF16)


def _in_proj(x2, g, w_bf, w_tail, batch, seq, tm):
    tokens = batch * seq
    nsb = seq // tm

    def resident(arr):
        return pl.BlockSpec(arr.shape, lambda i: (0, 0), pipeline_mode=pl.Buffered(1))

    return pl.pallas_call(
        _in_proj_kernel,
        grid=(tokens // tm,),
        in_specs=[
            pl.BlockSpec((tm, D_MODEL), lambda i: (i, 0)),
            resident(g),
            resident(w_bf),
            resident(w_tail),
        ],
        out_specs=[
            pl.BlockSpec((tm, N_PROJ_BLOCKS * D_MODEL), lambda i: (i, 0)),
            pl.BlockSpec((None, D_MODEL, tm), lambda i: (i // nsb, 0, i % nsb)),
            pl.BlockSpec((tm, LANES), lambda i: (i, 0)),
            pl.BlockSpec((None, LANES, tm), lambda i: (i // nsb, 0, i % nsb)),
            pl.BlockSpec((SUBLANES, LANES), lambda i: (i, 0)),
        ],
        out_shape=[
            jax.ShapeDtypeStruct((tokens, N_PROJ_BLOCKS * D_MODEL), BF16),
            jax.ShapeDtypeStruct((batch, D_MODEL, seq), BF16),
            jax.ShapeDtypeStruct((tokens, LANES), F32),
            jax.ShapeDtypeStruct((batch, LANES, seq), F32),
            jax.ShapeDtypeStruct((tokens // tm * SUBLANES, LANES), F32),
        ],
        compiler_params=pltpu.CompilerParams(
            dimension_semantics=("parallel",), vmem_limit_bytes=VMEM_LIMIT),
    )(x2, g, w_bf, w_tail)


def _attn_kernel(slopes_ref, kmax_ref, lam_ref, q_ref, k_ref, vt_ref, g_ref, o_ref,
                 qq_scr, pos_scr, s_scr, p_scr, acc_scr, m_scr, l_scr, *, tile, heads):
    batch_idx = pl.program_id(0)
    group = pl.program_id(1)
    qi = pl.program_id(2)
    kt_size = tile // 2
    width = 2 * tile
    slope2 = [slopes_ref[group * heads + hh] * LOG2E for hh in range(heads)]

    def head_cols(hh):
        return slice(hh * DA_V_DIM, (hh + 1) * DA_V_DIM)

    @pl.when(qi == 0)
    def _():
        key_off = lax.broadcasted_iota(jnp.int32, pos_scr.shape, 0).astype(F32)
        lane = lax.broadcasted_iota(jnp.int32, pos_scr.shape, 1)
        pos_scr[...] = jnp.where(lane < 2, key_off, 0.0).astype(BF16)
        for hh in range(heads):
            slope_vec = jnp.full((ONES_ROWS, width), slope2[hh], F32)
            slope_hi = slope_vec.astype(BF16).astype(F32)
            feat = lax.broadcasted_iota(jnp.int32, slope_vec.shape, 0)
            qq_scr[hh, DA_V_DIM:DA_V_DIM + ONES_ROWS, :width] = jnp.where(
                feat == 0, slope_hi, jnp.where(feat == 1, slope_vec - slope_hi, 0.0)).astype(BF16)
            qq_scr[hh, DA_V_DIM + ONES_ROWS:, :width] = jnp.zeros(
                (DA_V_DIM - ONES_ROWS, width), BF16)

    q_lane = lax.broadcasted_iota(jnp.int32, (1, width), 1)
    q_off = jnp.where(q_lane >= tile, q_lane - tile, q_lane).astype(F32)
    spread = jnp.zeros((1, 1), F32)
    for hh in range(heads):
        qt = q_ref[:, head_cols(hh)].T
        d_idx = lax.broadcasted_iota(jnp.int32, qt.shape, 0)
        zero = jnp.zeros_like(qt)
        q0 = jnp.where(d_idx < DA_HEAD_DIM, qt, zero)
        q1 = jnp.where(d_idx >= DA_HEAD_DIM, qt, zero)
        qq_scr[hh, 0:DA_V_DIM, :width] = jnp.concatenate([q0, q1], axis=1)
        qt = qt.astype(F32)
        acc_scr[hh, :, :width] = jnp.zeros((acc_scr.shape[1], width), F32)
        q_sq = qt * qt
        q_norm = jnp.sqrt(jnp.concatenate(
            [jnp.sum(q_sq[:DA_HEAD_DIM], axis=0, keepdims=True),
             jnp.sum(q_sq[DA_HEAD_DIM:], axis=0, keepdims=True)], axis=1))
        head = group * heads + hh
        k_max = jnp.where(q_lane < tile, kmax_ref[batch_idx, 2 * head],
                          kmax_ref[batch_idx, 2 * head + 1])
        qk_bound = q_norm * k_max * BOUND_SLACK + 1.0
        spread = jnp.maximum(spread, jnp.max(qk_bound, axis=1, keepdims=True))
        m_scr[hh] = qk_bound + slope2[hh] * q_off
    bounded_ok = jnp.max(spread) * 2.0 < MAX_SHIFT_SPREAD

    def lanes(blk):
        return slice(blk * kt_size, (blk + 1) * kt_size)

    n_blocks = width // kt_size
    upper_blocks = (1, 3)

    def key_rows(hh, t):
        start = pl.multiple_of(t * kt_size, kt_size)
        return jnp.concatenate([k_ref[pl.ds(start, kt_size), head_cols(hh)], pos_scr[...]], axis=1)

    def values_t(hh, t, n_tiles=1, with_ones=True):
        start = pl.multiple_of(jnp.maximum(t, 0) * kt_size, kt_size)
        vt = vt_ref[head_cols(hh), pl.ds(start, n_tiles * kt_size)]
        if not with_ones:
            return vt
        return jnp.concatenate([vt, jnp.ones((ONES_ROWS, n_tiles * kt_size), BF16)], axis=0)

    kk = lax.broadcasted_iota(jnp.int32, (kt_size, width), 0)
    qpos = lax.broadcasted_iota(jnp.int32, (kt_size, width), 1)
    qpos = jnp.where(qpos >= tile, qpos - tile, qpos)
    tri = (lax.broadcasted_iota(jnp.int32, (kt_size, kt_size), 0)
           <= lax.broadcasted_iota(jnp.int32, (kt_size, kt_size), 1))

    lam = lam_ref[...]
    lam_full = (jnp.exp(jnp.sum(lam[0:1] * lam[1:2], axis=1, keepdims=True))
                - jnp.exp(jnp.sum(lam[2:3] * lam[3:4], axis=1, keepdims=True)) + LAM_INIT)

    def finish(hh, acc, l):
        inv_l = 1.0 / l
        o = (acc[:, :tile] * inv_l[:, :tile]
             - acc[:, tile:] * (lam_full * inv_l[:, tile:]))
        o = o * lax.rsqrt(jnp.mean(o * o, axis=0, keepdims=True) + EPS) * (1.0 - LAM_INIT)
        o_ref[:, head_cols(hh)] = (o.T * g_ref[:, head_cols(hh)]).astype(o_ref.dtype)

    def interleave(stage_iters):
        pending = list(stage_iters)
        while pending:
            for it in list(pending):
                if next(it, StopIteration) is StopIteration:
                    pending.remove(it)

    @pl.when(bounded_ok)
    def _():
        for hh in range(heads):
            l_scr[hh] = jnp.zeros(l_scr.shape[1:], F32)

        def tile_group_stages(hh, t0, n_tiles, diagonal=False):
            shift = m_scr[hh]
            for r in range(n_tiles):
                t = t0 + r
                c = slope2[hh] * (t * kt_size - qi * tile).astype(F32)
                kt = key_rows(hh, t)
                if diagonal and r == 1:
                    for blk in upper_blocks:
                        s = jnp.dot(kt, qq_scr[hh, :, lanes(blk)], preferred_element_type=F32)
                        s = jnp.where(tri, s, NEG_BIG)
                        p = jnp.exp2(s + (c - shift[:, lanes(blk)]))
                        l_scr[hh, :, lanes(blk)] += jnp.sum(p, axis=0, keepdims=True)
                        p_scr[BOUNDED_GROUP * hh + r, :, lanes(blk)] = p.astype(BF16)
                else:
                    s = jnp.dot(kt, qq_scr[hh, :, :width], preferred_element_type=F32)
                    if diagonal:
                        s = jnp.where(kk <= qpos, s, NEG_BIG)
                    p = jnp.exp2(s + (c - shift))
                    l_scr[hh] += jnp.sum(p, axis=0, keepdims=True)
                    p_scr[BOUNDED_GROUP * hh + r, :, :width] = p.astype(BF16)
                yield
            if diagonal:
                acc_scr[hh, :DA_V_DIM, :width] += jnp.dot(
                    values_t(hh, t0, with_ones=False), p_scr[BOUNDED_GROUP * hh, :, :width],
                    preferred_element_type=F32)
                vt = values_t(hh, t0 + 1, with_ones=False)
                for blk in upper_blocks:
                    acc_scr[hh, :DA_V_DIM, lanes(blk)] += jnp.dot(
                        vt, p_scr[BOUNDED_GROUP * hh + 1, :, lanes(blk)], preferred_element_type=F32)
            else:
                first = BOUNDED_GROUP * hh
                p_all = p_scr[first:first + n_tiles, :, :width].reshape(n_tiles * kt_size, width)
                acc_scr[hh, :DA_V_DIM, :width] += jnp.dot(
                    values_t(hh, t0, n_tiles, with_ones=False), p_all, preferred_element_type=F32)

        def body(i, carry):
            interleave([tile_group_stages(hh, BOUNDED_GROUP * i, BOUNDED_GROUP)
                        for hh in range(heads)])
            return carry

        lax.fori_loop(0, (2 * qi) // BOUNDED_GROUP, body, 0)

        @pl.when(qi % 2 == 1)
        def _():
            interleave([tile_group_stages(hh, 2 * (qi - 1), 2) for hh in range(heads)])

        def diagonal_stages(hh):
            yield from tile_group_stages(hh, 2 * qi, 2, diagonal=True)
            finish(hh, acc_scr[hh, :DA_V_DIM, :width], l_scr[hh])

        interleave([diagonal_stages(hh) for hh in range(heads)])

    @pl.when(jnp.logical_not(bounded_ok))
    def _():
        for hh in range(heads):
            m_scr[hh] = jnp.full(m_scr.shape[1:], NEG_BIG, F32)
            p_scr[2 * hh + 1, :, :width] = jnp.zeros((kt_size, width), BF16)

        def scores(hh, t, slot, blocks=None):
            kt = key_rows(hh, t)
            if blocks is None:
                s_scr[2 * hh + slot, :, :width] = jnp.dot(
                    kt, qq_scr[hh, :, :width], preferred_element_type=F32)
            else:
                for blk in blocks:
                    s_scr[2 * hh + slot, :, lanes(blk)] = jnp.dot(
                        kt, qq_scr[hh, :, lanes(blk)], preferred_element_type=F32)

        def weighted_values(hh, t, slot):
            return jnp.dot(values_t(hh, t), p_scr[2 * hh + slot, :, :width],
                           preferred_element_type=F32)

        def step(hh, t, slot, mask=None, prefetch_blocks=None):
            scores(hh, t + 1, 1 - slot, prefetch_blocks)
            c = slope2[hh] * (t * kt_size - qi * tile).astype(F32)
            s = s_scr[2 * hh + slot, :, :width]
            if mask is not None:
                s = jnp.where(mask, s, NEG_BIG)
            m_old = m_scr[hh]
            m_new = jnp.maximum(m_old, jnp.max(s, axis=0, keepdims=True) + c)
            p = jnp.exp2(s - (m_new - c))
            alpha = jnp.exp2(m_old - m_new)
            p_scr[2 * hh + slot, :, :width] = p.astype(BF16)
            m_scr[hh] = m_new
            acc_scr[hh, :, :width] = alpha * (acc_scr[hh, :, :width]
                                              + weighted_values(hh, t - 1, 1 - slot))

        for hh in range(heads):
            scores(hh, 0, 0)

        def body(i, carry):
            for slot in range(2):
                for hh in range(heads):
                    step(hh, 2 * i + slot, slot)
            return carry

        lax.fori_loop(0, qi, body, 0)

        for hh in range(heads):
            step(hh, 2 * qi, 0, mask=kk <= qpos, prefetch_blocks=upper_blocks)

        def last_tile(hh):
            t = 2 * qi + 1
            c = slope2[hh] * kt_size
            acc = acc_scr[hh, :, :width] + weighted_values(hh, t - 1, 0)
            vt = values_t(hh, t)
            parts = []
            for blk in range(n_blocks):
                part = acc[:, lanes(blk)]
                if blk in upper_blocks:
                    s = jnp.where(tri, s_scr[2 * hh + 1, :, lanes(blk)], NEG_BIG)
                    m_old = m_scr[hh, :, lanes(blk)]
                    m_new = jnp.maximum(m_old, jnp.max(s, axis=0, keepdims=True) + c)
                    p = jnp.exp2(s - (m_new - c)).astype(BF16)
                    part = (jnp.exp2(m_old - m_new) * part
                            + jnp.dot(vt, p, preferred_element_type=F32))
                parts.append(part)
            return jnp.concatenate(parts, axis=1)

        for hh in range(heads):
            acc = last_tile(hh)
            finish(hh, acc[:DA_V_DIM, :], acc[DA_V_DIM:DA_V_DIM + 1, :])


def _attention(slopes, kmax, lam, proj3, vt, g, batch, seq, tile, heads):
    kern = functools.partial(_attn_kernel, tile=tile, heads=heads)
    width = heads * DA_V_DIM
    groups = DA_HEADS // heads
    pitch = 2 * tile + LANES
    return pl.pallas_call(
        kern,
        grid=(batch, groups, seq // tile),
        in_specs=[
            pl.BlockSpec(memory_space=pltpu.SMEM),
            pl.BlockSpec(memory_space=pltpu.SMEM),
            pl.BlockSpec((4, DA_HEAD_DIM), lambda b, h, i: (0, 0)),
            pl.BlockSpec((None, tile, width), lambda b, h, i: (b, i, COL_DA_Q * groups + h)),
            pl.BlockSpec((None, seq, width), lambda b, h, i: (b, 0, COL_DA_K * groups + h)),
            pl.BlockSpec((None, width, seq), lambda b, h, i: (b, h, 0)),
            pl.BlockSpec((1, width), lambda b, h, i: (0, h)),
        ],
        out_specs=pl.BlockSpec((None, tile, width), lambda b, h, i: (b, i, h)),
        out_shape=jax.ShapeDtypeStruct((batch, seq, D_MODEL), BF16),
        scratch_shapes=[
            pltpu.VMEM((heads, 2 * DA_V_DIM, pitch), BF16),
            pltpu.VMEM((tile // 2, DA_V_DIM), BF16),
            pltpu.VMEM((2 * heads, tile // 2, pitch), F32),
            pltpu.VMEM((BOUNDED_GROUP * heads, tile // 2, pitch), BF16),
            pltpu.VMEM((heads, DA_V_DIM + ONES_ROWS, pitch), F32),
            pltpu.VMEM((heads, 1, 2 * tile), F32),
            pltpu.VMEM((heads, 1, 2 * tile), F32),
        ],
        compiler_params=pltpu.CompilerParams(
            dimension_semantics=("parallel", "parallel", "arbitrary"),
            vmem_limit_bytes=VMEM_LIMIT),
    )(slopes, kmax, lam, proj3, proj3, vt, g)


def _rms(x, g):
    return x * lax.rsqrt(jnp.mean(x * x, axis=-1, keepdims=True) + EPS) * g


def _mlstm_chunk(r0, chunk, qk_ref, v_ref, og_ref, grow_ref, gcol_ref, cw_ref, cb_ref, brow_ref,
                 bcol_ref, ng_ref, c_scr, n_scr, m_scr, ext_scr, mout_scr):
    pad = SUBLANES
    rows = slice(r0, r0 + chunk)

    ext_scr[pad:pad + chunk, :] = qk_ref[rows, :].astype(F32)
    conv = cb_ref[...] + cw_ref[CONV_WIDTH - 1:CONV_WIDTH, :] * ext_scr[pad:pad + chunk, :]
    for tap in range(1, CONV_WIDTH):
        conv = conv + (cw_ref[CONV_WIDTH - 1 - tap:CONV_WIDTH - tap, :]
                       * ext_scr[pad - tap:pad - tap + chunk, :])
    ext_scr[0:pad, :] = ext_scr[chunk:chunk + pad, :]
    qk = conv * _sigmoid(conv)

    g_rows = grow_ref[:, rows] + brow_ref[...]
    g_cols = gcol_ref[rows, :] + bcol_ref[...]
    r_idx = lax.broadcasted_iota(jnp.int32, (chunk, chunk), 0)
    c_idx = lax.broadcasted_iota(jnp.int32, (chunk, chunk), 1)
    causal = r_idx >= c_idx
    tril = jnp.where(causal, 1.0, 0.0).astype(BF16)
    triu = jnp.where(r_idx <= c_idx, 1.0, 0.0).astype(BF16)

    def bf16_terms(x):
        hi = x.astype(BF16)
        mid = (x - hi.astype(F32)).astype(BF16)
        lo = (x - hi.astype(F32) - mid.astype(F32)).astype(BF16)
        return hi, mid, lo

    b_rows = sum(jnp.dot(term, triu, preferred_element_type=F32)
                 for term in bf16_terms(_log_sigmoid(g_rows)))
    b_cols = sum(jnp.dot(tril, term, preferred_element_type=F32)
                 for term in bf16_terms(_log_sigmoid(g_cols)))

    q_scale = ML_QK_DIM ** -0.5
    for hd in range(ML_HEADS):
        yield
        qf = qk[:, hd * ML_QK_DIM:(hd + 1) * ML_QK_DIM] * q_scale
        kf = qk[:, ML_QK_WIDTH + hd * ML_QK_DIM:ML_QK_WIDTH + (hd + 1) * ML_QK_DIM]
        qb = qf.astype(BF16)
        sl = slice(hd * ML_V_DIM, (hd + 1) * ML_V_DIM)
        vb = v_ref[rows, sl]
        fcol = ML_HEADS + hd
        bt = b_cols[:, fcol:fcol + 1]
        bs = b_rows[fcol:fcol + 1, :]
        i_row = g_rows[hd:hd + 1, :]
        i_col = g_cols[:, hd:hd + 1]
        m_prev = m_scr[hd]
        c_prev = c_scr[hd]
        n_prev = n_scr[hd]

        logd = jnp.where(causal, bt - bs + i_row, NEG_BIG)
        inter = bt + m_prev
        m_t = jnp.maximum(inter, jnp.max(logd, axis=1, keepdims=True))
        dmat = jnp.exp(logd - m_t)
        sc = lax.dot_general(qb, kf.astype(BF16), NT_DIMS, preferred_element_type=F32) * dmat
        w_inter = jnp.exp(inter - m_t)
        num = (w_inter * jnp.dot(qb, c_prev.astype(BF16), preferred_element_type=F32)
               + jnp.dot(sc.astype(BF16), vb, preferred_element_type=F32))
        den = (w_inter * jnp.sum(qf * n_prev, axis=1, keepdims=True)
               + jnp.sum(sc, axis=1, keepdims=True))
        hh = num / jnp.maximum(jnp.abs(den), jnp.exp(-m_t))

        g_last = bt[chunk - 1:chunk, :]
        log_w = g_last - bt + i_col
        m_new = jnp.maximum(g_last + m_prev, jnp.max(log_w, axis=0, keepdims=True))
        kw = kf * jnp.exp(log_w - m_new)
        decay = jnp.exp(g_last + m_prev - m_new)
        c_scr[hd] = decay * c_prev + lax.dot_general(
            kw.astype(BF16), vb, TN_DIMS, preferred_element_type=F32)
        n_scr[hd] = decay * n_prev + jnp.sum(kw, axis=0, keepdims=True)
        m_scr[hd] = m_new

        hn = hh * lax.rsqrt(jnp.mean(hh * hh, axis=1, keepdims=True) + EPS) * ng_ref[:, sl]
        mout_scr[rows, sl] = (_sigmoid(og_ref[rows, sl].astype(F32)) * hn).astype(mout_scr.dtype)


def _mlstm_tail_kernel(qk_ref, v_ref, og_ref, grow_ref, gcol_ref, cw_ref, cb_ref, brow_ref,
                       bcol_ref, ng_ref,
                       x_ref, a_ref, ga_ref, gm_ref, bm_ref, wa_ref, wm_ref, wo_ref,
                       gmlp_ref, w1_ref, w2_ref, gfin_ref,
                       o_ref, c_scr, n_scr, m_scr, ext_scr, mout_scr, u_scr, *, chunk, tiles_per_seq):
    i = pl.program_id(0)

    @pl.when(i == 0)
    def _():
        mout_scr[...] = jnp.zeros(mout_scr.shape, mout_scr.dtype)

    @pl.when(i % tiles_per_seq == 0)
    def _():
        c_scr[...] = jnp.zeros(c_scr.shape, F32)
        n_scr[...] = jnp.zeros(n_scr.shape, F32)
        m_scr[...] = jnp.zeros(m_scr.shape, F32)
        ext_scr[0:SUBLANES, :] = jnp.zeros((SUBLANES, ext_scr.shape[1]), F32)

    def tail_stages():
        ya = jnp.dot(a_ref[...], wa_ref[...], preferred_element_type=F32)
        ym = jnp.dot(mout_scr[...], wm_ref[...], preferred_element_type=F32)
        gate_a = _sigmoid(ga_ref[...].astype(F32) + bm_ref[:, :D_MODEL])
        gate_m = _sigmoid(gm_ref[...].astype(F32) + bm_ref[:, D_MODEL:])
        merged = (gate_a * ya + gate_m * ym).astype(BF16)
        yield
        x1 = x_ref[...] + jnp.dot(merged, wo_ref[...], preferred_element_type=F32)
        hm = _rms(x1, gmlp_ref[...]).astype(BF16)
        acc = x1
        half = u_scr.shape[1]
        for h0 in range(0, D_FF, half):
            for c0 in range(0, half, FF_BLOCK):
                yield
                u = jnp.maximum(jnp.dot(hm, w1_ref[:, h0 + c0:h0 + c0 + FF_BLOCK],
                                        preferred_element_type=F32), 0.0)
                u_scr[:, c0:c0 + FF_BLOCK] = (u * u).astype(BF16)
            yield
            acc = acc + jnp.dot(u_scr[...], w2_ref[h0:h0 + half, :], preferred_element_type=F32)
        o_ref[...] = _rms(acc, gfin_ref[...])

    def mlstm_stages():
        for r0 in range(0, mout_scr.shape[0], chunk):
            yield from _mlstm_chunk(r0, chunk, qk_ref, v_ref, og_ref, grow_ref, gcol_ref, cw_ref,
                                    cb_ref, brow_ref, bcol_ref, ng_ref, c_scr, n_scr, m_scr,
                                    ext_scr, mout_scr)

    pending = [mlstm_stages(), tail_stages()]
    while pending:
        for stage in list(pending):
            if next(stage, StopIteration) is StopIteration:
                pending.remove(stage)


def _mlstm_tail(proj3, proj, g_rows, g_cols3, conv_w, conv_b, b_row, b_col, norm_g,
                x2, a2, b_merge, wa, wm, wo, g_mlp, w1, w2, g_fin, batch, seq, tm, chunk):
    tokens = batch * seq
    n_tiles = tokens // tm
    tps = seq // tm
    kern = functools.partial(_mlstm_tail_kernel, chunk=chunk, tiles_per_seq=tps)

    def cur(i):
        return jnp.minimum(i, n_tiles - 1)

    def prev(i):
        return jnp.maximum(i - 1, 0)

    def resident(arr):
        return pl.BlockSpec(arr.shape, lambda i: (0,) * arr.ndim, pipeline_mode=pl.Buffered(1))

    def seq_block(col):
        return pl.BlockSpec((None, tm, D_MODEL), lambda i: (cur(i) // tps, cur(i) % tps, col))

    return pl.pallas_call(
        kern,
        grid=(n_tiles + 1,),
        in_specs=[
            seq_block(COL_ML_QK),
            seq_block(COL_ML_V),
            seq_block(COL_ML_O),
            pl.BlockSpec((None, 2 * ML_HEADS, tm), lambda i: (cur(i) // tps, 0, cur(i) % tps)),
            pl.BlockSpec((None, tm, LANES), lambda i: (cur(i) // tps, cur(i) % tps, 0)),
            resident(conv_w), resident(conv_b), resident(b_row), resident(b_col), resident(norm_g),
            pl.BlockSpec((tm, D_MODEL), lambda i: (prev(i), 0)),
            pl.BlockSpec((tm, D_MODEL), lambda i: (prev(i), 0)),
            pl.BlockSpec((tm, D_MODEL), lambda i: (prev(i), COL_MG_A)),
            pl.BlockSpec((tm, D_MODEL), lambda i: (prev(i), COL_MG_M)),
            resident(b_merge), resident(wa), resident(wm), resident(wo), resident(g_mlp),
            resident(w1), resident(w2), resident(g_fin),
        ],
        out_specs=pl.BlockSpec((tm, D_MODEL), lambda i: (prev(i), 0)),
        out_shape=jax.ShapeDtypeStruct((tokens, D_MODEL), F32),
        scratch_shapes=[
            pltpu.VMEM((ML_HEADS, ML_QK_DIM, ML_V_DIM), F32),
            pltpu.VMEM((ML_HEADS, 1, ML_QK_DIM), F32),
            pltpu.VMEM((ML_HEADS, 1, 1), F32),
            pltpu.VMEM((chunk + 2 * SUBLANES, D_MODEL), F32),
            pltpu.VMEM((tm, D_MODEL), BF16),
            pltpu.VMEM((tm, D_FF // 2), BF16),
        ],
        compiler_params=pltpu.CompilerParams(
            dimension_semantics=("arbitrary",), vmem_limit_bytes=VMEM_LIMIT),
    )(proj3, proj3, proj3, g_rows, g_cols3, conv_w, conv_b, b_row, b_col, norm_g,
      x2, a2, proj, proj, b_merge, wa, wm, wo, g_mlp, w1, w2, g_fin)


def kernel(x, norm_mix_g, w_in, b_gates, conv_w, conv_b, lam, da_norm_g, ml_norm_g, b_merge,
           w_branch_a, w_branch_m, w_out, norm_mlp_g, w_ff1, w_ff2, norm_final_g):
    batch, seq, d_model = x.shape
    assert d_model == D_MODEL and w_in.shape == (1, D_MODEL, D_IN)
    assert seq % max(ROW_TILE, ATTN_TILE) == 0 and ROW_TILE % MLSTM_CHUNK == 0
    tokens = batch * seq
    x2 = x.reshape(tokens, D_MODEL)

    w_bf = w_in[0].astype(BF16)
    w_tail = w_bf[:, OFF_ML_O:]

    proj, vt, ifg, ifg_t, k_sq = _in_proj(x2, norm_mix_g, w_bf, w_tail, batch, seq, ROW_TILE)
    kmax = jnp.sqrt(jnp.max(k_sq.reshape(batch, -1, LANES), axis=1))
    proj3 = proj.reshape(batch, seq, N_PROJ_BLOCKS * D_MODEL)

    slopes = 2.0 ** (-8.0 * jnp.arange(1, DA_HEADS + 1, dtype=F32) / DA_HEADS)
    a_out = _attention(slopes, kmax, lam[0], proj3, vt, da_norm_g, batch, seq,
                       ATTN_TILE, ATTN_HEADS_PER_STEP)

    ifg3 = ifg.reshape(batch, seq, LANES)
    b_row = b_gates[0].reshape(2 * ML_HEADS, 1)
    b_col = jnp.pad(b_gates, ((0, 0), (0, LANES - 2 * ML_HEADS)))
    out = _mlstm_tail(proj3, proj, ifg_t, ifg3, conv_w[0], conv_b, b_row, b_col, ml_norm_g,
                      x2, a_out.reshape(tokens, D_MODEL), b_merge,
                      w_branch_a[0].astype(BF16), w_branch_m[0].astype(BF16), w_out[0].astype(BF16),
                      norm_mlp_g, w_ff1[0].astype(BF16), w_ff2[0].astype(BF16),
                      norm_final_g.reshape(1, D_MODEL), batch, seq, ROW_TILE, MLSTM_CHUNK)
    return out.reshape(batch, seq, D_MODEL)
```
